```python
import jax
import jax.numpy as jnp
from jax import lax
import numpy as np

D_MODEL = 2048
BATCH = 4
SEQ = 2048
DEPTH = 1

HEAD_DIM = 128
NSA_HEADS = 8
NSA_KV_HEADS = 2
NSA_GROUP = NSA_HEADS // NSA_KV_HEADS
CMP_BLOCK = 32
CMP_STRIDE = 16
CMP_HIDDEN = 256
SEL_BLOCK = 64
SEL_TOP = 8
WINDOW = 512
MLA_HEADS = 8
MLA_Q_RANK = 384
MLA_KV_RANK = 256
MLA_NOPE_DIM = 128
MLA_ROPE_DIM = 64
MLA_V_DIM = 128
ROPE_THETA = 10000.0
D_FF = 5632
Q_BLOCK = 128
EPS = 1e-6
NEG = -1e30
FORCE_SCORE = 1e4

NSA_WIDTH = NSA_HEADS * HEAD_DIM
MLA_WIDTH = MLA_HEADS * MLA_V_DIM
MIX_WIDTH = NSA_WIDTH + MLA_WIDTH
NSA_KV_WIDTH = NSA_KV_HEADS * HEAD_DIM
MLA_QK_DIM = MLA_NOPE_DIM + MLA_ROPE_DIM
IN_SPLITS = (NSA_WIDTH,) + (NSA_KV_WIDTH,) * 6 + (NSA_HEADS * 3, MLA_Q_RANK, MLA_KV_RANK, MLA_ROPE_DIM)
IN_DIM = sum(IN_SPLITS)

kernel_name = "hybrid_nsa_mla_macaron_layer"


def rms_norm(x, g):
    xf = x.astype(jnp.float32)
    y = xf * lax.rsqrt(jnp.mean(xf * xf, axis=-1, keepdims=True) + EPS)
    return (y * g.astype(jnp.float32)).astype(x.dtype)


def swiglu(x, w_gate, w_up, w_down):
    return (jax.nn.silu(x @ w_gate) * (x @ w_up)) @ w_down


def masked_softmax(scores, mask):
    s = jnp.where(mask, scores.astype(jnp.float32), NEG)
    p = jax.nn.softmax(s, axis=-1)
    return jnp.where(mask, p, 0.0)


def alibi_slopes(n):
    return 2.0 ** (-8.0 * jnp.arange(1, n + 1, dtype=jnp.float32) / n)


def rope(x, pos):
    half = x.shape[-1] // 2
    inv = ROPE_THETA ** (-jnp.arange(half, dtype=jnp.float32) / half)
    ang = pos.astype(jnp.float32)[:, None] * inv[None, :]
    cos = jnp.cos(ang)[None, :, None, :]
    sin = jnp.sin(ang)[None, :, None, :]
    x1 = x[..., :half].astype(jnp.float32)
    x2 = x[..., half:].astype(jnp.float32)
    return jnp.concatenate([x1 * cos - x2 * sin, x2 * cos + x1 * sin], axis=-1).astype(x.dtype)


def compress_blocks(kv, cmp_idx, pos_emb, w1, w2):
    B = kv.shape[0]
    n_cmp = cmp_idx.shape[0]
    blocks = kv[:, cmp_idx] + pos_emb[None, None, :, None, :]
    flat = blocks.transpose(0, 1, 3, 2, 4).reshape(B, n_cmp, NSA_KV_HEADS, CMP_BLOCK * HEAD_DIM)
    return jax.nn.gelu(flat @ w1) @ w2


def nsa_mixer(q, k_cmp, v_cmp, k_sel, v_sel, k_win, v_win, gates, q_gain, k_gains,
              cmp_pos_k, cmp_w1_k, cmp_w2_k, cmp_pos_v, cmp_w1_v, cmp_w2_v):
    B, S = q.shape[0], q.shape[1]
    H, G, R, D = NSA_HEADS, NSA_KV_HEADS, NSA_GROUP, HEAD_DIM
    slopes = alibi_slopes(H).reshape(G, R)
    scale = D ** -0.5
    q = rms_norm(q.reshape(B, S, H, D), q_gain).reshape(B, S, G, R, D)
    gates = gates.reshape(B, S, G, R, 3)

    n_cmp = (S - CMP_BLOCK) // CMP_STRIDE + 1
    cmp_idx = np.arange(n_cmp)[:, None] * CMP_STRIDE + np.arange(CMP_BLOCK)[None, :]
    kc = rms_norm(compress_blocks(k_cmp.reshape(B, S, G, D), cmp_idx, cmp_pos_k, cmp_w1_k, cmp_w2_k), k_gains[0])
    vc = compress_blocks(v_cmp.reshape(B, S, G, D), cmp_idx, cmp_pos_v, cmp_w1_v, cmp_w2_v)
    cmp_end = jnp.asarray(cmp_idx[:, -1], jnp.int32)
    cmp_centre = jnp.asarray(cmp_idx[:, 0] + 0.5 * (CMP_BLOCK - 1), jnp.float32)

    n_sel = S // SEL_BLOCK
    top = min(SEL_TOP, n_sel)
    sel_start_np = np.arange(n_sel) * SEL_BLOCK
    overlap = jnp.asarray((cmp_idx[:, :1] < sel_start_np[None, :] + SEL_BLOCK)
                          & (cmp_idx[:, -1:] >= sel_start_np[None, :]), jnp.float32)
    sel_start = jnp.asarray(sel_start_np, jnp.int32)
    ks = rms_norm(k_sel.reshape(B, S, G, D), k_gains[1])
    ks_blocks = ks.reshape(B, n_sel, SEL_BLOCK, G, D).transpose(0, 3, 1, 2, 4)
    vs_blocks = v_sel.reshape(B, n_sel, SEL_BLOCK, G, D).transpose(0, 3, 1, 2, 4)
    bi = jnp.arange(B)[:, None, None, None]
    gi = jnp.arange(G)[None, :, None, None]

    kw = rms_norm(k_win.reshape(B, S, G, D), k_gains[2])
    kw_pad = jnp.pad(kw, ((0, 0), (WINDOW, 0), (0, 0), (0, 0)))
    vw_pad = jnp.pad(v_win.reshape(B, S, G, D), ((0, 0), (WINDOW, 0), (0, 0), (0, 0)))
    n_win = WINDOW + Q_BLOCK

    def block(qb):
        q0 = qb * Q_BLOCK
        t = q0 + jnp.arange(Q_BLOCK)
        qq = lax.dynamic_slice_in_dim(q, q0, Q_BLOCK, axis=1)
        gb = lax.dynamic_slice_in_dim(gates, q0, Q_BLOCK, axis=1)

        dist_c = t[:, None].astype(jnp.float32) - cmp_centre[None, :]
        s_c = jnp.einsum('bqgrd,bngd->bgrqn', qq, kc) * scale - slopes[:, :, None, None] * dist_c
        p_c = masked_softmax(s_c, cmp_end[None, :] <= t[:, None])
        o_c = jnp.einsum('bgrqn,bngd->bqgrd', p_c.astype(vc.dtype), vc)

        imp = jnp.einsum('bgrqn,ns->bgqs', p_c, overlap)
        blk = jnp.arange(n_sel)
        forced = (blk[None, :] == 0) | (blk[None, :] == t[:, None] // SEL_BLOCK)
        future = sel_start[None, :] > t[:, None]
        imp = jnp.where(future, NEG, jnp.where(forced, FORCE_SCORE, imp))
        _, idx = lax.top_k(imp, top)
        kg = ks_blocks[bi, gi, idx].reshape(B, G, Q_BLOCK, top * SEL_BLOCK, D)
        vg = vs_blocks[bi, gi, idx].reshape(B, G, Q_BLOCK, top * SEL_BLOCK, D)
        s_pos = (idx[..., None] * SEL_BLOCK + jnp.arange(SEL_BLOCK)).reshape(B, G, Q_BLOCK, top * SEL_BLOCK)
        dist_s = t[None, None, :, None] - s_pos
        s_s = (jnp.einsum('bqgrd,bgqkd->bgrqk', qq, kg) * scale
               - slopes[None, :, :, None, None] * dist_s[:, :, None].astype(jnp.float32))
        p_s = masked_softmax(s_s, (dist_s >= 0)[:, :, None])
        o_s = jnp.einsum('bgrqk,bgqkd->bqgrd', p_s.astype(vg.dtype), vg)

        kwb = lax.dynamic_slice_in_dim(kw_pad, q0, n_win, axis=1)
        vwb = lax.dynamic_slice_in_dim(vw_pad, q0, n_win, axis=1)
        w_pos = q0 - WINDOW + jnp.arange(n_win)
        dist_w = t[:, None] - w_pos[None, :]
        m_w = (w_pos[None, :] >= 0) & (dist_w >= 0) & (dist_w < WINDOW)
        s_w = (jnp.einsum('bqgrd,bkgd->bgrqk', qq, kwb) * scale
               - slopes[:, :, None, None] * dist_w.astype(jnp.float32))
        p_w = masked_softmax(s_w, m_w)
        o_w = jnp.einsum('bgrqk,bkgd->bqgrd', p_w.astype(vwb.dtype), vwb)

        return gb[..., 0:1] * o_c + gb[..., 1:2] * o_s + gb[..., 2:3] * o_w

    out = lax.map(block, jnp.arange(S // Q_BLOCK))
    return out.transpose(1, 0, 2, 3, 4, 5).reshape(B, S, NSA_WIDTH)


def mla_mixer(c_q, c_kv, k_rope, q_a_gain, w_uq, kv_a_gain, w_ukv, q_gain, k_gain):
    B, S = c_q.shape[0], c_q.shape[1]
    Hm = MLA_HEADS
    pos = jnp.arange(S)
    q = (rms_norm(c_q, q_a_gain) @ w_uq).reshape(B, S, Hm, MLA_QK_DIM)
    kv = (rms_norm(c_kv, kv_a_gain) @ w_ukv).reshape(B, S, Hm, MLA_NOPE_DIM + MLA_V_DIM)
    k_nope, v = kv[..., :MLA_NOPE_DIM], kv[..., MLA_NOPE_DIM:]
    k = jnp.concatenate([k_nope, jnp.broadcast_to(k_rope[:, :, None, :], (B, S, Hm, MLA_ROPE_DIM))], axis=-1)
    q = rms_norm(q, q_gain)
    k = rms_norm(k, k_gain)
    q = jnp.concatenate([q[..., :MLA_NOPE_DIM], rope(q[..., MLA_NOPE_DIM:], pos)], axis=-1)
    k = jnp.concatenate([k[..., :MLA_NOPE_DIM], rope(k[..., MLA_NOPE_DIM:], pos)], axis=-1)
    scale = MLA_QK_DIM ** -0.5

    def block(qb):
        q0 = qb * Q_BLOCK
        t = q0 + jnp.arange(Q_BLOCK)
        qq = lax.dynamic_slice_in_dim(q, q0, Q_BLOCK, axis=1)
        s = jnp.einsum('bqhd,bkhd->bhqk', qq, k) * scale
        p = masked_softmax(s, pos[None, :] <= t[:, None])
        return jnp.einsum('bhqk,bkhd->bqhd', p.astype(v.dtype), v)

    out = lax.map(block, jnp.arange(S // Q_BLOCK))
    return out.transpose(1, 0, 2, 3, 4).reshape(B, S, MLA_WIDTH)


def setup_inputs(seed: int = 0) -> dict:
    key = jax.random.key(seed)
    keys = iter(jax.random.split(key, 40))

    def dense(shape, fan_in):
        return jax.random.normal(next(keys), (DEPTH,) + shape, jnp.float32) * fan_in ** -0.5

    def gain(shape):
        return 1.0 + 0.02 * jax.random.normal(next(keys), (DEPTH,) + shape, jnp.float32)

    def small(shape):
        return 0.1 * jax.random.normal(next(keys), (DEPTH,) + shape, jnp.float32)

    x = jax.random.normal(next(keys), (BATCH, SEQ, D_MODEL), jnp.float32)
    return {
        "x": x,
        "ffn1_norm": gain((D_MODEL,)),
        "ffn1_w_gate": dense((D_MODEL, D_FF), D_MODEL),
        "ffn1_w_up": dense((D_MODEL, D_FF), D_MODEL),
        "ffn1_w_down": dense((D_FF, D_MODEL), D_FF),
        "mix_norm": gain((D_MODEL,)),
        "w_in": dense((D_MODEL, IN_DIM), D_MODEL),
        "nsa_q_norm": gain((HEAD_DIM,)),
        "nsa_k_norm": gain((3, HEAD_DIM)),
        "nsa_cmp_pos_k": small((CMP_BLOCK, HEAD_DIM)),
        "nsa_cmp_w1_k": dense((CMP_BLOCK * HEAD_DIM, CMP_HIDDEN), CMP_BLOCK * HEAD_DIM),
        "nsa_cmp_w2_k": dense((CMP_HIDDEN, HEAD_DIM), CMP_HIDDEN),
        "nsa_cmp_pos_v": small((CMP_BLOCK, HEAD_DIM)),
        "nsa_cmp_w1_v": dense((CMP_BLOCK * HEAD_DIM, CMP_HIDDEN), CMP_BLOCK * HEAD_DIM),
        "nsa_cmp_w2_v": dense((CMP_HIDDEN, HEAD_DIM), CMP_HIDDEN),
        "mla_q_a_norm": gain((MLA_Q_RANK,)),
        "mla_w_uq": dense((MLA_Q_RANK, MLA_HEADS * MLA_QK_DIM), MLA_Q_RANK),
        "mla_kv_a_norm": gain((MLA_KV_RANK,)),
        "mla_w_ukv": dense((MLA_KV_RANK, MLA_HEADS * (MLA_NOPE_DIM + MLA_V_DIM)), MLA_KV_RANK),
        "mla_q_norm": gain((MLA_QK_DIM,)),
        "mla_k_norm": gain((MLA_QK_DIM,)),
        "out_norm_nsa": gain((NSA_WIDTH,)),
        "out_norm_mla": gain((MLA_WIDTH,)),
        "w_out": dense((MIX_WIDTH, D_MODEL), MIX_WIDTH),
        "ffn2_norm": gain((D_MODEL,)),
        "ffn2_w_gate": dense((D_MODEL, D_FF), D_MODEL),
        "ffn2_w_up": dense((D_MODEL, D_FF), D_MODEL),
        "ffn2_w_down": dense((D_FF, D_MODEL), D_FF),
    }


def reference(x, ffn1_norm, ffn1_w_gate, ffn1_w_up, ffn1_w_down, mix_norm, w_in,
              nsa_q_norm, nsa_k_norm, nsa_cmp_pos_k, nsa_cmp_w1_k, nsa_cmp_w2_k,
              nsa_cmp_pos_v, nsa_cmp_w1_v, nsa_cmp_w2_v,
              mla_q_a_norm, mla_w_uq, mla_kv_a_norm, mla_w_ukv, mla_q_norm, mla_k_norm,
              out_norm_nsa, out_norm_mla, w_out,
              ffn2_norm, ffn2_w_gate, ffn2_w_up, ffn2_w_down):
    offsets = [int(v) for v in np.cumsum(IN_SPLITS)[:-1]]
    for l in range(DEPTH):
        h = rms_norm(x, ffn1_norm[l])
        x = x + 0.5 * swiglu(h, ffn1_w_gate[l], ffn1_w_up[l], ffn1_w_down[l])

        h = rms_norm(x, mix_norm[l])
        proj = h @ w_in[l]
        (q_a, k_cmp, v_cmp, k_sel, v_sel, k_win, v_win, g_a,
         c_q, c_kv, k_rope) = jnp.split(proj, offsets, axis=-1)
        gates = jax.nn.sigmoid(g_a.astype(jnp.float32)).astype(x.dtype)
        o_a = nsa_mixer(q_a, k_cmp, v_cmp, k_sel, v_sel, k_win, v_win, gates,
                        nsa_q_norm[l], nsa_k_norm[l],
                        nsa_cmp_pos_k[l], nsa_cmp_w1_k[l], nsa_cmp_w2_k[l],
                        nsa_cmp_pos_v[l], nsa_cmp_w1_v[l], nsa_cmp_w2_v[l])
        o_b = mla_mixer(c_q, c_kv, k_rope, mla_q_a_norm[l], mla_w_uq[l],
                        mla_kv_a_norm[l], mla_w_ukv[l], mla_q_norm[l], mla_k_norm[l])
        mixed = jnp.concatenate([rms_norm(o_a, out_norm_nsa[l]), rms_norm(o_b, out_norm_mla[l])], axis=-1)
        x = x + mixed @ w_out[l]

        h = rms_norm(x, ffn2_norm[l])
        x = x + 0.5 * swiglu(h, ffn2_w_gate[l], ffn2_w_up[l], ffn2_w_down[l])
    return x
```

```python
import functools

import numpy as np
import jax
import jax.numpy as jnp
from jax import lax
from jax.experimental import pallas as pl
from jax.experimental.pallas import tpu as pltpu

F32 = jnp.float32
BF16 = jnp.bfloat16

HEAD_DIM = 128
NSA_HEADS = 8
NSA_KV_HEADS = 2
NSA_GROUP = NSA_HEADS // NSA_KV_HEADS
CMP_BLOCK = 32
CMP_STRIDE = 16
CMP_HIDDEN = 256
SEL_BLOCK = 64
SEL_SHIFT = 6
SEL_TOP = 8
WINDOW = 512
MLA_HEADS = 8
MLA_Q_RANK = 384
MLA_KV_RANK = 256
MLA_NOPE_DIM = 128
MLA_ROPE_DIM = 64
MLA_V_DIM = 128
MLA_QK_DIM = MLA_NOPE_DIM + MLA_ROPE_DIM
ROPE_THETA = 10000.0
EPS = 1e-6
NEG = -1e30
FORCE_SCORE = 1e4
NSA_WIDTH = NSA_HEADS * HEAD_DIM
MLA_WIDTH = MLA_HEADS * MLA_V_DIM
NSA_KV_WIDTH = NSA_KV_HEADS * HEAD_DIM

LANES = 128
MLA_QK_PAD = 2 * LANES
VMEM_LIMIT = 56 * 1024 * 1024


def _cparams(sem):
    return pltpu.CompilerParams(dimension_semantics=sem, vmem_limit_bytes=VMEM_LIMIT)


def _rms(x, gain):
    return x * lax.rsqrt(jnp.mean(x * x, axis=-1, keepdims=True) + EPS) * gain


def _dot(a, b):
    return jnp.dot(a, b, preferred_element_type=F32)


def _dot_nt(a, b):
    return lax.dot_general(a, b, (((1,), (1,)), ((), ())), preferred_element_type=F32)


def _ffn_kernel(x_ref, g_ref, wg_ref, wu_ref, wd_ref, o_ref, h_ref):
    @pl.when(pl.program_id(1) == 0)
    def _():
        x = x_ref[...]
        h_ref[...] = _rms(x, g_ref[...]).astype(BF16)
        o_ref[...] = x

    h = h_ref[...]
    gate = _dot(h, wg_ref[...])
    up = _dot(h, wu_ref[...])
    act = gate * (1.0 / (1.0 + jnp.exp(-gate))) * up * 0.5
    o_ref[...] += _dot(act.astype(BF16), wd_ref[...])


def _ffn(x, gain, wg, wu, wd, *, tm=1024, tf=512):
    t, d = x.shape
    f = wg.shape[1]
    return pl.pallas_call(
        _ffn_kernel,
        grid=(t // tm, f // tf),
        in_specs=[
            pl.BlockSpec((tm, d), lambda i, j: (i, 0)),
            pl.BlockSpec((1, d), lambda i, j: (0, 0)),
            pl.BlockSpec((d, tf), lambda i, j: (0, j)),
            pl.BlockSpec((d, tf), lambda i, j: (0, j)),
            pl.BlockSpec((tf, d), lambda i, j: (j, 0)),
        ],
        out_specs=pl.BlockSpec((tm, d), lambda i, j: (i, 0)),
        out_shape=jax.ShapeDtypeStruct((t, d), F32),
        scratch_shapes=[pltpu.VMEM((tm, d), BF16)],
        compiler_params=_cparams(("parallel", "arbitrary")),
        name="ffn",
    )(x, gain, wg, wu, wd)


_P_Q = 0
_P_CMP = _P_Q + NSA_WIDTH
_P_KS = _P_CMP + 2 * NSA_KV_WIDTH
_P_VS = _P_KS + NSA_KV_WIDTH
_P_KW = _P_VS + NSA_KV_WIDTH
_P_VW = _P_KW + NSA_KV_WIDTH
_P_CQ = _P_VW + NSA_KV_WIDTH
_P_CKV = _P_CQ + MLA_Q_RANK
_P_KR = _P_CKV + MLA_KV_RANK
_P_GATE = _P_KR + 2 * LANES
_P_END = _P_GATE + LANES


def _proj_kernel(x_ref, g_ref, w_ref, qg_ref, kg_ref, qag_ref, kvag_ref,
                 qn_ref, cmp_ref, ks_ref, vs_ref, kw_ref, vw_ref, gate_ref, cq_ref, ckv_ref, kr_ref):
    h = _rms(x_ref[...], g_ref[...]).astype(BF16)

    def proj(lo, hi):
        return _dot(h, w_ref[:, lo:hi])

    qscale = HEAD_DIM ** -0.5
    for hd in range(NSA_HEADS):
        lo = _P_Q + hd * HEAD_DIM
        qn_ref[:, hd * HEAD_DIM:(hd + 1) * HEAD_DIM] = (
            _rms(proj(lo, lo + HEAD_DIM), qg_ref[...]) * qscale).astype(BF16)
    cmp_ref[...] = proj(_P_CMP, _P_KS).astype(BF16)
    for g in range(NSA_KV_HEADS):
        sl = slice(g * HEAD_DIM, (g + 1) * HEAD_DIM)
        ks_ref[:, sl] = _rms(proj(_P_KS + g * HEAD_DIM, _P_KS + (g + 1) * HEAD_DIM), kg_ref[1:2, :]).astype(BF16)
        kw_ref[:, sl] = _rms(proj(_P_KW + g * HEAD_DIM, _P_KW + (g + 1) * HEAD_DIM), kg_ref[2:3, :]).astype(BF16)
    vs_ref[...] = proj(_P_VS, _P_KW).astype(BF16)
    vw_ref[...] = proj(_P_VW, _P_CQ).astype(BF16)
    cq_ref[...] = _rms(proj(_P_CQ, _P_CKV), qag_ref[...]).astype(BF16)
    ckv_ref[...] = _rms(proj(_P_CKV, _P_KR), kvag_ref[...]).astype(BF16)
    kr_ref[...] = proj(_P_KR, _P_GATE)
    gate_ref[...] = 1.0 / (1.0 + jnp.exp(-proj(_P_GATE, _P_END)))


def _proj(x, gain, w, q_gain, k_gains, qa_gain, kva_gain, *, tm=256):
    t, d = x.shape
    n = w.shape[1]

    def full(a):
        return pl.BlockSpec(a.shape, lambda i: (0,) * a.ndim)

    def rows(width):
        return pl.BlockSpec((tm, width), lambda i: (i, 0))

    widths = [(NSA_WIDTH, BF16), (2 * NSA_KV_WIDTH, BF16), (NSA_KV_WIDTH, BF16), (NSA_KV_WIDTH, BF16),
              (NSA_KV_WIDTH, BF16), (NSA_KV_WIDTH, BF16), (LANES, F32), (MLA_Q_RANK, BF16),
              (MLA_KV_RANK, BF16), (2 * LANES, F32)]
    return pl.pallas_call(
        _proj_kernel,
        grid=(t // tm,),
        in_specs=[rows(d), full(gain), full(w), full(q_gain), full(k_gains), full(qa_gain), full(kva_gain)],
        out_specs=[rows(wd) for wd, _ in widths],
        out_shape=[jax.ShapeDtypeStruct((t, wd), dt) for wd, dt in widths],
        compiler_params=_cparams(("parallel",)),
        name="proj",
    )(x, gain, w, q_gain, k_gains, qa_gain, kva_gain)


def _gelu_tanh(x):
    return 0.5 * x * (1.0 + jnp.tanh(np.sqrt(2.0 / np.pi) * (x + 0.044715 * (x * x * x))))


def _compress_kernel(x_ref, w1k_ref, w2k_ref, pk_ref, w1v_ref, w2v_ref, pv_ref, kg_ref, kc_ref, vc_ref):
    half = CMP_STRIDE * HEAD_DIM
    row_w = 2 * NSA_KV_WIDTH
    for which, (w1_ref, w2_ref, p_ref, o_ref) in enumerate(
            ((w1k_ref, w2k_ref, pk_ref, kc_ref), (w1v_ref, w2v_ref, pv_ref, vc_ref))):
        pos_bias = _dot(jnp.broadcast_to(p_ref[...], (8, 2 * half)).astype(BF16), w1_ref[...])[0:1, :]
        for g in range(NSA_KV_HEADS):
            col = which * NSA_KV_WIDTH + g * HEAD_DIM
            xg = jnp.concatenate(
                [x_ref[0, :, l * row_w + col:l * row_w + col + HEAD_DIM] for l in range(CMP_STRIDE)], axis=1)
            first = _dot(xg, w1_ref[0:half, :])
            second = _dot(xg, w1_ref[half:2 * half, :])
            nrow = second.shape[0]
            pre = first + pltpu.roll(second, nrow - 1, 0) + pos_bias
            out = _dot(_gelu_tanh(pre).astype(BF16), w2_ref[...])
            if which == 0:
                out = _rms(out, kg_ref[0:1, :])
            o_ref[0, g] = out.astype(BF16)


def _compress(xc, w1k, w2k, pk, w1v, w2v, pv, k_gains):
    b, nrow, width = xc.shape

    def full(a):
        return pl.BlockSpec(a.shape, lambda i: (0,) * a.ndim)

    out_sds = jax.ShapeDtypeStruct((b, NSA_KV_HEADS, nrow, HEAD_DIM), BF16)
    out_spec = pl.BlockSpec((1, NSA_KV_HEADS, nrow, HEAD_DIM), lambda i: (i, 0, 0, 0))
    return pl.pallas_call(
        _compress_kernel,
        grid=(b,),
        in_specs=[pl.BlockSpec((1, nrow, width), lambda i: (i, 0, 0)),
                  full(w1k), full(w2k), full(pk), full(w1v), full(w2v), full(pv), full(k_gains)],
        out_specs=[out_spec, out_spec],
        out_shape=[out_sds, out_sds],
        compiler_params=_cparams(("parallel",)),
        name="compress",
    )(xc, w1k, w2k, pk, w1v, w2v, pv, k_gains)


def _flash(q, k_tile, v_tile, bias_mask, lo, hi, rows, d_v):
    def body(j, carry):
        m, l, acc = carry
        s = _dot_nt(q, k_tile(j))
        bias, mask = bias_mask(j)
        s = jnp.where(mask, s - bias, NEG)
        m_new = jnp.maximum(m, jnp.max(s, axis=-1, keepdims=True))
        p = jnp.where(mask, jnp.exp(s - m_new), 0.0)
        alpha = jnp.exp(m - m_new)
        l = alpha * l + jnp.sum(p, axis=-1, keepdims=True)
        acc = alpha * acc + _dot(p.astype(BF16), v_tile(j))
        return m_new, l, acc

    init = (jnp.full((rows, 1), NEG, F32), jnp.zeros((rows, 1), F32), jnp.zeros((rows, d_v), F32))
    m, l, acc = lax.fori_loop(lo, hi, body, init)
    return acc / jnp.where(l > 0.0, l, 1.0)


def _nsa_kernel(q_ref, kc_ref, vc_ref, ks_ref, vs_ref, kw_ref, vw_ref, gate_ref, og_ref, o_ref, *, tq, tk_sel):
    qi = pl.program_id(1)
    q0 = qi * tq
    rows = NSA_GROUP * tq
    ncmp = kc_ref.shape[2]
    n_sel_pad = LANES

    t_q = q0 + lax.broadcasted_iota(jnp.int32, (tq, 1), 0)
    gates = gate_ref[0]
    outs = []
    for g in range(NSA_KV_HEADS):
        qg = jnp.concatenate(
            [q_ref[0, :, (g * NSA_GROUP + r) * HEAD_DIM:(g * NSA_GROUP + r + 1) * HEAD_DIM] for r in range(NSA_GROUP)],
            axis=0)
        slopes = [2.0 ** (-(g * NSA_GROUP + r + 1)) for r in range(NSA_GROUP)]

        def stack_bias(dist):
            return jnp.concatenate([s * dist for s in slopes], axis=0)

        def stack4(a):
            return jnp.concatenate([a] * NSA_GROUP, axis=0)

        n_idx = lax.broadcasted_iota(jnp.int32, (tq, ncmp), 1)
        dist_c = t_q.astype(F32) - (n_idx.astype(F32) * CMP_STRIDE + 0.5 * (CMP_BLOCK - 1))
        mask_c = stack4((n_idx * CMP_STRIDE + (CMP_BLOCK - 1)) <= t_q)
        s_c = _dot_nt(qg, kc_ref[0, g]) - stack_bias(dist_c)
        s_c = jnp.where(mask_c, s_c, NEG)
        m_c = jnp.max(s_c, axis=-1, keepdims=True)
        p_c = jnp.where(mask_c, jnp.exp(s_c - m_c), 0.0)
        l_c = jnp.sum(p_c, axis=-1, keepdims=True)
        p_c = p_c / jnp.where(l_c > 0.0, l_c, 1.0)
        o_c = _dot(p_c.astype(BF16), vc_ref[0, g])

        p_sum = p_c[0:tq]
        for r in range(1, NSA_GROUP):
            p_sum = p_sum + p_c[r * tq:(r + 1) * tq]
        n_row = lax.broadcasted_iota(jnp.int32, (ncmp, n_sel_pad), 0)
        s_col = lax.broadcasted_iota(jnp.int32, (ncmp, n_sel_pad), 1)
        overlap = jnp.where((n_row * CMP_STRIDE < (s_col + 1) * SEL_BLOCK)
                            & (n_row * CMP_STRIDE + (CMP_BLOCK - 1) >= s_col * SEL_BLOCK), 1.0, 0.0).astype(BF16)
        p_hi = p_sum.astype(BF16)
        p_lo = (p_sum - p_hi.astype(F32)).astype(BF16)
        imp = _dot(p_hi, overlap) + _dot(p_lo, overlap)
        n_sel = ks_ref.shape[1] // SEL_BLOCK
        blk = lax.broadcasted_iota(jnp.int32, (tq, n_sel_pad), 1)
        forced = (blk == 0) | (blk == jnp.right_shift(t_q, SEL_SHIFT))
        future = blk * SEL_BLOCK > t_q
        imp = jnp.where(future, NEG, jnp.where(forced, FORCE_SCORE, imp))
        imp = jnp.where(blk < n_sel, imp, -jnp.inf)
        sel = jnp.zeros((tq, n_sel_pad), F32)
        for _ in range(min(SEL_TOP, n_sel)):
            mx = jnp.max(imp, axis=-1, keepdims=True)
            first = jnp.min(jnp.where(imp == mx, blk, n_sel_pad), axis=-1, keepdims=True)
            pick = blk == first
            sel = jnp.where(pick, 1.0, sel)
            imp = jnp.where(pick, -jnp.inf, imp)
        sel_b = sel.astype(BF16)

        def sel_bias_mask(j):
            kpos = j * tk_sel + lax.broadcasted_iota(jnp.int32, (tq, tk_sel), 1)
            dist = t_q - kpos
            e_row = lax.broadcasted_iota(jnp.int32, (n_sel_pad, tk_sel), 0)
            e_col = j * tk_sel + lax.broadcasted_iota(jnp.int32, (n_sel_pad, tk_sel), 1)
            expand = jnp.where(e_row == jnp.right_shift(e_col, SEL_SHIFT), 1.0, 0.0).astype(BF16)
            chosen = _dot(sel_b, expand) > 0.5
            return stack_bias(dist.astype(F32)), stack4(chosen & (dist >= 0))

        gsl = slice(g * HEAD_DIM, (g + 1) * HEAD_DIM)
        o_s = _flash(qg,
                     lambda j: ks_ref[0, pl.ds(pl.multiple_of(j * tk_sel, tk_sel), tk_sel), gsl],
                     lambda j: vs_ref[0, pl.ds(pl.multiple_of(j * tk_sel, tk_sel), tk_sel), gsl],
                     sel_bias_mask, 0, (q0 + tq + tk_sel - 1) // tk_sel, rows, HEAD_DIM)

        def win_bias_mask(j):
            kpos = j * tq + lax.broadcasted_iota(jnp.int32, (tq, tq), 1)
            dist = t_q - kpos
            return stack_bias(dist.astype(F32)), stack4((dist >= 0) & (dist < WINDOW))

        o_w = _flash(qg,
                     lambda j: kw_ref[0, pl.ds(pl.multiple_of(j * tq, tq), tq), gsl],
                     lambda j: vw_ref[0, pl.ds(pl.multiple_of(j * tq, tq), tq), gsl],
                     win_bias_mask, jnp.maximum(qi - WINDOW // tq, 0), qi + 1, rows, HEAD_DIM)

        for r in range(NSA_GROUP):
            hd = g * NSA_GROUP + r
            rs = slice(r * tq, (r + 1) * tq)
            outs.append(gates[:, 3 * hd:3 * hd + 1] * o_c[rs]
                        + gates[:, 3 * hd + 1:3 * hd + 2] * o_s[rs]
                        + gates[:, 3 * hd + 2:3 * hd + 3] * o_w[rs])

    ssq = outs[0] * outs[0]
    for o in outs[1:]:
        ssq = ssq + o * o
    inv = lax.rsqrt(jnp.sum(ssq, axis=-1, keepdims=True) / NSA_WIDTH + EPS)
    for hd, o in enumerate(outs):
        sl = slice(hd * HEAD_DIM, (hd + 1) * HEAD_DIM)
        o_ref[0, :, sl] = (o * inv * og_ref[:, sl]).astype(BF16)


def _nsa(qn, kc, vc, ks, vs, kw, vw, gates, out_gain, *, tq=128, tk_sel=256):
    b, s, _ = qn.shape

    def per_b(a):
        return pl.BlockSpec((1,) + a.shape[1:], lambda i, j: (i,) + (0,) * (a.ndim - 1))

    def q_rows(width):
        return pl.BlockSpec((1, tq, width), lambda i, j: (i, j, 0))

    return pl.pallas_call(
        functools.partial(_nsa_kernel, tq=tq, tk_sel=tk_sel),
        grid=(b, s // tq),
        in_specs=[q_rows(NSA_WIDTH), per_b(kc), per_b(vc), per_b(ks), per_b(vs), per_b(kw), per_b(vw),
                  q_rows(LANES), pl.BlockSpec(out_gain.shape, lambda i, j: (0, 0))],
        out_specs=q_rows(NSA_WIDTH),
        out_shape=jax.ShapeDtypeStruct((b, s, NSA_WIDTH), BF16),
        compiler_params=_cparams(("parallel", "arbitrary")),
        name="nsa",
    )(qn, kc, vc, ks, vs, kw, vw, gates, out_gain)


def _mla_prep_kernel(cq_ref, ckv_ref, kr_ref, cos_ref, sin_ref, wq_ref, wkv_ref, qgn_ref, qgr_ref, qgt_ref,
                     kgn_ref, kgr_ref, kgt_ref, q_ref, k_ref, v_ref):
    cq = cq_ref[0]
    ckv = ckv_ref[0]
    cos = cos_ref[...]
    sin = sin_ref[...]
    k_rope = kr_ref[0, :, 0:LANES]
    k_rot = kr_ref[0, :, LANES:2 * LANES]
    k_rope_ssq = jnp.sum(k_rope * k_rope, axis=-1, keepdims=True)
    k_rope_emb = k_rope * kgr_ref[...] * cos + k_rot * kgt_ref[...] * sin
    qscale = MLA_QK_DIM ** -0.5
    rot_base = MLA_HEADS * MLA_QK_PAD
    for hd in range(MLA_HEADS):
        qx = _dot(cq, wq_ref[:, hd * MLA_QK_PAD:(hd + 1) * MLA_QK_PAD])
        q_rot = _dot(cq, wq_ref[:, rot_base + hd * LANES:rot_base + (hd + 1) * LANES])
        q_nope, q_rope = qx[:, 0:LANES], qx[:, LANES:2 * LANES]
        ssq = jnp.sum(q_nope * q_nope, axis=-1, keepdims=True) + jnp.sum(q_rope * q_rope, axis=-1, keepdims=True)
        inv = lax.rsqrt(ssq / MLA_QK_DIM + EPS) * qscale
        q_ref[0, hd, :, 0:LANES] = (q_nope * inv * qgn_ref[...]).astype(BF16)
        q_ref[0, hd, :, LANES:2 * LANES] = (
            (q_rope * qgr_ref[...] * cos + q_rot * qgt_ref[...] * sin) * inv).astype(BF16)

        kv = _dot(ckv, wkv_ref[:, hd * 2 * LANES:(hd + 1) * 2 * LANES])
        k_nope = kv[:, 0:LANES]
        ssq = jnp.sum(k_nope * k_nope, axis=-1, keepdims=True) + k_rope_ssq
        inv = lax.rsqrt(ssq / MLA_QK_DIM + EPS)
        k_ref[0, hd, :, 0:LANES] = (k_nope * inv * kgn_ref[...]).astype(BF16)
        k_ref[0, hd, :, LANES:2 * LANES] = (k_rope_emb * inv).astype(BF16)
        v_ref[0, hd] = kv[:, LANES:2 * LANES].astype(BF16)


def _mla_prep(cq, ckv, kr, cos, sin, wq, wkv, gains, *, tm=256):
    b, s, _ = cq.shape

    def full(a):
        return pl.BlockSpec(a.shape, lambda i, j: (0,) * a.ndim)

    def rows(width):
        return pl.BlockSpec((1, tm, width), lambda i, j: (i, j, 0))

    def heads(width):
        return pl.BlockSpec((1, MLA_HEADS, tm, width), lambda i, j: (i, 0, j, 0))

    pos_rows = pl.BlockSpec((tm, LANES), lambda i, j: (j, 0))
    return pl.pallas_call(
        _mla_prep_kernel,
        grid=(b, s // tm),
        in_specs=[rows(MLA_Q_RANK), rows(MLA_KV_RANK), rows(2 * LANES), pos_rows, pos_rows, full(wq), full(wkv)]
                 + [full(gn) for gn in gains],
        out_specs=[heads(MLA_QK_PAD), heads(MLA_QK_PAD), heads(MLA_V_DIM)],
        out_shape=[jax.ShapeDtypeStruct((b, MLA_HEADS, s, MLA_QK_PAD), BF16),
                   jax.ShapeDtypeStruct((b, MLA_HEADS, s, MLA_QK_PAD), BF16),
                   jax.ShapeDtypeStruct((b, MLA_HEADS, s, MLA_V_DIM), BF16)],
        compiler_params=_cparams(("parallel", "parallel")),
        name="mla_prep",
    )(cq, ckv, kr, cos, sin, wq, wkv, *gains)


def _mla_attn_kernel(q_ref, k_ref, v_ref, o_ref, *, tq, tk):
    qi = pl.program_id(2)
    t_q = qi * tq + lax.broadcasted_iota(jnp.int32, (tq, 1), 0)

    def bias_mask(j):
        kpos = j * tk + lax.broadcasted_iota(jnp.int32, (tq, tk), 1)
        return 0.0, kpos <= t_q

    o_ref[0] = _flash(q_ref[0, 0],
                      lambda j: k_ref[0, 0, pl.ds(pl.multiple_of(j * tk, tk), tk), :],
                      lambda j: v_ref[0, 0, pl.ds(pl.multiple_of(j * tk, tk), tk), :],
                      bias_mask, 0, (qi * tq + tq + tk - 1) // tk, tq, v_ref.shape[-1])


def _mla_attn(q, k, v, *, tq=256, tk=256):
    b, h, s, dq = q.shape
    dv = v.shape[-1]
    return pl.pallas_call(
        functools.partial(_mla_attn_kernel, tq=tq, tk=tk),
        grid=(b, h, s // tq),
        in_specs=[pl.BlockSpec((1, 1, tq, dq), lambda i, j, l: (i, j, l, 0)),
                  pl.BlockSpec((1, 1, s, dq), lambda i, j, l: (i, j, 0, 0)),
                  pl.BlockSpec((1, 1, s, dv), lambda i, j, l: (i, j, 0, 0))],
        out_specs=pl.BlockSpec((1, tq, dv), lambda i, j, l: (i, l, j)),
        out_shape=jax.ShapeDtypeStruct((b, s, h * dv), F32),
        compiler_params=_cparams(("parallel", "parallel", "arbitrary")),
        name="mla_attn",
    )(q, k, v)


def _out_kernel(x_ref, oa_ref, ob_ref, gb_ref, w_ref, o_ref):
    na = oa_ref.shape[1]
    ob = _rms(ob_ref[...], gb_ref[...]).astype(BF16)
    o_ref[...] = x_ref[...] + _dot(oa_ref[...], w_ref[0:na, :]) + _dot(ob, w_ref[na:, :])


def _out_proj(x, oa, ob, gain_b, w, *, tm=256):
    t, d = x.shape

    def rows(width):
        return pl.BlockSpec((tm, width), lambda i: (i, 0))

    return pl.pallas_call(
        _out_kernel,
        grid=(t // tm,),
        in_specs=[rows(d), rows(oa.shape[1]), rows(ob.shape[1]),
                  pl.BlockSpec(gain_b.shape, lambda i: (0, 0)), pl.BlockSpec(w.shape, lambda i: (0, 0))],
        out_specs=rows(d),
        out_shape=jax.ShapeDtypeStruct((t, d), F32),
        compiler_params=_cparams(("parallel",)),
        name="out_proj",
    )(x, oa, ob, gain_b, w)


def _rot_cols(w):
    half = w.shape[1] // 2
    return jnp.concatenate([-w[:, half:], w[:, :half]], axis=1)


def _layout_w_in(w_in):
    d = w_in.shape[0]
    off = np.cumsum([0, NSA_WIDTH] + [NSA_KV_WIDTH] * 6 + [NSA_HEADS * 3, MLA_Q_RANK, MLA_KV_RANK, MLA_ROPE_DIM])
    q, kc, vc, ks, vs, kw, vw, gt, cq, ckv, kr = [w_in[:, off[i]:off[i + 1]] for i in range(11)]
    zr = jnp.zeros((d, LANES - MLA_ROPE_DIM), w_in.dtype)
    zg = jnp.zeros((d, LANES - NSA_HEADS * 3), w_in.dtype)
    return jnp.concatenate([q, kc, vc, ks, vs, kw, vw, cq, ckv, kr, zr, _rot_cols(kr), zr, gt, zg], axis=1).astype(BF16)


def _layout_w_uq(w_uq):
    r = w_uq.shape[0]
    zr = jnp.zeros((r, LANES - MLA_ROPE_DIM), w_uq.dtype)
    main, rot = [], []
    for hd in range(MLA_HEADS):
        base = hd * MLA_QK_DIM
        rope = w_uq[:, base + MLA_NOPE_DIM:base + MLA_QK_DIM]
        main += [w_uq[:, base:base + MLA_NOPE_DIM], rope, zr]
        rot += [_rot_cols(rope), zr]
    return jnp.concatenate(main + rot, axis=1).astype(BF16)


def _rope_gains(gain):
    half = MLA_ROPE_DIM // 2
    zr = jnp.zeros((LANES - MLA_ROPE_DIM,), gain.dtype)
    g_rope = gain[MLA_NOPE_DIM:]
    return (gain[None, :MLA_NOPE_DIM], jnp.concatenate([g_rope, zr])[None, :],
            jnp.concatenate([g_rope[half:], g_rope[:half], zr])[None, :])


def _rope_tables(s):
    half = MLA_ROPE_DIM // 2
    inv = ROPE_THETA ** (-jnp.arange(half, dtype=F32) / half)
    ang = jnp.arange(s).astype(F32)[:, None] * inv[None, :]
    zr = jnp.zeros((s, LANES - MLA_ROPE_DIM), F32)
    cos, sin = jnp.cos(ang), jnp.sin(ang)
    return jnp.concatenate([cos, cos, zr], axis=1), jnp.concatenate([sin, sin, zr], axis=1)


def kernel(x, ffn1_norm, ffn1_w_gate, ffn1_w_up, ffn1_w_down, mix_norm, w_in, nsa_q_norm, nsa_k_norm, nsa_cmp_pos_k, nsa_cmp_w1_k, nsa_cmp_w2_k, nsa_cmp_pos_v, nsa_cmp_w1_v, nsa_cmp_w2_v, mla_q_a_norm, mla_w_uq, mla_kv_a_norm, mla_w_ukv, mla_q_norm, mla_k_norm, out_norm_nsa, out_norm_mla, w_out, ffn2_norm, ffn2_w_gate, ffn2_w_up, ffn2_w_down):
    b, s, d = x.shape
    depth = ffn1_norm.shape[0]
    cos, sin = _rope_tables(s)
    xt = x.reshape(b * s, d)
    for l in range(depth):
        xt = _ffn(xt, ffn1_norm[l][None, :], ffn1_w_gate[l].astype(BF16), ffn1_w_up[l].astype(BF16),
                  ffn1_w_down[l].astype(BF16))

        qn, kv_cmp, ks, vs, kw, vw, gates, cq, ckv, kr = _proj(
            xt, mix_norm[l][None, :], _layout_w_in(w_in[l]), nsa_q_norm[l][None, :], nsa_k_norm[l],
            mla_q_a_norm[l][None, :], mla_kv_a_norm[l][None, :])

        kc, vc = _compress(
            kv_cmp.reshape(b, s // CMP_STRIDE, CMP_STRIDE * 2 * NSA_KV_WIDTH),
            nsa_cmp_w1_k[l].astype(BF16), nsa_cmp_w2_k[l].astype(BF16), nsa_cmp_pos_k[l].reshape(1, -1),
            nsa_cmp_w1_v[l].astype(BF16), nsa_cmp_w2_v[l].astype(BF16), nsa_cmp_pos_v[l].reshape(1, -1),
            nsa_k_norm[l])

        def bs(a):
            return a.reshape(b, s, a.shape[-1])

        o_a = _nsa(bs(qn), kc, vc, bs(ks), bs(vs), bs(kw), bs(vw), bs(gates), out_norm_nsa[l][None, :])

        q_m, k_m, v_m = _mla_prep(
            bs(cq), bs(ckv), bs(kr), cos, sin, _layout_w_uq(mla_w_uq[l]), mla_w_ukv[l].astype(BF16),
            _rope_gains(mla_q_norm[l]) + _rope_gains(mla_k_norm[l]))
        o_b = _mla_attn(q_m, k_m, v_m)

        xt = _out_proj(xt, o_a.reshape(b * s, NSA_WIDTH), o_b.reshape(b * s, MLA_WIDTH),
                       out_norm_mla[l][None, :], w_out[l].astype(BF16))

        xt = _ffn(xt, ffn2_norm[l][None, :], ffn2_w_gate[l].astype(BF16), ffn2_w_up[l].astype(BF16),
                  ffn2_w_down[l].astype(BF16))
    return xt.reshape(b, s, d)
```

```python
import functools

import numpy as np
import jax
import jax.numpy as jnp
from jax import lax
from jax.experimental import pallas as pl
from jax.experimental.pallas import tpu as pltpu

F32 = jnp.float32
BF16 = jnp.bfloat16

HEAD_DIM = 128
NSA_HEADS = 8
NSA_KV_HEADS = 2
NSA_GROUP = NSA_HEADS // NSA_KV_HEADS
CMP_BLOCK = 32
CMP_STRIDE = 16
CMP_HIDDEN = 256
SEL_BLOCK = 64
SEL_SHIFT = 6
SEL_TOP = 8
WINDOW = 512
MLA_HEADS = 8
MLA_Q_RANK = 384
MLA_KV_RANK = 256
MLA_NOPE_DIM = 128
MLA_ROPE_DIM = 64
MLA_V_DIM = 128
MLA_QK_DIM = MLA_NOPE_DIM + MLA_ROPE_DIM
ROPE_THETA = 10000.0
EPS = 1e-6
NEG = -1e30
FORCE_SCORE = 1e4
NSA_WIDTH = NSA_HEADS * HEAD_DIM
MLA_WIDTH = MLA_HEADS * MLA_V_DIM
NSA_KV_WIDTH = NSA_KV_HEADS * HEAD_DIM

LANES = 128
MLA_QK_PAD = 2 * LANES
VMEM_LIMIT = 56 * 1024 * 1024


def _cparams(sem):
    return pltpu.CompilerParams(dimension_semantics=sem, vmem_limit_bytes=VMEM_LIMIT)


def _rms(x, gain):
    return x * lax.rsqrt(jnp.mean(x * x, axis=-1, keepdims=True) + EPS) * gain


def _dot(a, b):
    return jnp.dot(a, b, preferred_element_type=F32)


def _dot_nt(a, b):
    return lax.dot_general(a, b, (((1,), (1,)), ((), ())), preferred_element_type=F32)


def _ffn_kernel(x_ref, g_ref, wg_ref, wu_ref, wd_ref, o_ref, h_ref):
    @pl.when(pl.program_id(1) == 0)
    def _():
        x = x_ref[...]
        h_ref[...] = _rms(x, g_ref[...]).astype(BF16)
        o_ref[...] = x

    h = h_ref[...]
    gate = _dot(h, wg_ref[...])
    up = _dot(h, wu_ref[...])
    act = gate * (1.0 / (1.0 + jnp.exp(-gate))) * up * 0.5
    o_ref[...] += _dot(act.astype(BF16), wd_ref[...])


def _ffn(x, gain, wg, wu, wd, *, tm=1024, tf=512):
    t, d = x.shape
    f = wg.shape[1]
    return pl.pallas_call(
        _ffn_kernel,
        grid=(t // tm, f // tf),
        in_specs=[
            pl.BlockSpec((tm, d), lambda i, j: (i, 0)),
            pl.BlockSpec((1, d), lambda i, j: (0, 0)),
            pl.BlockSpec((d, tf), lambda i, j: (0, j)),
            pl.BlockSpec((d, tf), lambda i, j: (0, j)),
            pl.BlockSpec((tf, d), lambda i, j: (j, 0)),
        ],
        out_specs=pl.BlockSpec((tm, d), lambda i, j: (i, 0)),
        out_shape=jax.ShapeDtypeStruct((t, d), F32),
        scratch_shapes=[pltpu.VMEM((tm, d), BF16)],
        compiler_params=_cparams(("parallel", "arbitrary")),
        name="ffn",
    )(x, gain, wg, wu, wd)


_P_Q = 0
_P_CMP = _P_Q + NSA_WIDTH
_P_KS = _P_CMP + 2 * NSA_KV_WIDTH
_P_VS = _P_KS + NSA_KV_WIDTH
_P_KW = _P_VS + NSA_KV_WIDTH
_P_VW = _P_KW + NSA_KV_WIDTH
_P_CQ = _P_VW + NSA_KV_WIDTH
_P_CKV = _P_CQ + MLA_Q_RANK
_P_KR = _P_CKV + MLA_KV_RANK
_P_GATE = _P_KR + 2 * LANES
_P_END = _P_GATE + LANES


def _proj_kernel(x_ref, g_ref, w_ref, qg_ref, kg_ref, qag_ref, kvag_ref,
                 qn_ref, cmp_ref, ks_ref, vs_ref, kw_ref, vw_ref, gate_ref, cq_ref, ckv_ref, kr_ref):
    h = _rms(x_ref[...], g_ref[...]).astype(BF16)

    def proj(lo, hi):
        return _dot(h, w_ref[:, lo:hi])

    qscale = HEAD_DIM ** -0.5
    for hd in range(NSA_HEADS):
        lo = _P_Q + hd * HEAD_DIM
        qn_ref[:, hd * HEAD_DIM:(hd + 1) * HEAD_DIM] = (
            _rms(proj(lo, lo + HEAD_DIM), qg_ref[...]) * qscale).astype(BF16)
    cmp_ref[...] = proj(_P_CMP, _P_KS).astype(BF16)
    for g in range(NSA_KV_HEADS):
        sl = slice(g * HEAD_DIM, (g + 1) * HEAD_DIM)
        ks_ref[:, sl] = _rms(proj(_P_KS + g * HEAD_DIM, _P_KS + (g + 1) * HEAD_DIM), kg_ref[1:2, :]).astype(BF16)
        kw_ref[:, sl] = _rms(proj(_P_KW + g * HEAD_DIM, _P_KW + (g + 1) * HEAD_DIM), kg_ref[2:3, :]).astype(BF16)
    vs_ref[...] = proj(_P_VS, _P_KW).astype(BF16)
    vw_ref[...] = proj(_P_VW, _P_CQ).astype(BF16)
    cq_ref[...] = _rms(proj(_P_CQ, _P_CKV), qag_ref[...]).astype(BF16)
    ckv_ref[...] = _rms(proj(_P_CKV, _P_KR), kvag_ref[...]).astype(BF16)
    kr_ref[...] = proj(_P_KR, _P_GATE)
    gate_ref[...] = 1.0 / (1.0 + jnp.exp(-proj(_P_GATE, _P_END)))


def _proj(x, gain, w, q_gain, k_gains, qa_gain, kva_gain, *, tm=256):
    t, d = x.shape
    n = w.shape[1]

    def full(a):
        return pl.BlockSpec(a.shape, lambda i: (0,) * a.ndim)

    def rows(width):
        return pl.BlockSpec((tm, width), lambda i: (i, 0))

    widths = [(NSA_WIDTH, BF16), (2 * NSA_KV_WIDTH, BF16), (NSA_KV_WIDTH, BF16), (NSA_KV_WIDTH, BF16),
              (NSA_KV_WIDTH, BF16), (NSA_KV_WIDTH, BF16), (LANES, F32), (MLA_Q_RANK, BF16),
              (MLA_KV_RANK, BF16), (2 * LANES, F32)]
    return pl.pallas_call(
        _proj_kernel,
        grid=(t // tm,),
        in_specs=[rows(d), full(gain), full(w), full(q_gain), full(k_gains), full(qa_gain), full(kva_gain)],
        out_specs=[rows(wd) for wd, _ in widths],
        out_shape=[jax.ShapeDtypeStruct((t, wd), dt) for wd, dt in widths],
        compiler_params=_cparams(("parallel",)),
        name="proj",
    )(x, gain, w, q_gain, k_gains, qa_gain, kva_gain)


def _gelu_tanh(x):
    return 0.5 * x * (1.0 + jnp.tanh(np.sqrt(2.0 / np.pi) * (x + 0.044715 * (x * x * x))))


def _compress_kernel(x_ref, w1k_ref, w2k_ref, pk_ref, w1v_ref, w2v_ref, pv_ref, kg_ref, kc_ref, vc_ref):
    half = CMP_STRIDE * HEAD_DIM
    row_w = 2 * NSA_KV_WIDTH
    for which, (w1_ref, w2_ref, p_ref, o_ref) in enumerate(
            ((w1k_ref, w2k_ref, pk_ref, kc_ref), (w1v_ref, w2v_ref, pv_ref, vc_ref))):
        pos_bias = _dot(jnp.broadcast_to(p_ref[...], (8, 2 * half)).astype(BF16), w1_ref[...])[0:1, :]
        for g in range(NSA_KV_HEADS):
            col = which * NSA_KV_WIDTH + g * HEAD_DIM
            xg = jnp.concatenate(
                [x_ref[0, :, l * row_w + col:l * row_w + col + HEAD_DIM] for l in range(CMP_STRIDE)], axis=1)
            first = _dot(xg, w1_ref[0:half, :])
            second = _dot(xg, w1_ref[half:2 * half, :])
            nrow = second.shape[0]
            pre = first + pltpu.roll(second, nrow - 1, 0) + pos_bias
            out = _dot(_gelu_tanh(pre).astype(BF16), w2_ref[...])
            if which == 0:
                out = _rms(out, kg_ref[0:1, :])
            o_ref[0, g] = out.astype(BF16)


def _compress(xc, w1k, w2k, pk, w1v, w2v, pv, k_gains):
    b, nrow, width = xc.shape

    def full(a):
        return pl.BlockSpec(a.shape, lambda i: (0,) * a.ndim)

    out_sds = jax.ShapeDtypeStruct((b, NSA_KV_HEADS, nrow, HEAD_DIM), BF16)
    out_spec = pl.BlockSpec((1, NSA_KV_HEADS, nrow, HEAD_DIM), lambda i: (i, 0, 0, 0))
    return pl.pallas_call(
        _compress_kernel,
        grid=(b,),
        in_specs=[pl.BlockSpec((1, nrow, width), lambda i: (i, 0, 0)),
                  full(w1k), full(w2k), full(pk), full(w1v), full(w2v), full(pv), full(k_gains)],
        out_specs=[out_spec, out_spec],
        out_shape=[out_sds, out_sds],
        compiler_params=_cparams(("parallel",)),
        name="compress",
    )(xc, w1k, w2k, pk, w1v, w2v, pv, k_gains)


def _flash(q, k_tile, v_tile, bias_mask, lo, hi, rows, d_v):
    def body(j, carry):
        m, l, acc = carry
        s = _dot_nt(q, k_tile(j))
        bias, mask = bias_mask(j)
        s = jnp.where(mask, s - bias, NEG)
        m_new = jnp.maximum(m, jnp.max(s, axis=-1, keepdims=True))
        p = jnp.where(mask, jnp.exp(s - m_new), 0.0)
        alpha = jnp.exp(m - m_new)
        l = alpha * l + jnp.sum(p, axis=-1, keepdims=True)
        acc = alpha * acc + _dot(p.astype(BF16), v_tile(j))
        return m_new, l, acc

    init = (jnp.full((rows, 1), NEG, F32), jnp.zeros((rows, 1), F32), jnp.zeros((rows, d_v), F32))
    m, l, acc = lax.fori_loop(lo, hi, body, init)
    return acc / jnp.where(l > 0.0, l, 1.0)


def _nsa_kernel(q_ref, kc_ref, vc_ref, ks_ref, vs_ref, kw_ref, vw_ref, gate_ref, og_ref, o_ref, *, tq, tk_sel):
    qi = pl.program_id(1)
    q0 = qi * tq
    rows = NSA_GROUP * tq
    ncmp = kc_ref.shape[2]
    n_sel_pad = LANES

    t_q = q0 + lax.broadcasted_iota(jnp.int32, (tq, 1), 0)
    gates = gate_ref[0]
    outs = []
    for g in range(NSA_KV_HEADS):
        qg = jnp.concatenate(
            [q_ref[0, :, (g * NSA_GROUP + r) * HEAD_DIM:(g * NSA_GROUP + r + 1) * HEAD_DIM] for r in range(NSA_GROUP)],
            axis=0)
        slopes = [2.0 ** (-(g * NSA_GROUP + r + 1)) for r in range(NSA_GROUP)]

        def stack_bias(dist):
            return jnp.concatenate([s * dist for s in slopes], axis=0)

        def stack4(a):
            return jnp.concatenate([a] * NSA_GROUP, axis=0)

        n_idx = lax.broadcasted_iota(jnp.int32, (tq, ncmp), 1)
        dist_c = t_q.astype(F32) - (n_idx.astype(F32) * CMP_STRIDE + 0.5 * (CMP_BLOCK - 1))
        mask_c = stack4((n_idx * CMP_STRIDE + (CMP_BLOCK - 1)) <= t_q)
        s_c = _dot_nt(qg, kc_ref[0, g]) - stack_bias(dist_c)
        s_c = jnp.where(mask_c, s_c, NEG)
        m_c = jnp.max(s_c, axis=-1, keepdims=True)
        p_c = jnp.where(mask_c, jnp.exp(s_c - m_c), 0.0)
        l_c = jnp.sum(p_c, axis=-1, keepdims=True)
        p_c = p_c / jnp.where(l_c > 0.0, l_c, 1.0)
        o_c = _dot(p_c.astype(BF16), vc_ref[0, g])

        p_sum = p_c[0:tq]
        for r in range(1, NSA_GROUP):
            p_sum = p_sum + p_c[r * tq:(r + 1) * tq]
        n_row = lax.broadcasted_iota(jnp.int32, (ncmp, n_sel_pad), 0)
        s_col = lax.broadcasted_iota(jnp.int32, (ncmp, n_sel_pad), 1)
        overlap = jnp.where((n_row * CMP_STRIDE < (s_col + 1) * SEL_BLOCK)
                            & (n_row * CMP_STRIDE + (CMP_BLOCK - 1) >= s_col * SEL_BLOCK), 1.0, 0.0).astype(BF16)
        p_hi = p_sum.astype(BF16)
        p_lo = (p_sum - p_hi.astype(F32)).astype(BF16)
        imp = _dot(p_hi, overlap) + _dot(p_lo, overlap)
        n_sel = ks_ref.shape[1] // SEL_BLOCK
        blk = lax.broadcasted_iota(jnp.int32, (tq, n_sel_pad), 1)
        forced = (blk == 0) | (blk == jnp.right_shift(t_q, SEL_SHIFT))
        future = blk * SEL_BLOCK > t_q
        imp = jnp.where(future, NEG, jnp.where(forced, FORCE_SCORE, imp))
        imp = jnp.where(blk < n_sel, imp, -jnp.inf)
        sel = jnp.zeros((tq, n_sel_pad), F32)
        for _ in range(min(SEL_TOP, n_sel)):
            mx = jnp.max(imp, axis=-1, keepdims=True)
            first = jnp.min(jnp.where(imp == mx, blk, n_sel_pad), axis=-1, keepdims=True)
            pick = blk == first
            sel = jnp.where(pick, 1.0, sel)
            imp = jnp.where(pick, -jnp.inf, imp)
        sel_b = sel.astype(BF16)

        def sel_bias_mask(j):
            kpos = j * tk_sel + lax.broadcasted_iota(jnp.int32, (tq, tk_sel), 1)
            dist = t_q - kpos
            e_row = lax.broadcasted_iota(jnp.int32, (n_sel_pad, tk_sel), 0)
            e_col = j * tk_sel + lax.broadcasted_iota(jnp.int32, (n_sel_pad, tk_sel), 1)
            expand = jnp.where(e_row == jnp.right_shift(e_col, SEL_SHIFT), 1.0, 0.0).astype(BF16)
            chosen = _dot(sel_b, expand) > 0.5
            return stack_bias(dist.astype(F32)), stack4(chosen & (dist >= 0))

        gsl = slice(g * HEAD_DIM, (g + 1) * HEAD_DIM)
        o_s = _flash(qg,
                     lambda j: ks_ref[0, pl.ds(pl.multiple_of(j * tk_sel, tk_sel), tk_sel), gsl],
                     lambda j: vs_ref[0, pl.ds(pl.multiple_of(j * tk_sel, tk_sel), tk_sel), gsl],
                     sel_bias_mask, 0, (q0 + tq + tk_sel - 1) // tk_sel, rows, HEAD_DIM)

        def win_bias_mask(j):
            kpos = j * tq + lax.broadcasted_iota(jnp.int32, (tq, tq), 1)
            dist = t_q - kpos
            return stack_bias(dist.astype(F32)), stack4((dist >= 0) & (dist < WINDOW))

        o_w = _flash(qg,
                     lambda j: kw_ref[0, pl.ds(pl.multiple_of(j * tq, tq), tq), gsl],
                     lambda j: vw_ref[0, pl.ds(pl.multiple_of(j * tq, tq), tq), gsl],
                     win_bias_mask, jnp.maximum(qi - WINDOW // tq, 0), qi + 1, rows, HEAD_DIM)

        for r in range(NSA_GROUP):
            hd = g * NSA_GROUP + r
            rs = slice(r * tq, (r + 1) * tq)
            outs.append(gates[:, 3 * hd:3 * hd + 1] * o_c[rs]
                        + gates[:, 3 * hd + 1:3 * hd + 2] * o_s[rs]
                        + gates[:, 3 * hd + 2:3 * hd + 3] * o_w[rs])

    ssq = outs[0] * outs[0]
    for o in outs[1:]:
        ssq = ssq + o * o
    inv = lax.rsqrt(jnp.sum(ssq, axis=-1, keepdims=True) / NSA_WIDTH + EPS)
    for hd, o in enumerate(outs):
        sl = slice(hd * HEAD_DIM, (hd + 1) * HEAD_DIM)
        o_ref[0, :, sl] = (o * inv * og_ref[:, sl]).astype(BF16)


def _nsa(qn, kc, vc, ks, vs, kw, vw, gates, out_gain, *, tq=128, tk_sel=256):
    b, s, _ = qn.shape

    def per_b(a):
        return pl.BlockSpec((1,) + a.shape[1:], lambda i, j: (i,) + (0,) * (a.ndim - 1))

    def q_rows(width):
        return pl.BlockSpec((1, tq, width), lambda i, j: (i, j, 0))

    return pl.pallas_call(
        functools.partial(_nsa_kernel, tq=tq, tk_sel=tk_sel),
        grid=(b, s // tq),
        in_specs=[q_rows(NSA_WIDTH), per_b(kc), per_b(vc), per_b(ks), per_b(vs), per_b(kw), per_b(vw),
                  q_rows(LANES), pl.BlockSpec(out_gain.shape, lambda i, j: (0, 0))],
        out_specs=q_rows(NSA_WIDTH),
        out_shape=jax.ShapeDtypeStruct((b, s, NSA_WIDTH), BF16),
        compiler_params=_cparams(("parallel", "arbitrary")),
        name="nsa",
    )(qn, kc, vc, ks, vs, kw, vw, gates, out_gain)


def _mla_prep_kernel(cq_ref, ckv_ref, kr_ref, cos_ref, sin_ref, cos_t_ref, sin_t_ref, wq_t_ref, wqr_t_ref, wk_ref,
                     wv_t_ref, qgn_ref, qgr_ref, qgt_ref, kgn_ref, kgr_ref, kgt_ref, q_t_ref, k_ref, v_t_ref):
    cq = cq_ref[0]
    ckv = ckv_ref[0]
    k_rope = kr_ref[0, :, 0:LANES]
    k_rot = kr_ref[0, :, LANES:2 * LANES]
    k_rope_ssq = jnp.sum(k_rope * k_rope, axis=-1, keepdims=True)
    k_rope_emb = k_rope * kgr_ref[...] * cos_ref[...] + k_rot * kgt_ref[...] * sin_ref[...]
    q_cos = qgr_ref[...] * cos_t_ref[...]
    q_sin = qgt_ref[...] * sin_t_ref[...]
    qscale = MLA_QK_DIM ** -0.5
    for hd in range(MLA_HEADS):
        qx = _dot_nt(wq_t_ref[hd * MLA_QK_PAD:(hd + 1) * MLA_QK_PAD, :], cq)
        q_rot = _dot_nt(wqr_t_ref[hd * LANES:(hd + 1) * LANES, :], cq)
        q_nope, q_rope = qx[0:LANES], qx[LANES:2 * LANES]
        ssq = jnp.sum(q_nope * q_nope, axis=0, keepdims=True) + jnp.sum(q_rope * q_rope, axis=0, keepdims=True)
        inv = lax.rsqrt(ssq / MLA_QK_DIM + EPS) * qscale
        q_t_ref[0, hd, 0:LANES, :] = (q_nope * inv * qgn_ref[...]).astype(BF16)
        q_t_ref[0, hd, LANES:2 * LANES, :] = ((q_rope * q_cos + q_rot * q_sin) * inv).astype(BF16)

        k_nope = _dot(ckv, wk_ref[:, hd * LANES:(hd + 1) * LANES])
        ssq = jnp.sum(k_nope * k_nope, axis=-1, keepdims=True) + k_rope_ssq
        inv = lax.rsqrt(ssq / MLA_QK_DIM + EPS)
        k_ref[0, hd, :, 0:LANES] = (k_nope * inv * kgn_ref[...]).astype(BF16)
        k_ref[0, hd, :, LANES:2 * LANES] = (k_rope_emb * inv).astype(BF16)
        v_t_ref[0, hd] = _dot_nt(wv_t_ref[hd * LANES:(hd + 1) * LANES, :], ckv).astype(BF16)


def _mla_prep(cq, ckv, kr, tables, weights, q_gains, k_gains, *, tm=256):
    b, s, _ = cq.shape
    cos, sin = tables
    q_gains = tuple(jnp.broadcast_to(g.reshape(LANES, 1), (LANES, tm)) for g in q_gains)

    def full(a):
        return pl.BlockSpec(a.shape, lambda i, j: (0,) * a.ndim)

    def rows(width):
        return pl.BlockSpec((1, tm, width), lambda i, j: (i, j, 0))

    pos_rows = pl.BlockSpec((tm, LANES), lambda i, j: (j, 0))
    pos_cols = pl.BlockSpec((LANES, tm), lambda i, j: (0, j))
    return pl.pallas_call(
        _mla_prep_kernel,
        grid=(b, s // tm),
        in_specs=[rows(MLA_Q_RANK), rows(MLA_KV_RANK), rows(2 * LANES), pos_rows, pos_rows, pos_cols, pos_cols]
                 + [full(w) for w in weights] + [full(g) for g in q_gains + k_gains],
        out_specs=[pl.BlockSpec((1, MLA_HEADS, MLA_QK_PAD, tm), lambda i, j: (i, 0, 0, j)),
                   pl.BlockSpec((1, MLA_HEADS, tm, MLA_QK_PAD), lambda i, j: (i, 0, j, 0)),
                   pl.BlockSpec((1, MLA_HEADS, MLA_V_DIM, tm), lambda i, j: (i, 0, 0, j))],
        out_shape=[jax.ShapeDtypeStruct((b, MLA_HEADS, MLA_QK_PAD, s), BF16),
                   jax.ShapeDtypeStruct((b, MLA_HEADS, s, MLA_QK_PAD), BF16),
                   jax.ShapeDtypeStruct((b, MLA_HEADS, MLA_V_DIM, s), BF16)],
        compiler_params=_cparams(("parallel", "parallel")),
        name="mla_prep",
    )(cq, ckv, kr, cos, sin, cos.T, sin.T, *weights, *q_gains, *k_gains)


def _mla_attn_kernel(q_t_ref, k_ref, v_t_ref, o_ref, *, tq, ck, hb):
    q0 = pl.program_id(2) * tq
    n_full = q0 // ck
    rel = lax.broadcasted_iota(jnp.int32, (ck, tq), 0) - lax.broadcasted_iota(jnp.int32, (ck, tq), 1)
    diag_mask = rel <= q0 - n_full * ck

    def scores(h, c0):
        return _dot(k_ref[0, h, pl.ds(c0, ck), :], q_t_ref[0, h])

    def weighted_v(h, c0, p):
        return _dot(v_t_ref[0, h, :, pl.ds(c0, ck)], p.astype(BF16))

    carry = []
    c_diag = pl.multiple_of(n_full * ck, ck)
    for h in range(hb):
        s = jnp.where(diag_mask, scores(h, c_diag), NEG)
        m = jnp.max(s, axis=0, keepdims=True)
        p = jnp.exp(s - m)
        carry += [m, jnp.sum(p, axis=0, keepdims=True), weighted_v(h, c_diag, p)]

    def body(c, carry):
        c0 = pl.multiple_of(c * ck, ck)
        out = []
        for h in range(hb):
            m, l, acc = carry[3 * h:3 * h + 3]
            s = scores(h, c0)
            m_new = jnp.maximum(m, jnp.max(s, axis=0, keepdims=True))
            p = jnp.exp(s - m_new)
            alpha = jnp.exp(m - m_new)
            out += [m_new, alpha * l + jnp.sum(p, axis=0, keepdims=True), alpha * acc + weighted_v(h, c0, p)]
        return tuple(out)

    carry = lax.fori_loop(0, n_full, body, tuple(carry))
    d_v = v_t_ref.shape[2]
    for h in range(hb):
        m, l, acc = carry[3 * h:3 * h + 3]
        o_ref[0, :, h * d_v:(h + 1) * d_v] = (acc / l).T


def _mla_attn(q_t, k, v_t, *, tq=256, ck=512, hb=2):
    b, h, s, dq = k.shape
    dv = v_t.shape[2]
    return pl.pallas_call(
        functools.partial(_mla_attn_kernel, tq=tq, ck=ck, hb=hb),
        grid=(b, h // hb, s // tq),
        in_specs=[pl.BlockSpec((1, hb, dq, tq), lambda i, j, l: (i, j, 0, l)),
                  pl.BlockSpec((1, hb, s, dq), lambda i, j, l: (i, j, 0, 0)),
                  pl.BlockSpec((1, hb, dv, s), lambda i, j, l: (i, j, 0, 0))],
        out_specs=pl.BlockSpec((1, tq, hb * dv), lambda i, j, l: (i, l, j)),
        out_shape=jax.ShapeDtypeStruct((b, s, h * dv), F32),
        compiler_params=_cparams(("parallel", "parallel", "arbitrary")),
        name="mla_attn",
    )(q_t, k, v_t)


def _out_kernel(x_ref, oa_ref, ob_ref, gb_ref, w_ref, o_ref):
    na = oa_ref.shape[1]
    ob = _rms(ob_ref[...], gb_ref[...]).astype(BF16)
    o_ref[...] = x_ref[...] + _dot(oa_ref[...], w_ref[0:na, :]) + _dot(ob, w_ref[na:, :])


def _out_proj(x, oa, ob, gain_b, w, *, tm=256):
    t, d = x.shape

    def rows(width):
        return pl.BlockSpec((tm, width), lambda i: (i, 0))

    return pl.pallas_call(
        _out_kernel,
        grid=(t // tm,),
        in_specs=[rows(d), rows(oa.shape[1]), rows(ob.shape[1]),
                  pl.BlockSpec(gain_b.shape, lambda i: (0, 0)), pl.BlockSpec(w.shape, lambda i: (0, 0))],
        out_specs=rows(d),
        out_shape=jax.ShapeDtypeStruct((t, d), F32),
        compiler_params=_cparams(("parallel",)),
        name="out_proj",
    )(x, oa, ob, gain_b, w)


def _rot_cols(w):
    half = w.shape[1] // 2
    return jnp.concatenate([-w[:, half:], w[:, :half]], axis=1)


def _layout_w_in(w_in):
    d = w_in.shape[0]
    off = np.cumsum([0, NSA_WIDTH] + [NSA_KV_WIDTH] * 6 + [NSA_HEADS * 3, MLA_Q_RANK, MLA_KV_RANK, MLA_ROPE_DIM])
    q, kc, vc, ks, vs, kw, vw, gt, cq, ckv, kr = [w_in[:, off[i]:off[i + 1]] for i in range(11)]
    zr = jnp.zeros((d, LANES - MLA_ROPE_DIM), w_in.dtype)
    zg = jnp.zeros((d, LANES - NSA_HEADS * 3), w_in.dtype)
    return jnp.concatenate([q, kc, vc, ks, vs, kw, vw, cq, ckv, kr, zr, _rot_cols(kr), zr, gt, zg], axis=1).astype(BF16)


def _layout_mla_weights(w_uq, w_ukv):
    r = w_uq.shape[0]
    zr = jnp.zeros((r, LANES - MLA_ROPE_DIM), w_uq.dtype)
    main, rot, k_cols, v_cols = [], [], [], []
    for hd in range(MLA_HEADS):
        base = hd * MLA_QK_DIM
        rope = w_uq[:, base + MLA_NOPE_DIM:base + MLA_QK_DIM]
        main += [w_uq[:, base:base + MLA_NOPE_DIM], rope, zr]
        rot += [_rot_cols(rope), zr]
        kv = hd * (MLA_NOPE_DIM + MLA_V_DIM)
        k_cols.append(w_ukv[:, kv:kv + MLA_NOPE_DIM])
        v_cols.append(w_ukv[:, kv + MLA_NOPE_DIM:kv + MLA_NOPE_DIM + MLA_V_DIM])
    cat = lambda parts: jnp.concatenate(parts, axis=1).astype(BF16)
    return cat(main).T, cat(rot).T, cat(k_cols), cat(v_cols).T


def _rope_gains(gain):
    half = MLA_ROPE_DIM // 2
    zr = jnp.zeros((LANES - MLA_ROPE_DIM,), gain.dtype)
    g_rope = gain[MLA_NOPE_DIM:]
    return (gain[None, :MLA_NOPE_DIM], jnp.concatenate([g_rope, zr])[None, :],
            jnp.concatenate([g_rope[half:], g_rope[:half], zr])[None, :])


def _rope_tables(s):
    half = MLA_ROPE_DIM // 2
    inv = ROPE_THETA ** (-jnp.arange(half, dtype=F32) / half)
    ang = jnp.arange(s).astype(F32)[:, None] * inv[None, :]
    zr = jnp.zeros((s, LANES - MLA_ROPE_DIM), F32)
    cos, sin = jnp.cos(ang), jnp.sin(ang)
    return jnp.concatenate([cos, cos, zr], axis=1), jnp.concatenate([sin, sin, zr], axis=1)


def kernel(x, ffn1_norm, ffn1_w_gate, ffn1_w_up, ffn1_w_down, mix_norm, w_in, nsa_q_norm, nsa_k_norm, nsa_cmp_pos_k, nsa_cmp_w1_k, nsa_cmp_w2_k, nsa_cmp_pos_v, nsa_cmp_w1_v, nsa_cmp_w2_v, mla_q_a_norm, mla_w_uq, mla_kv_a_norm, mla_w_ukv, mla_q_norm, mla_k_norm, out_norm_nsa, out_norm_mla, w_out, ffn2_norm, ffn2_w_gate, ffn2_w_up, ffn2_w_down):
    b, s, d = x.shape
    depth = ffn1_norm.shape[0]
    cos, sin = _rope_tables(s)
    xt = x.reshape(b * s, d)
    for l in range(depth):
        xt = _ffn(xt, ffn1_norm[l][None, :], ffn1_w_gate[l].astype(BF16), ffn1_w_up[l].astype(BF16),
                  ffn1_w_down[l].astype(BF16))

        qn, kv_cmp, ks, vs, kw, vw, gates, cq, ckv, kr = _proj(
            xt, mix_norm[l][None, :], _layout_w_in(w_in[l]), nsa_q_norm[l][None, :], nsa_k_norm[l],
            mla_q_a_norm[l][None, :], mla_kv_a_norm[l][None, :])

        kc, vc = _compress(
            kv_cmp.reshape(b, s // CMP_STRIDE, CMP_STRIDE * 2 * NSA_KV_WIDTH),
            nsa_cmp_w1_k[l].astype(BF16), nsa_cmp_w2_k[l].astype(BF16), nsa_cmp_pos_k[l].reshape(1, -1),
            nsa_cmp_w1_v[l].astype(BF16), nsa_cmp_w2_v[l].astype(BF16), nsa_cmp_pos_v[l].reshape(1, -1),
            nsa_k_norm[l])

        def bs(a):
            return a.reshape(b, s, a.shape[-1])

        o_a = _nsa(bs(qn), kc, vc, bs(ks), bs(vs), bs(kw), bs(vw), bs(gates), out_norm_nsa[l][None, :])

        q_m, k_m, v_m = _mla_prep(
            bs(cq), bs(ckv), bs(kr), (cos, sin), _layout_mla_weights(mla_w_uq[l], mla_w_ukv[l]),
            _rope_gains(mla_q_norm[l]), _rope_gains(mla_k_norm[l]))
        o_b = _mla_attn(q_m, k_m, v_m)

        xt = _out_proj(xt, o_a.reshape(b * s, NSA_WIDTH), o_b.reshape(b * s, MLA_WIDTH),
                       out_norm_mla[l][None, :], w_out[l].astype(BF16))

        xt = _ffn(xt, ffn2_norm[l][None, :], ffn2_w_gate[l].astype(BF16), ffn2_w_up[l].astype(BF16),
                  ffn2_w_down[l].astype(BF16))
    return xt.reshape(b, s, d)
```

```python
import functools

import numpy as np
import jax
import jax.numpy as jnp
from jax import lax
from jax.experimental import pallas as pl
from jax.experimental.pallas import tpu as pltpu

F32 = jnp.float32
BF16 = jnp.bfloat16

HEAD_DIM = 128
NSA_HEADS = 8
NSA_KV_HEADS = 2
NSA_GROUP = NSA_HEADS // NSA_KV_HEADS
CMP_BLOCK = 32
CMP_STRIDE = 16
CMP_HIDDEN = 256
SEL_BLOCK = 64
SEL_SHIFT = 6
SEL_TOP = 8
WINDOW = 512
MLA_HEADS = 8
MLA_Q_RANK = 384
MLA_KV_RANK = 256
MLA_NOPE_DIM = 128
MLA_ROPE_DIM = 64
MLA_V_DIM = 128
MLA_QK_DIM = MLA_NOPE_DIM + MLA_ROPE_DIM
ROPE_THETA = 10000.0
EPS = 1e-6
NEG = -1e30
FORCE_SCORE = 1e4
NSA_WIDTH = NSA_HEADS * HEAD_DIM
MLA_WIDTH = MLA_HEADS * MLA_V_DIM
NSA_KV_WIDTH = NSA_KV_HEADS * HEAD_DIM

LANES = 128
MLA_QK_PAD = 2 * LANES
VMEM_LIMIT = 56 * 1024 * 1024
POS_SPLIT = 16
FEAT_ROWS = 16


def _cparams(sem):
    return pltpu.CompilerParams(dimension_semantics=sem, vmem_limit_bytes=VMEM_LIMIT)


def _rms(x, gain):
    return x * lax.rsqrt(jnp.mean(x * x, axis=-1, keepdims=True) + EPS) * gain


def _dot(a, b):
    return jnp.dot(a, b, preferred_element_type=F32)


def _dot_nt(a, b):
    return lax.dot_general(a, b, (((1,), (1,)), ((), ())), preferred_element_type=F32)


def _sigmoid(x):
    return 1.0 / (1.0 + jnp.exp(-x))


def _ffn_kernel(x_ref, g_ref, wg_ref, wu_ref, wd_ref, o_ref, h_ref):
    @pl.when(pl.program_id(1) == 0)
    def _():
        x = x_ref[...]
        h_ref[...] = _rms(x, g_ref[...]).astype(BF16)
        o_ref[...] = x

    h = h_ref[...]
    gate = _dot(h, wg_ref[...])
    up = _dot(h, wu_ref[...])
    act = gate * _sigmoid(gate) * up * 0.5
    o_ref[...] += _dot(act.astype(BF16), wd_ref[...])


def _ffn(x, gain, wg, wu, wd, *, tm=1024, tf=512):
    t, d = x.shape
    f = wg.shape[1]
    return pl.pallas_call(
        _ffn_kernel,
        grid=(t // tm, f // tf),
        in_specs=[
            pl.BlockSpec((tm, d), lambda i, j: (i, 0)),
            pl.BlockSpec((1, d), lambda i, j: (0, 0)),
            pl.BlockSpec((d, tf), lambda i, j: (0, j)),
            pl.BlockSpec((d, tf), lambda i, j: (0, j)),
            pl.BlockSpec((tf, d), lambda i, j: (j, 0)),
        ],
        out_specs=pl.BlockSpec((tm, d), lambda i, j: (i, 0)),
        out_shape=jax.ShapeDtypeStruct((t, d), F32),
        scratch_shapes=[pltpu.VMEM((tm, d), BF16)],
        compiler_params=_cparams(("parallel", "arbitrary")),
        name="ffn",
    )(x, gain, wg, wu, wd)


_N_CMP = 0
_N_KS = _N_CMP + 2 * NSA_KV_WIDTH
_N_KW = _N_KS + NSA_KV_WIDTH
_N_CQ = _N_KW + NSA_KV_WIDTH
_N_CKV = _N_CQ + MLA_Q_RANK
_N_KR = _N_CKV + MLA_KV_RANK
_N_END = _N_KR + 2 * LANES
_T_Q = 0
_T_VS = _T_Q + NSA_WIDTH
_T_VW = _T_VS + NSA_KV_WIDTH
_T_GATE = _T_VW + NSA_KV_WIDTH
_T_END = _T_GATE + LANES


def _proj_kernel(x_ref, g_ref, wn_ref, wt_ref, qg_ref, kg_ref, qag_ref, kvag_ref,
                 q_t_ref, cmp_ref, ks_ref, vs_t_ref, kw_ref, vw_t_ref, gate_t_ref, cq_ref, ckv_ref, kr_ref):
    h = _rms(x_ref[...], g_ref[...]).astype(BF16)

    def proj(lo, hi):
        return _dot(h, wn_ref[:, lo:hi])

    def proj_t(lo, hi):
        return _dot_nt(wt_ref[lo:hi, :], h)

    qscale = HEAD_DIM ** -0.5
    for hd in range(NSA_HEADS):
        q = proj_t(_T_Q + hd * HEAD_DIM, _T_Q + (hd + 1) * HEAD_DIM)
        inv = lax.rsqrt(jnp.mean(q * q, axis=0, keepdims=True) + EPS) * qscale
        q_t_ref[hd * HEAD_DIM:(hd + 1) * HEAD_DIM, :] = (q * inv * qg_ref[...]).astype(BF16)
    cmp_ref[...] = proj(_N_CMP, _N_KS).astype(BF16)
    for g in range(NSA_KV_HEADS):
        sl = slice(g * HEAD_DIM, (g + 1) * HEAD_DIM)
        ks_ref[:, sl] = _rms(proj(_N_KS + g * HEAD_DIM, _N_KS + (g + 1) * HEAD_DIM), kg_ref[1:2, :]).astype(BF16)
        kw_ref[:, sl] = _rms(proj(_N_KW + g * HEAD_DIM, _N_KW + (g + 1) * HEAD_DIM), kg_ref[2:3, :]).astype(BF16)
    vs_t_ref[...] = proj_t(_T_VS, _T_VW).astype(BF16)
    vw_t_ref[...] = proj_t(_T_VW, _T_GATE).astype(BF16)
    cq_ref[...] = _rms(proj(_N_CQ, _N_CKV), qag_ref[...]).astype(BF16)
    ckv_ref[...] = _rms(proj(_N_CKV, _N_KR), kvag_ref[...]).astype(BF16)
    kr_ref[...] = proj(_N_KR, _N_END)
    gate_t_ref[...] = _sigmoid(proj_t(_T_GATE, _T_END))


def _proj(x, gain, w_n, w_t, q_gain, k_gains, qa_gain, kva_gain, *, tm=256):
    t, d = x.shape
    q_gain = jnp.broadcast_to(q_gain.reshape(HEAD_DIM, 1), (HEAD_DIM, tm))

    def full(a):
        return pl.BlockSpec(a.shape, lambda i: (0,) * a.ndim)

    def rows(width):
        return pl.BlockSpec((tm, width), lambda i: (i, 0))

    def cols(height):
        return pl.BlockSpec((height, tm), lambda i: (0, i))

    outs = [(cols, NSA_WIDTH, BF16), (rows, 2 * NSA_KV_WIDTH, BF16), (rows, NSA_KV_WIDTH, BF16),
            (cols, NSA_KV_WIDTH, BF16), (rows, NSA_KV_WIDTH, BF16), (cols, NSA_KV_WIDTH, BF16),
            (cols, LANES, F32), (rows, MLA_Q_RANK, BF16), (rows, MLA_KV_RANK, BF16), (rows, 2 * LANES, F32)]
    return pl.pallas_call(
        _proj_kernel,
        grid=(t // tm,),
        in_specs=[rows(d), full(gain), full(w_n), full(w_t), full(q_gain), full(k_gains), full(qa_gain),
                  full(kva_gain)],
        out_specs=[kind(n) for kind, n, _ in outs],
        out_shape=[jax.ShapeDtypeStruct((t, n) if kind is rows else (n, t), dt) for kind, n, dt in outs],
        compiler_params=_cparams(("parallel",)),
        name="proj",
    )(x, gain, w_n, w_t, q_gain, k_gains, qa_gain, kva_gain)


def _gelu_tanh(x):
    return 0.5 * x * (1.0 + jnp.tanh(np.sqrt(2.0 / np.pi) * (x + 0.044715 * (x * x * x))))


def _compress_kernel(x_ref, w1k_ref, w2k_ref, pk_ref, w1v_ref, w2v_ref, pv_ref, kg_ref, kc_ref, vc_t_ref):
    half = CMP_STRIDE * HEAD_DIM
    row_w = 2 * NSA_KV_WIDTH
    for which, (w1_ref, w2_ref, p_ref) in enumerate(((w1k_ref, w2k_ref, pk_ref), (w1v_ref, w2v_ref, pv_ref))):
        pos_bias = _dot(jnp.broadcast_to(p_ref[...], (8, 2 * half)).astype(BF16), w1_ref[...])[0:1, :]
        for g in range(NSA_KV_HEADS):
            col = which * NSA_KV_WIDTH + g * HEAD_DIM
            xg = jnp.concatenate(
                [x_ref[0, :, l * row_w + col:l * row_w + col + HEAD_DIM] for l in range(CMP_STRIDE)], axis=1)
            first = _dot(xg, w1_ref[0:half, :])
            second = _dot(xg, w1_ref[half:2 * half, :])
            nrow = second.shape[0]
            pre = first + pltpu.roll(second, nrow - 1, 0) + pos_bias
            hidden = _gelu_tanh(pre).astype(BF16)
            if which == 0:
                kc_ref[0, g] = _rms(_dot(hidden, w2_ref[...]), kg_ref[0:1, :]).astype(BF16)
            else:
                vc_t_ref[0, g] = _dot_nt(w2_ref[...], hidden).astype(BF16)


def _compress(xc, w1k, w2k, pk, w1v, w2v, pv, k_gains):
    b, nrow, width = xc.shape

    def full(a):
        return pl.BlockSpec(a.shape, lambda i: (0,) * a.ndim)

    out_sds = jax.ShapeDtypeStruct((b, NSA_KV_HEADS, nrow, HEAD_DIM), BF16)
    out_spec = pl.BlockSpec((1, NSA_KV_HEADS, nrow, HEAD_DIM), lambda i: (i, 0, 0, 0))
    return pl.pallas_call(
        _compress_kernel,
        grid=(b,),
        in_specs=[pl.BlockSpec((1, nrow, width), lambda i: (i, 0, 0)),
                  full(w1k), full(w2k), full(pk), full(w1v), full(w2v), full(pv), full(k_gains)],
        out_specs=[out_spec, out_spec],
        out_shape=[out_sds, out_sds],
        compiler_params=_cparams(("parallel",)),
        name="compress",
    )(xc, w1k, w2k, pk, w1v, w2v, pv, k_gains)


def _softmax_cols(s):
    m = jnp.max(s, axis=0, keepdims=True)
    p = jnp.exp(s - m)
    return m, jnp.sum(p, axis=0, keepdims=True), p


def _nsa_kernel(q_t_ref, kc_ref, vc_t_ref, ks_ref, vs_t_ref, kw_ref, vw_t_ref, gate_t_ref, kfeat_ref, cfeat_ref,
                slope_ref, o_ref, *, tq, ck):
    qi = pl.program_id(1)
    q0 = qi * tq
    ng = NSA_KV_HEADS
    wide = NSA_GROUP * tq
    n_sel = ks_ref.shape[1] // SEL_BLOCK
    ncmp = kc_ref.shape[2]
    n_win = WINDOW + tq

    def tile4(a):
        return jnp.concatenate([a] * NSA_GROUP, axis=1)

    def masked(s, mask, fill=NEG):
        return jnp.concatenate([jnp.where(mask, s[:, r * tq:(r + 1) * tq], fill) for r in range(NSA_GROUP)], axis=1)

    def q_aug(g, extra):
        q = jnp.concatenate([q_t_ref[(g * NSA_GROUP + r) * HEAD_DIM:(g * NSA_GROUP + r + 1) * HEAD_DIM, :]
                             for r in range(NSA_GROUP)], axis=1)
        rows = [q, slope_ref[g]]
        used = slope_ref.shape[1]
        if extra is not None:
            rows.append(extra)
            used += extra.shape[0]
        rows.append(jnp.zeros((HEAD_DIM - used, wide), BF16))
        return jnp.concatenate(rows, axis=0)

    def gsl(g):
        return slice(g * HEAD_DIM, (g + 1) * HEAD_DIM)

    lane_q = q0 + lax.broadcasted_iota(jnp.int32, (1, tq), 1)

    qa_plain = [q_aug(g, None) for g in range(ng)]
    w0 = pl.multiple_of(jnp.maximum(q0 - WINDOW, 0), tq)
    s_cmp = [_dot(jnp.concatenate([kc_ref[0, g], cfeat_ref[...]], axis=1), qa_plain[g]) for g in range(ng)]
    s_win = [_dot(jnp.concatenate([kw_ref[0, pl.ds(w0, n_win), gsl(g)], kfeat_ref[pl.ds(w0, n_win), :]], axis=1),
                  qa_plain[g]) for g in range(ng)]

    n_row = lax.broadcasted_iota(jnp.int32, (ncmp, tq), 0)
    mask_c = n_row * CMP_STRIDE + (CMP_BLOCK - 1) <= lane_q
    s_row = lax.broadcasted_iota(jnp.int32, (n_sel, ncmp), 0)
    n_col = lax.broadcasted_iota(jnp.int32, (n_sel, ncmp), 1)
    overlap_t = jnp.where((n_col * CMP_STRIDE < (s_row + 1) * SEL_BLOCK)
                          & (n_col * CMP_STRIDE + (CMP_BLOCK - 1) >= s_row * SEL_BLOCK), 1.0, 0.0).astype(BF16)
    p_cmp, imp = [], []
    for g in range(ng):
        s = masked(s_cmp[g], mask_c)
        m = jnp.max(s, axis=0, keepdims=True)
        p = masked(jnp.exp(s - m), mask_c, 0.0)
        l = jnp.sum(p, axis=0, keepdims=True)
        p = p / jnp.where(l > 0.0, l, 1.0)
        p_cmp.append(p.astype(BF16))
        p_sum = p[:, 0:tq]
        for r in range(1, NSA_GROUP):
            p_sum = p_sum + p[:, r * tq:(r + 1) * tq]
        p_hi = p_sum.astype(BF16)
        p_lo = (p_sum - p_hi.astype(F32)).astype(BF16)
        imp.append(_dot(overlap_t, p_hi) + _dot(overlap_t, p_lo))
    o_cmp = [_dot(vc_t_ref[0, g], p_cmp[g]) for g in range(ng)]

    blk = lax.broadcasted_iota(jnp.int32, (n_sel, tq), 0)
    blk_f = blk.astype(F32)
    forced = (blk == 0) | (blk == jnp.right_shift(lane_q, SEL_SHIFT))
    future = blk * SEL_BLOCK > lane_q
    penalty = []
    for g in range(ng):
        v = jnp.where(future, NEG, jnp.where(forced, FORCE_SCORE, imp[g]))
        unselected = jnp.full((n_sel, tq), NEG, F32)
        for _ in range(min(SEL_TOP, n_sel)):
            mx = jnp.max(v, axis=0, keepdims=True)
            first = jnp.min(jnp.where(v == mx, blk_f, float(n_sel)), axis=0, keepdims=True)
            pick = blk_f == first
            unselected = jnp.where(pick, 0.0, unselected)
            v = jnp.where(pick, -jnp.inf, v)
        penalty.append(tile4(unselected.astype(BF16)))

    rel = lax.broadcasted_iota(jnp.int32, (n_win, tq), 0) - lax.broadcasted_iota(jnp.int32, (n_win, tq), 1)
    off = q0 - w0
    mask_w = (rel <= off) & (rel > off - WINDOW)
    o_win = []
    p_win = []
    for g in range(ng):
        _, l, p = _softmax_cols(masked(s_win[g], mask_w))
        p_win.append((l, p.astype(BF16)))
    for g in range(ng):
        l, p = p_win[g]
        o_win.append(_dot(vw_t_ref[gsl(g), pl.ds(w0, n_win)], p) / l)

    qa_sel = [q_aug(g, penalty[g]) for g in range(ng)]
    n_full = q0 // ck
    c_diag = pl.multiple_of(n_full * ck, ck)
    rel_d = lax.broadcasted_iota(jnp.int32, (ck, tq), 0) - lax.broadcasted_iota(jnp.int32, (ck, tq), 1)
    mask_d = rel_d <= q0 - c_diag

    def sel_scores(g, c0):
        return _dot(jnp.concatenate([ks_ref[0, pl.ds(c0, ck), gsl(g)], kfeat_ref[pl.ds(c0, ck), :]], axis=1),
                    qa_sel[g])

    s_d = [sel_scores(g, c_diag) for g in range(ng)]
    stats = [_softmax_cols(masked(s_d[g], mask_d)) for g in range(ng)]
    carry = []
    for g in range(ng):
        m, l, p = stats[g]
        carry += [m, l, _dot(vs_t_ref[gsl(g), pl.ds(c_diag, ck)], p.astype(BF16))]

    def body(c, carry):
        c0 = pl.multiple_of(c * ck, ck)
        ss = [sel_scores(g, c0) for g in range(ng)]
        staged = []
        for g in range(ng):
            m, l, acc = carry[3 * g:3 * g + 3]
            m_new = jnp.maximum(m, jnp.max(ss[g], axis=0, keepdims=True))
            p = jnp.exp(ss[g] - m_new)
            alpha = jnp.exp(m - m_new)
            staged.append((m_new, alpha * l + jnp.sum(p, axis=0, keepdims=True), alpha * acc, p.astype(BF16)))
        out = []
        for g in range(ng):
            m_new, l_new, acc_scaled, p = staged[g]
            out += [m_new, l_new, acc_scaled + _dot(vs_t_ref[gsl(g), pl.ds(c0, ck)], p)]
        return tuple(out)

    carry = lax.fori_loop(0, n_full, body, tuple(carry))

    for g in range(ng):
        o_sel = carry[3 * g + 2] / carry[3 * g + 1]
        for r in range(NSA_GROUP):
            hd = g * NSA_GROUP + r
            cs = slice(r * tq, (r + 1) * tq)
            mixed = (gate_t_ref[3 * hd:3 * hd + 1, :] * o_cmp[g][:, cs]
                     + gate_t_ref[3 * hd + 1:3 * hd + 2, :] * o_sel[:, cs]
                     + gate_t_ref[3 * hd + 2:3 * hd + 3, :] * o_win[g][:, cs])
            o_ref[0, :, hd * HEAD_DIM:(hd + 1) * HEAD_DIM] = mixed.T


def _nsa(q_t, kc, vc_t, ks, vs_t, kw, vw_t, gate_t, *, tq=128, ck=512):
    b, s, _ = ks.shape
    nq = s // tq
    pos = np.arange(s)
    kfeat = np.zeros((s, LANES), np.float32)
    kfeat[:, 0] = POS_SPLIT * (pos // POS_SPLIT)
    kfeat[:, 1] = pos % POS_SPLIT
    kfeat[pos, FEAT_ROWS + pos // SEL_BLOCK] = 1.0
    ncmp = kc.shape[2]
    cfeat = np.zeros((ncmp, LANES), np.float32)
    cfeat[:, 0] = CMP_STRIDE * np.arange(ncmp)
    cfeat[:, 1] = 0.5 * (CMP_BLOCK - 1)
    slope = np.zeros((NSA_KV_HEADS, FEAT_ROWS, NSA_GROUP * tq), np.float32)
    for hd in range(NSA_HEADS):
        g, r = divmod(hd, NSA_GROUP)
        slope[g, 0:2, r * tq:(r + 1) * tq] = 2.0 ** (-8.0 * (hd + 1) / NSA_HEADS)
    assert FEAT_ROWS + s // SEL_BLOCK <= LANES and s // SEL_BLOCK % FEAT_ROWS == 0 and s % ck == 0 and ck % tq == 0

    def full(a):
        return pl.BlockSpec(a.shape, lambda i, j: (0,) * a.ndim)

    def per_b(a):
        return pl.BlockSpec((1,) + a.shape[1:], lambda i, j: (i,) + (0,) * (a.ndim - 1))

    def per_b_cols(a):
        return pl.BlockSpec((a.shape[0], s), lambda i, j: (0, i))

    def q_cols(a):
        return pl.BlockSpec((a.shape[0], tq), lambda i, j: (0, i * nq + j))

    consts = [jnp.asarray(kfeat, BF16), jnp.asarray(cfeat, BF16), jnp.asarray(slope, BF16)]
    return pl.pallas_call(
        functools.partial(_nsa_kernel, tq=tq, ck=ck),
        grid=(b, nq),
        in_specs=[q_cols(q_t), per_b(kc), per_b(vc_t), per_b(ks), per_b_cols(vs_t), per_b(kw), per_b_cols(vw_t),
                  q_cols(gate_t)] + [full(c) for c in consts],
        out_specs=pl.BlockSpec((1, tq, NSA_WIDTH), lambda i, j: (i, j, 0)),
        out_shape=jax.ShapeDtypeStruct((b, s, NSA_WIDTH), F32),
        compiler_params=_cparams(("parallel", "arbitrary")),
        name="nsa",
    )(q_t, kc, vc_t, ks, vs_t, kw, vw_t, gate_t, *consts)


def _mla_prep_kernel(cq_ref, ckv_ref, kr_ref, cos_ref, sin_ref, cos_t_ref, sin_t_ref, wq_t_ref, wqr_t_ref, wk_ref,
                     wv_t_ref, qgn_ref, qgr_ref, qgt_ref, kgn_ref, kgr_ref, kgt_ref, q_t_ref, k_ref, v_t_ref):
    cq = cq_ref[0]
    ckv = ckv_ref[0]
    k_rope = kr_ref[0, :, 0:LANES]
    k_rot = kr_ref[0, :, LANES:2 * LANES]
    k_rope_ssq = jnp.sum(k_rope * k_rope, axis=-1, keepdims=True)
    k_rope_emb = k_rope * kgr_ref[...] * cos_ref[...] + k_rot * kgt_ref[...] * sin_ref[...]
    q_cos = qgr_ref[...] * cos_t_ref[...]
    q_sin = qgt_ref[...] * sin_t_ref[...]
    qscale = MLA_QK_DIM ** -0.5
    for hd in range(MLA_HEADS):
        qx = _dot_nt(wq_t_ref[hd * MLA_QK_PAD:(hd + 1) * MLA_QK_PAD, :], cq)
        q_rot = _dot_nt(wqr_t_ref[hd * LANES:(hd + 1) * LANES, :], cq)
        q_nope, q_rope = qx[0:LANES], qx[LANES:2 * LANES]
        ssq = jnp.sum(q_nope * q_nope, axis=0, keepdims=True) + jnp.sum(q_rope * q_rope, axis=0, keepdims=True)
        inv = lax.rsqrt(ssq / MLA_QK_DIM + EPS) * qscale
        q_t_ref[0, hd, 0:LANES, :] = (q_nope * inv * qgn_ref[...]).astype(BF16)
        q_t_ref[0, hd, LANES:2 * LANES, :] = ((q_rope * q_cos + q_rot * q_sin) * inv).astype(BF16)

        k_nope = _dot(ckv, wk_ref[:, hd * LANES:(hd + 1) * LANES])
        ssq = jnp.sum(k_nope * k_nope, axis=-1, keepdims=True) + k_rope_ssq
        inv = lax.rsqrt(ssq / MLA_QK_DIM + EPS)
        k_ref[0, hd, :, 0:LANES] = (k_nope * inv * kgn_ref[...]).astype(BF16)
        k_ref[0, hd, :, LANES:2 * LANES] = (k_rope_emb * inv).astype(BF16)
        v_t_ref[0, hd] = _dot_nt(wv_t_ref[hd * LANES:(hd + 1) * LANES, :], ckv).astype(BF16)


def _mla_prep(cq, ckv, kr, tables, weights, q_gains, k_gains, *, tm=256):
    b, s, _ = cq.shape
    cos, sin = tables
    q_gains = tuple(jnp.broadcast_to(g.reshape(LANES, 1), (LANES, tm)) for g in q_gains)

    def full(a):
        return pl.BlockSpec(a.shape, lambda i, j: (0,) * a.ndim)

    def rows(width):
        return pl.BlockSpec((1, tm, width), lambda i, j: (i, j, 0))

    pos_rows = pl.BlockSpec((tm, LANES), lambda i, j: (j, 0))
    pos_cols = pl.BlockSpec((LANES, tm), lambda i, j: (0, j))
    return pl.pallas_call(
        _mla_prep_kernel,
        grid=(b, s // tm),
        in_specs=[rows(MLA_Q_RANK), rows(MLA_KV_RANK), rows(2 * LANES), pos_rows, pos_rows, pos_cols, pos_cols]
                 + [full(w) for w in weights] + [full(g) for g in q_gains + k_gains],
        out_specs=[pl.BlockSpec((1, MLA_HEADS, MLA_QK_PAD, tm), lambda i, j: (i, 0, 0, j)),
                   pl.BlockSpec((1, MLA_HEADS, tm, MLA_QK_PAD), lambda i, j: (i, 0, j, 0)),
                   pl.BlockSpec((1, MLA_HEADS, MLA_V_DIM, tm), lambda i, j: (i, 0, 0, j))],
        out_shape=[jax.ShapeDtypeStruct((b, MLA_HEADS, MLA_QK_PAD, s), BF16),
                   jax.ShapeDtypeStruct((b, MLA_HEADS, s, MLA_QK_PAD), BF16),
                   jax.ShapeDtypeStruct((b, MLA_HEADS, MLA_V_DIM, s), BF16)],
        compiler_params=_cparams(("parallel", "parallel")),
        name="mla_prep",
    )(cq, ckv, kr, cos, sin, cos.T, sin.T, *weights, *q_gains, *k_gains)


def _mla_attn_kernel(q_t_ref, k_ref, v_t_ref, o_ref, *, tq, ck, hb):
    q0 = pl.program_id(2) * tq
    n_full = q0 // ck
    rel = lax.broadcasted_iota(jnp.int32, (ck, tq), 0) - lax.broadcasted_iota(jnp.int32, (ck, tq), 1)
    diag_mask = rel <= q0 - n_full * ck

    def scores(h, c0):
        return _dot(k_ref[0, h, pl.ds(c0, ck), :], q_t_ref[0, h])

    c_diag = pl.multiple_of(n_full * ck, ck)
    ss = [scores(h, c_diag) for h in range(hb)]
    stats = [_softmax_cols(jnp.where(diag_mask, ss[h], NEG)) for h in range(hb)]
    carry = []
    for h in range(hb):
        m, l, p = stats[h]
        carry += [m, l, _dot(v_t_ref[0, h, :, pl.ds(c_diag, ck)], p.astype(BF16))]

    def body(c, carry):
        c0 = pl.multiple_of(c * ck, ck)
        ss = [scores(h, c0) for h in range(hb)]
        staged = []
        for h in range(hb):
            m, l, acc = carry[3 * h:3 * h + 3]
            m_new = jnp.maximum(m, jnp.max(ss[h], axis=0, keepdims=True))
            p = jnp.exp(ss[h] - m_new)
            alpha = jnp.exp(m - m_new)
            staged.append((m_new, alpha * l + jnp.sum(p, axis=0, keepdims=True), alpha * acc, p.astype(BF16)))
        out = []
        for h in range(hb):
            m_new, l_new, acc_scaled, p = staged[h]
            out += [m_new, l_new, acc_scaled + _dot(v_t_ref[0, h, :, pl.ds(c0, ck)], p)]
        return tuple(out)

    carry = lax.fori_loop(0, n_full, body, tuple(carry))
    d_v = v_t_ref.shape[2]
    for h in range(hb):
        o_ref[0, :, h * d_v:(h + 1) * d_v] = (carry[3 * h + 2] / carry[3 * h + 1]).T


def _mla_attn(q_t, k, v_t, *, tq=256, ck=512, hb=4):
    b, h, s, dq = k.shape
    dv = v_t.shape[2]
    return pl.pallas_call(
        functools.partial(_mla_attn_kernel, tq=tq, ck=ck, hb=hb),
        grid=(b, h // hb, s // tq),
        in_specs=[pl.BlockSpec((1, hb, dq, tq), lambda i, j, l: (i, j, 0, l)),
                  pl.BlockSpec((1, hb, s, dq), lambda i, j, l: (i, j, 0, 0)),
                  pl.BlockSpec((1, hb, dv, s), lambda i, j, l: (i, j, 0, 0))],
        out_specs=pl.BlockSpec((1, tq, hb * dv), lambda i, j, l: (i, l, j)),
        out_shape=jax.ShapeDtypeStruct((b, s, h * dv), F32),
        compiler_params=_cparams(("parallel", "parallel", "arbitrary")),
        name="mla_attn",
    )(q_t, k, v_t)


def _out_kernel(x_ref, oa_ref, ob_ref, ga_ref, gb_ref, w_ref, o_ref):
    na = oa_ref.shape[1]
    oa = _rms(oa_ref[...], ga_ref[...]).astype(BF16)
    ob = _rms(ob_ref[...], gb_ref[...]).astype(BF16)
    o_ref[...] = x_ref[...] + _dot(oa, w_ref[0:na, :]) + _dot(ob, w_ref[na:, :])


def _out_proj(x, oa, ob, gain_a, gain_b, w, *, tm=256):
    t, d = x.shape

    def rows(width):
        return pl.BlockSpec((tm, width), lambda i: (i, 0))

    def full(a):
        return pl.BlockSpec(a.shape, lambda i: (0,) * a.ndim)

    return pl.pallas_call(
        _out_kernel,
        grid=(t // tm,),
        in_specs=[rows(d), rows(oa.shape[1]), rows(ob.shape[1]), full(gain_a), full(gain_b), full(w)],
        out_specs=rows(d),
        out_shape=jax.ShapeDtypeStruct((t, d), F32),
        compiler_params=_cparams(("parallel",)),
        name="out_proj",
    )(x, oa, ob, gain_a, gain_b, w)


def _rot_cols(w):
    half = w.shape[1] // 2
    return jnp.concatenate([-w[:, half:], w[:, :half]], axis=1)


def _layout_w_in(w_in):
    d = w_in.shape[0]
    off = np.cumsum([0, NSA_WIDTH] + [NSA_KV_WIDTH] * 6 + [NSA_HEADS * 3, MLA_Q_RANK, MLA_KV_RANK, MLA_ROPE_DIM])
    q, kc, vc, ks, vs, kw, vw, gt, cq, ckv, kr = [w_in[:, off[i]:off[i + 1]] for i in range(11)]
    zr = jnp.zeros((d, LANES - MLA_ROPE_DIM), w_in.dtype)
    zg = jnp.zeros((d, LANES - NSA_HEADS * 3), w_in.dtype)
    w_n = jnp.concatenate([kc, vc, ks, kw, cq, ckv, kr, zr, _rot_cols(kr), zr], axis=1).astype(BF16)
    w_t = jnp.concatenate([q, vs, vw, gt, zg], axis=1).astype(BF16).T
    return w_n, w_t


def _layout_mla_weights(w_uq, w_ukv):
    r = w_uq.shape[0]
    zr = jnp.zeros((r, LANES - MLA_ROPE_DIM), w_uq.dtype)
    main, rot, k_cols, v_cols = [], [], [], []
    for hd in range(MLA_HEADS):
        base = hd * MLA_QK_DIM
        rope = w_uq[:, base + MLA_NOPE_DIM:base + MLA_QK_DIM]
        main += [w_uq[:, base:base + MLA_NOPE_DIM], rope, zr]
        rot += [_rot_cols(rope), zr]
        kv = hd * (MLA_NOPE_DIM + MLA_V_DIM)
        k_cols.append(w_ukv[:, kv:kv + MLA_NOPE_DIM])
        v_cols.append(w_ukv[:, kv + MLA_NOPE_DIM:kv + MLA_NOPE_DIM + MLA_V_DIM])
    cat = lambda parts: jnp.concatenate(parts, axis=1).astype(BF16)
    return cat(main).T, cat(rot).T, cat(k_cols), cat(v_cols).T


def _rope_gains(gain):
    half = MLA_ROPE_DIM // 2
    zr = jnp.zeros((LANES - MLA_ROPE_DIM,), gain.dtype)
    g_rope = gain[MLA_NOPE_DIM:]
    return (gain[None, :MLA_NOPE_DIM], jnp.concatenate([g_rope, zr])[None, :],
            jnp.concatenate([g_rope[half:], g_rope[:half], zr])[None, :])


def _rope_tables(s):
    half = MLA_ROPE_DIM // 2
    inv = ROPE_THETA ** (-jnp.arange(half, dtype=F32) / half)
    ang = jnp.arange(s).astype(F32)[:, None] * inv[None, :]
    zr = jnp.zeros((s, LANES - MLA_ROPE_DIM), F32)
    cos, sin = jnp.cos(ang), jnp.sin(ang)
    return jnp.concatenate([cos, cos, zr], axis=1), jnp.concatenate([sin, sin, zr], axis=1)


def kernel(x, ffn1_norm, ffn1_w_gate, ffn1_w_up, ffn1_w_down, mix_norm, w_in, nsa_q_norm, nsa_k_norm, nsa_cmp_pos_k, nsa_cmp_w1_k, nsa_cmp_w2_k, nsa_cmp_pos_v, nsa_cmp_w1_v, nsa_cmp_w2_v, mla_q_a_norm, mla_w_uq, mla_kv_a_norm, mla_w_ukv, mla_q_norm, mla_k_norm, out_norm_nsa, out_norm_mla, w_out, ffn2_norm, ffn2_w_gate, ffn2_w_up, ffn2_w_down):
    b, s, d = x.shape
    depth = ffn1_norm.shape[0]
    tables = _rope_tables(s)
    xt = x.reshape(b * s, d)
    for l in range(depth):
        xt = _ffn(xt, ffn1_norm[l][None, :], ffn1_w_gate[l].astype(BF16), ffn1_w_up[l].astype(BF16),
                  ffn1_w_down[l].astype(BF16))

        w_n, w_t = _layout_w_in(w_in[l])
        q_t, kv_cmp, ks, vs_t, kw, vw_t, gate_t, cq, ckv, kr = _proj(
            xt, mix_norm[l][None, :], w_n, w_t, nsa_q_norm[l], nsa_k_norm[l],
            mla_q_a_norm[l][None, :], mla_kv_a_norm[l][None, :])

        kc, vc_t = _compress(
            kv_cmp.reshape(b, s // CMP_STRIDE, CMP_STRIDE * 2 * NSA_KV_WIDTH),
            nsa_cmp_w1_k[l].astype(BF16), nsa_cmp_w2_k[l].astype(BF16), nsa_cmp_pos_k[l].reshape(1, -1),
            nsa_cmp_w1_v[l].astype(BF16), nsa_cmp_w2_v[l].astype(BF16).T, nsa_cmp_pos_v[l].reshape(1, -1),
            nsa_k_norm[l])

        def bs(a):
            return a.reshape(b, s, a.shape[-1])

        o_a = _nsa(q_t, kc, vc_t, bs(ks), vs_t, bs(kw), vw_t, gate_t)

        q_m, k_m, v_m = _mla_prep(
            bs(cq), bs(ckv), bs(kr), tables, _layout_mla_weights(mla_w_uq[l], mla_w_ukv[l]),
            _rope_gains(mla_q_norm[l]), _rope_gains(mla_k_norm[l]))
        o_b = _mla_attn(q_m, k_m, v_m)

        xt = _out_proj(xt, o_a.reshape(b * s, NSA_WIDTH), o_b.reshape(b * s, MLA_WIDTH),
                       out_norm_nsa[l][None, :], out_norm_mla[l][None, :], w_out[l].astype(BF16))

        xt = _ffn(xt, ffn2_norm[l][None, :], ffn2_w_gate[l].astype(BF16), ffn2_w_up[l].astype(BF16),
                  ffn2_w_down[l].astype(BF16))
    return xt.reshape(b, s, d)
```

```python
import functools

import numpy as np
import jax
import jax.numpy as jnp
from jax import lax
from jax.experimental import pallas as pl
from jax.experimental.pallas import tpu as pltpu

F32 = jnp.float32
BF16 = jnp.bfloat16

HEAD_DIM = 128
NSA_HEADS = 8
NSA_KV_HEADS = 2
NSA_GROUP = NSA_HEADS // NSA_KV_HEADS
CMP_BLOCK = 32
CMP_STRIDE = 16
CMP_HIDDEN = 256
SEL_BLOCK = 64
SEL_SHIFT = 6
SEL_TOP = 8
WINDOW = 512
MLA_HEADS = 8
MLA_Q_RANK = 384
MLA_KV_RANK = 256
MLA_NOPE_DIM = 128
MLA_ROPE_DIM = 64
MLA_V_DIM = 128
MLA_QK_DIM = MLA_NOPE_DIM + MLA_ROPE_DIM
ROPE_THETA = 10000.0
EPS = 1e-6
NEG = -1e30
FORCE_SCORE = 1e4
NSA_WIDTH = NSA_HEADS * HEAD_DIM
MLA_WIDTH = MLA_HEADS * MLA_V_DIM
NSA_KV_WIDTH = NSA_KV_HEADS * HEAD_DIM

LANES = 128
MLA_QK_PAD = 2 * LANES
VMEM_LIMIT = 56 * 1024 * 1024
POS_SPLIT = 16
FEAT_ROWS = 16


def _cparams(sem):
    return pltpu.CompilerParams(dimension_semantics=sem, vmem_limit_bytes=VMEM_LIMIT)


def _rms(x, gain):
    return x * lax.rsqrt(jnp.mean(x * x, axis=-1, keepdims=True) + EPS) * gain


def _dot(a, b):
    return jnp.dot(a, b, preferred_element_type=F32)


def _dot_nt(a, b):
    return lax.dot_general(a, b, (((1,), (1,)), ((), ())), preferred_element_type=F32)


def _sigmoid(x):
    return 1.0 / (1.0 + jnp.exp(-x))


def _ffn_kernel(x_ref, g_ref, wg_ref, wu_ref, wd_ref, o_ref, h_ref):
    @pl.when(pl.program_id(1) == 0)
    def _():
        x = x_ref[...]
        h_ref[...] = _rms(x, g_ref[...]).astype(BF16)
        o_ref[...] = x

    h = h_ref[...]
    gate = _dot(h, wg_ref[...].astype(BF16))
    up = _dot(h, wu_ref[...].astype(BF16))
    act = gate * _sigmoid(gate) * up * 0.5
    o_ref[...] += _dot(act.astype(BF16), wd_ref[...].astype(BF16))


def _ffn(x, gain, wg, wu, wd, *, tm=1024, tf=256):
    t, d = x.shape
    f = wg.shape[1]
    return pl.pallas_call(
        _ffn_kernel,
        grid=(t // tm, f // tf),
        in_specs=[
            pl.BlockSpec((tm, d), lambda i, j: (i, 0)),
            pl.BlockSpec((1, d), lambda i, j: (0, 0)),
            pl.BlockSpec((d, tf), lambda i, j: (0, j)),
            pl.BlockSpec((d, tf), lambda i, j: (0, j)),
            pl.BlockSpec((tf, d), lambda i, j: (j, 0)),
        ],
        out_specs=pl.BlockSpec((tm, d), lambda i, j: (i, 0)),
        out_shape=jax.ShapeDtypeStruct((t, d), F32),
        scratch_shapes=[pltpu.VMEM((tm, d), BF16)],
        compiler_params=_cparams(("parallel", "arbitrary")),
        name="ffn",
    )(x, gain, wg, wu, wd)


_N_CMP = 0
_N_KS = _N_CMP + 2 * NSA_KV_WIDTH
_N_KW = _N_KS + NSA_KV_WIDTH
_N_CQ = _N_KW + NSA_KV_WIDTH
_N_CKV = _N_CQ + MLA_Q_RANK
_N_KR = _N_CKV + MLA_KV_RANK
_N_END = _N_KR + 2 * LANES
_T_Q = 0
_T_VS = _T_Q + NSA_WIDTH
_T_VW = _T_VS + NSA_KV_WIDTH
_T_GATE = _T_VW + NSA_KV_WIDTH
_T_END = _T_GATE + LANES


def _proj_kernel(x_ref, g_ref, wn_ref, wt_ref, qg_ref, kg_ref, qag_ref, kvag_ref,
                 q_t_ref, cmp_ref, ks_ref, vs_t_ref, kw_ref, vw_t_ref, gate_t_ref, cq_ref, ckv_ref, kr_ref):
    h = _rms(x_ref[...], g_ref[...]).astype(BF16)
    p_n = _dot(h, wn_ref[...])
    p_t = _dot_nt(wt_ref[...], h)

    def proj(lo, hi):
        return p_n[:, lo:hi]

    def proj_t(lo, hi):
        return p_t[lo:hi, :]

    qscale = HEAD_DIM ** -0.5
    for hd in range(NSA_HEADS):
        q = proj_t(_T_Q + hd * HEAD_DIM, _T_Q + (hd + 1) * HEAD_DIM)
        inv = lax.rsqrt(jnp.mean(q * q, axis=0, keepdims=True) + EPS) * qscale
        q_t_ref[hd * HEAD_DIM:(hd + 1) * HEAD_DIM, :] = (q * inv * qg_ref[...]).astype(BF16)
    cmp_ref[...] = proj(_N_CMP, _N_KS).astype(BF16)
    for g in range(NSA_KV_HEADS):
        sl = slice(g * HEAD_DIM, (g + 1) * HEAD_DIM)
        ks_ref[:, sl] = _rms(proj(_N_KS + g * HEAD_DIM, _N_KS + (g + 1) * HEAD_DIM), kg_ref[1:2, :]).astype(BF16)
        kw_ref[:, sl] = _rms(proj(_N_KW + g * HEAD_DIM, _N_KW + (g + 1) * HEAD_DIM), kg_ref[2:3, :]).astype(BF16)
    vs_t_ref[...] = proj_t(_T_VS, _T_VW).astype(BF16)
    vw_t_ref[...] = proj_t(_T_VW, _T_GATE).astype(BF16)
    cq_ref[...] = _rms(proj(_N_CQ, _N_CKV), qag_ref[...]).astype(BF16)
    ckv_ref[...] = _rms(proj(_N_CKV, _N_KR), kvag_ref[...]).astype(BF16)
    kr_ref[...] = proj(_N_KR, _N_END)
    gate_t_ref[...] = _sigmoid(proj_t(_T_GATE, _T_END))


def _proj(x, gain, w_n, w_t, q_gain, k_gains, qa_gain, kva_gain, *, tm=512):
    t, d = x.shape
    q_gain = jnp.broadcast_to(q_gain.reshape(HEAD_DIM, 1), (HEAD_DIM, tm))

    def full(a):
        return pl.BlockSpec(a.shape, lambda i: (0,) * a.ndim)

    def rows(width):
        return pl.BlockSpec((tm, width), lambda i: (i, 0))

    def cols(height):
        return pl.BlockSpec((height, tm), lambda i: (0, i))

    outs = [(cols, NSA_WIDTH, BF16), (rows, 2 * NSA_KV_WIDTH, BF16), (rows, NSA_KV_WIDTH, BF16),
            (cols, NSA_KV_WIDTH, BF16), (rows, NSA_KV_WIDTH, BF16), (cols, NSA_KV_WIDTH, BF16),
            (cols, LANES, F32), (rows, MLA_Q_RANK, BF16), (rows, MLA_KV_RANK, BF16), (rows, 2 * LANES, F32)]
    return pl.pallas_call(
        _proj_kernel,
        grid=(t // tm,),
        in_specs=[rows(d), full(gain), full(w_n), full(w_t), full(q_gain), full(k_gains), full(qa_gain),
                  full(kva_gain)],
        out_specs=[kind(n) for kind, n, _ in outs],
        out_shape=[jax.ShapeDtypeStruct((t, n) if kind is rows else (n, t), dt) for kind, n, dt in outs],
        compiler_params=_cparams(("parallel",)),
        name="proj",
    )(x, gain, w_n, w_t, q_gain, k_gains, qa_gain, kva_gain)


def _gelu_tanh(x):
    return 0.5 * x * (1.0 + jnp.tanh(np.sqrt(2.0 / np.pi) * (x + 0.044715 * (x * x * x))))


def _compress_kernel(x_ref, w1k_ref, w2k_ref, pk_ref, w1v_ref, w2v_ref, pv_ref, kg_ref, kc_ref, vc_t_ref):
    half = CMP_STRIDE * HEAD_DIM
    row_w = 2 * NSA_KV_WIDTH
    for which, (w1_ref, w2_ref, p_ref) in enumerate(((w1k_ref, w2k_ref, pk_ref), (w1v_ref, w2v_ref, pv_ref))):
        pos_bias = _dot(jnp.broadcast_to(p_ref[...], (8, 2 * half)).astype(BF16), w1_ref[...])[0:1, :]
        for g in range(NSA_KV_HEADS):
            col = which * NSA_KV_WIDTH + g * HEAD_DIM
            xg = jnp.concatenate(
                [x_ref[0, :, l * row_w + col:l * row_w + col + HEAD_DIM] for l in range(CMP_STRIDE)], axis=1)
            first = _dot(xg, w1_ref[0:half, :])
            second = _dot(xg, w1_ref[half:2 * half, :])
            nrow = second.shape[0]
            pre = first + pltpu.roll(second, nrow - 1, 0) + pos_bias
            hidden = _gelu_tanh(pre).astype(BF16)
            if which == 0:
                kc_ref[0, g] = _rms(_dot(hidden, w2_ref[...]), kg_ref[0:1, :]).astype(BF16)
            else:
                vc_t_ref[0, g] = _dot_nt(w2_ref[...], hidden).astype(BF16)


def _compress(xc, w1k, w2k, pk, w1v, w2v, pv, k_gains):
    b, nrow, width = xc.shape

    def full(a):
        return pl.BlockSpec(a.shape, lambda i: (0,) * a.ndim)

    out_sds = jax.ShapeDtypeStruct((b, NSA_KV_HEADS, nrow, HEAD_DIM), BF16)
    out_spec = pl.BlockSpec((1, NSA_KV_HEADS, nrow, HEAD_DIM), lambda i: (i, 0, 0, 0))
    return pl.pallas_call(
        _compress_kernel,
        grid=(b,),
        in_specs=[pl.BlockSpec((1, nrow, width), lambda i: (i, 0, 0)),
                  full(w1k), full(w2k), full(pk), full(w1v), full(w2v), full(pv), full(k_gains)],
        out_specs=[out_spec, out_spec],
        out_shape=[out_sds, out_sds],
        compiler_params=_cparams(("parallel",)),
        name="compress",
    )(xc, w1k, w2k, pk, w1v, w2v, pv, k_gains)


def _softmax_cols(s):
    m = jnp.max(s, axis=0, keepdims=True)
    p = jnp.exp(s - m)
    return m, jnp.sum(p, axis=0, keepdims=True), p


def _nsa_kernel(q_t_ref, kc_ref, vc_t_ref, ks_ref, vs_t_ref, kw_ref, vw_t_ref, gate_t_ref, kfeat_ref, cfeat_ref,
                slope_ref, o_ref, *, tq, ck):
    qi = pl.program_id(1)
    q0 = qi * tq
    ng = NSA_KV_HEADS
    wide = NSA_GROUP * tq
    n_sel = ks_ref.shape[1] // SEL_BLOCK
    ncmp = kc_ref.shape[2]
    n_win = WINDOW + tq

    def tile4(a):
        return jnp.concatenate([a] * NSA_GROUP, axis=1)

    def masked(s, mask, fill=NEG):
        return jnp.concatenate([jnp.where(mask, s[:, r * tq:(r + 1) * tq], fill) for r in range(NSA_GROUP)], axis=1)

    def q_aug(g, extra):
        q = jnp.concatenate([q_t_ref[(g * NSA_GROUP + r) * HEAD_DIM:(g * NSA_GROUP + r + 1) * HEAD_DIM, :]
                             for r in range(NSA_GROUP)], axis=1)
        rows = [q, slope_ref[g]]
        used = slope_ref.shape[1]
        if extra is not None:
            rows.append(extra)
            used += extra.shape[0]
        rows.append(jnp.zeros((HEAD_DIM - used, wide), BF16))
        return jnp.concatenate(rows, axis=0)

    def gsl(g):
        return slice(g * HEAD_DIM, (g + 1) * HEAD_DIM)

    lane_q = q0 + lax.broadcasted_iota(jnp.int32, (1, tq), 1)

    qa_plain = [q_aug(g, None) for g in range(ng)]
    w0 = pl.multiple_of(jnp.maximum(q0 - WINDOW, 0), tq)
    s_cmp = [_dot(jnp.concatenate([kc_ref[0, g], cfeat_ref[...]], axis=1), qa_plain[g]) for g in range(ng)]
    s_win = [_dot(jnp.concatenate([kw_ref[0, pl.ds(w0, n_win), gsl(g)], kfeat_ref[pl.ds(w0, n_win), :]], axis=1),
                  qa_plain[g]) for g in range(ng)]

    n_row = lax.broadcasted_iota(jnp.int32, (ncmp, tq), 0)
    mask_c = n_row * CMP_STRIDE + (CMP_BLOCK - 1) <= lane_q
    s_row = lax.broadcasted_iota(jnp.int32, (n_sel, ncmp), 0)
    n_col = lax.broadcasted_iota(jnp.int32, (n_sel, ncmp), 1)
    overlap_t = jnp.where((n_col * CMP_STRIDE < (s_row + 1) * SEL_BLOCK)
                          & (n_col * CMP_STRIDE + (CMP_BLOCK - 1) >= s_row * SEL_BLOCK), 1.0, 0.0).astype(BF16)
    p_cmp, imp = [], []
    for g in range(ng):
        s = masked(s_cmp[g], mask_c)
        m = jnp.max(s, axis=0, keepdims=True)
        p = masked(jnp.exp(s - m), mask_c, 0.0)
        l = jnp.sum(p, axis=0, keepdims=True)
        p = p / jnp.where(l > 0.0, l, 1.0)
        p_cmp.append(p.astype(BF16))
        p_sum = p[:, 0:tq]
        for r in range(1, NSA_GROUP):
            p_sum = p_sum + p[:, r * tq:(r + 1) * tq]
        p_hi = p_sum.astype(BF16)
        p_lo = (p_sum - p_hi.astype(F32)).astype(BF16)
        imp.append(_dot(overlap_t, p_hi) + _dot(overlap_t, p_lo))
    o_cmp = [_dot(vc_t_ref[0, g], p_cmp[g]) for g in range(ng)]

    blk = lax.broadcasted_iota(jnp.int32, (n_sel, tq), 0)
    blk_f = blk.astype(F32)
    forced = (blk == 0) | (blk == jnp.right_shift(lane_q, SEL_SHIFT))
    future = blk * SEL_BLOCK > lane_q
    penalty = []
    for g in range(ng):
        v = jnp.where(future, NEG, jnp.where(forced, FORCE_SCORE, imp[g]))
        unselected = jnp.full((n_sel, tq), NEG, F32)
        for _ in range(min(SEL_TOP, n_sel)):
            mx = jnp.max(v, axis=0, keepdims=True)
            first = jnp.min(jnp.where(v == mx, blk_f, float(n_sel)), axis=0, keepdims=True)
            pick = blk_f == first
            unselected = jnp.where(pick, 0.0, unselected)
            v = jnp.where(pick, -jnp.inf, v)
        penalty.append(tile4(unselected.astype(BF16)))

    rel = lax.broadcasted_iota(jnp.int32, (n_win, tq), 0) - lax.broadcasted_iota(jnp.int32, (n_win, tq), 1)
    off = q0 - w0
    mask_w = (rel <= off) & (rel > off - WINDOW)
    o_win = []
    p_win = []
    for g in range(ng):
        _, l, p = _softmax_cols(masked(s_win[g], mask_w))
        p_win.append((l, p.astype(BF16)))
    for g in range(ng):
        l, p = p_win[g]
        o_win.append(_dot(vw_t_ref[gsl(g), pl.ds(w0, n_win)], p) / l)

    qa_sel = [q_aug(g, penalty[g]) for g in range(ng)]
    n_full = q0 // ck
    c_diag = pl.multiple_of(n_full * ck, ck)
    rel_d = lax.broadcasted_iota(jnp.int32, (ck, tq), 0) - lax.broadcasted_iota(jnp.int32, (ck, tq), 1)
    mask_d = rel_d <= q0 - c_diag

    def sel_scores(g, c0):
        return _dot(jnp.concatenate([ks_ref[0, pl.ds(c0, ck), gsl(g)], kfeat_ref[pl.ds(c0, ck), :]], axis=1),
                    qa_sel[g])

    s_d = [sel_scores(g, c_diag) for g in range(ng)]
    stats = [_softmax_cols(masked(s_d[g], mask_d)) for g in range(ng)]
    carry = []
    for g in range(ng):
        m, l, p = stats[g]
        carry += [m, l, _dot(vs_t_ref[gsl(g), pl.ds(c_diag, ck)], p.astype(BF16))]

    def body(c, carry):
        c0 = pl.multiple_of(c * ck, ck)
        ss = [sel_scores(g, c0) for g in range(ng)]
        staged = []
        for g in range(ng):
            m, l, acc = carry[3 * g:3 * g + 3]
            m_new = jnp.maximum(m, jnp.max(ss[g], axis=0, keepdims=True))
            p = jnp.exp(ss[g] - m_new)
            alpha = jnp.exp(m - m_new)
            staged.append((m_new, alpha * l + jnp.sum(p, axis=0, keepdims=True), alpha * acc, p.astype(BF16)))
        out = []
        for g in range(ng):
            m_new, l_new, acc_scaled, p = staged[g]
            out += [m_new, l_new, acc_scaled + _dot(vs_t_ref[gsl(g), pl.ds(c0, ck)], p)]
        return tuple(out)

    carry = lax.fori_loop(0, n_full, body, tuple(carry))

    for g in range(ng):
        o_sel = carry[3 * g + 2] / carry[3 * g + 1]
        for r in range(NSA_GROUP):
            hd = g * NSA_GROUP + r
            cs = slice(r * tq, (r + 1) * tq)
            mixed = (gate_t_ref[3 * hd:3 * hd + 1, :] * o_cmp[g][:, cs]
                     + gate_t_ref[3 * hd + 1:3 * hd + 2, :] * o_sel[:, cs]
                     + gate_t_ref[3 * hd + 2:3 * hd + 3, :] * o_win[g][:, cs])
            o_ref[0, :, hd * HEAD_DIM:(hd + 1) * HEAD_DIM] = mixed.T


def _nsa(q_t, kc, vc_t, ks, vs_t, kw, vw_t, gate_t, *, tq=128, ck=512):
    b, s, _ = ks.shape
    nq = s // tq
    pos = np.arange(s)
    kfeat = np.zeros((s, LANES), np.float32)
    kfeat[:, 0] = POS_SPLIT * (pos // POS_SPLIT)
    kfeat[:, 1] = pos % POS_SPLIT
    kfeat[pos, FEAT_ROWS + pos // SEL_BLOCK] = 1.0
    ncmp = kc.shape[2]
    cfeat = np.zeros((ncmp, LANES), np.float32)
    cfeat[:, 0] = CMP_STRIDE * np.arange(ncmp)
    cfeat[:, 1] = 0.5 * (CMP_BLOCK - 1)
    slope = np.zeros((NSA_KV_HEADS, FEAT_ROWS, NSA_GROUP * tq), np.float32)
    for hd in range(NSA_HEADS):
        g, r = divmod(hd, NSA_GROUP)
        slope[g, 0:2, r * tq:(r + 1) * tq] = 2.0 ** (-8.0 * (hd + 1) / NSA_HEADS)
    assert FEAT_ROWS + s // SEL_BLOCK <= LANES and s // SEL_BLOCK % FEAT_ROWS == 0 and s % ck == 0 and ck % tq == 0

    def full(a):
        return pl.BlockSpec(a.shape, lambda i, j: (0,) * a.ndim)

    def per_b(a):
        return pl.BlockSpec((1,) + a.shape[1:], lambda i, j: (i,) + (0,) * (a.ndim - 1))

    def per_b_cols(a):
        return pl.BlockSpec((a.shape[0], s), lambda i, j: (0, i))

    def q_cols(a):
        return pl.BlockSpec((a.shape[0], tq), lambda i, j: (0, i * nq + j))

    consts = [jnp.asarray(kfeat, BF16), jnp.asarray(cfeat, BF16), jnp.asarray(slope, BF16)]
    return pl.pallas_call(
        functools.partial(_nsa_kernel, tq=tq, ck=ck),
        grid=(b, nq),
        in_specs=[q_cols(q_t), per_b(kc), per_b(vc_t), per_b(ks), per_b_cols(vs_t), per_b(kw), per_b_cols(vw_t),
                  q_cols(gate_t)] + [full(c) for c in consts],
        out_specs=pl.BlockSpec((1, tq, NSA_WIDTH), lambda i, j: (i, j, 0)),
        out_shape=jax.ShapeDtypeStruct((b, s, NSA_WIDTH), F32),
        compiler_params=_cparams(("parallel", "arbitrary")),
        name="nsa",
    )(q_t, kc, vc_t, ks, vs_t, kw, vw_t, gate_t, *consts)


def _mla_prep_kernel(cq_ref, ckv_ref, kr_ref, cos_ref, sin_ref, cos_t_ref, sin_t_ref, wq_t_ref, wqr_t_ref, wk_ref,
                     wv_t_ref, qgn_ref, qgr_ref, qgt_ref, kgn_ref, kgr_ref, kgt_ref, q_t_ref, k_ref, v_t_ref):
    cq = cq_ref[0]
    ckv = ckv_ref[0]
    k_rope = kr_ref[0, :, 0:LANES]
    k_rot = kr_ref[0, :, LANES:2 * LANES]
    k_rope_ssq = jnp.sum(k_rope * k_rope, axis=-1, keepdims=True)
    k_rope_emb = k_rope * kgr_ref[...] * cos_ref[...] + k_rot * kgt_ref[...] * sin_ref[...]
    q_cos = qgr_ref[...] * cos_t_ref[...]
    q_sin = qgt_ref[...] * sin_t_ref[...]
    qscale = MLA_QK_DIM ** -0.5
    qx_all = _dot_nt(wq_t_ref[...], cq)
    q_rot_all = _dot_nt(wqr_t_ref[...], cq)
    k_nope_all = _dot(ckv, wk_ref[...])
    v_all = _dot_nt(wv_t_ref[...], ckv)
    for hd in range(MLA_HEADS):
        q_nope = qx_all[hd * MLA_QK_PAD:hd * MLA_QK_PAD + LANES]
        q_rope = qx_all[hd * MLA_QK_PAD + LANES:(hd + 1) * MLA_QK_PAD]
        q_rot = q_rot_all[hd * LANES:(hd + 1) * LANES]
        ssq = jnp.sum(q_nope * q_nope, axis=0, keepdims=True) + jnp.sum(q_rope * q_rope, axis=0, keepdims=True)
        inv = lax.rsqrt(ssq / MLA_QK_DIM + EPS) * qscale
        q_t_ref[0, hd, 0:LANES, :] = (q_nope * inv * qgn_ref[...]).astype(BF16)
        q_t_ref[0, hd, LANES:2 * LANES, :] = ((q_rope * q_cos + q_rot * q_sin) * inv).astype(BF16)

        k_nope = k_nope_all[:, hd * LANES:(hd + 1) * LANES]
        ssq = jnp.sum(k_nope * k_nope, axis=-1, keepdims=True) + k_rope_ssq
        inv = lax.rsqrt(ssq / MLA_QK_DIM + EPS)
        k_ref[0, hd, :, 0:LANES] = (k_nope * inv * kgn_ref[...]).astype(BF16)
        k_ref[0, hd, :, LANES:2 * LANES] = (k_rope_emb * inv).astype(BF16)
        v_t_ref[0, hd] = v_all[hd * LANES:(hd + 1) * LANES].astype(BF16)


def _mla_prep(cq, ckv, kr, tables, weights, q_gains, k_gains, *, tm=256):
    b, s, _ = cq.shape
    cos, sin = tables
    q_gains = tuple(jnp.broadcast_to(g.reshape(LANES, 1), (LANES, tm)) for g in q_gains)

    def full(a):
        return pl.BlockSpec(a.shape, lambda i, j: (0,) * a.ndim)

    def rows(width):
        return pl.BlockSpec((1, tm, width), lambda i, j: (i, j, 0))

    pos_rows = pl.BlockSpec((tm, LANES), lambda i, j: (j, 0))
    pos_cols = pl.BlockSpec((LANES, tm), lambda i, j: (0, j))
    return pl.pallas_call(
        _mla_prep_kernel,
        grid=(b, s // tm),
        in_specs=[rows(MLA_Q_RANK), rows(MLA_KV_RANK), rows(2 * LANES), pos_rows, pos_rows, pos_cols, pos_cols]
                 + [full(w) for w in weights] + [full(g) for g in q_gains + k_gains],
        out_specs=[pl.BlockSpec((1, MLA_HEADS, MLA_QK_PAD, tm), lambda i, j: (i, 0, 0, j)),
                   pl.BlockSpec((1, MLA_HEADS, tm, MLA_QK_PAD), lambda i, j: (i, 0, j, 0)),
                   pl.BlockSpec((1, MLA_HEADS, MLA_V_DIM, tm), lambda i, j: (i, 0, 0, j))],
        out_shape=[jax.ShapeDtypeStruct((b, MLA_HEADS, MLA_QK_PAD, s), BF16),
                   jax.ShapeDtypeStruct((b, MLA_HEADS, s, MLA_QK_PAD), BF16),
                   jax.ShapeDtypeStruct((b, MLA_HEADS, MLA_V_DIM, s), BF16)],
        compiler_params=_cparams(("parallel", "parallel")),
        name="mla_prep",
    )(cq, ckv, kr, cos, sin, cos.T, sin.T, *weights, *q_gains, *k_gains)


def _mla_attn_kernel(q_t_ref, k_ref, v_t_ref, o_ref, *, tq, ck, hb):
    q0 = pl.program_id(2) * tq
    n_full = q0 // ck
    rel = lax.broadcasted_iota(jnp.int32, (ck, tq), 0) - lax.broadcasted_iota(jnp.int32, (ck, tq), 1)
    diag_mask = rel <= q0 - n_full * ck

    def scores(h, c0):
        return _dot(k_ref[0, h, pl.ds(c0, ck), :], q_t_ref[0, h])

    c_diag = pl.multiple_of(n_full * ck, ck)
    ss = [scores(h, c_diag) for h in range(hb)]
    stats = [_softmax_cols(jnp.where(diag_mask, ss[h], NEG)) for h in range(hb)]
    carry = []
    for h in range(hb):
        m, l, p = stats[h]
        carry += [m, l, _dot(v_t_ref[0, h, :, pl.ds(c_diag, ck)], p.astype(BF16))]

    def body(c, carry):
        c0 = pl.multiple_of(c * ck, ck)
        ss = [scores(h, c0) for h in range(hb)]
        staged = []
        for h in range(hb):
            m, l, acc = carry[3 * h:3 * h + 3]
            m_new = jnp.maximum(m, jnp.max(ss[h], axis=0, keepdims=True))
            p = jnp.exp(ss[h] - m_new)
            alpha = jnp.exp(m - m_new)
            staged.append((m_new, alpha * l + jnp.sum(p, axis=0, keepdims=True), alpha * acc, p.astype(BF16)))
        out = []
        for h in range(hb):
            m_new, l_new, acc_scaled, p = staged[h]
            out += [m_new, l_new, acc_scaled + _dot(v_t_ref[0, h, :, pl.ds(c0, ck)], p)]
        return tuple(out)

    carry = lax.fori_loop(0, n_full, body, tuple(carry))
    d_v = v_t_ref.shape[2]
    for h in range(hb):
        o_ref[0, :, h * d_v:(h + 1) * d_v] = (carry[3 * h + 2] / carry[3 * h + 1]).T


def _mla_attn(q_t, k, v_t, *, tq=256, ck=512, hb=4):
    b, h, s, dq = k.shape
    dv = v_t.shape[2]
    return pl.pallas_call(
        functools.partial(_mla_attn_kernel, tq=tq, ck=ck, hb=hb),
        grid=(b, h // hb, s // tq),
        in_specs=[pl.BlockSpec((1, hb, dq, tq), lambda i, j, l: (i, j, 0, l)),
                  pl.BlockSpec((1, hb, s, dq), lambda i, j, l: (i, j, 0, 0)),
                  pl.BlockSpec((1, hb, dv, s), lambda i, j, l: (i, j, 0, 0))],
        out_specs=pl.BlockSpec((1, tq, hb * dv), lambda i, j, l: (i, l, j)),
        out_shape=jax.ShapeDtypeStruct((b, s, h * dv), F32),
        compiler_params=_cparams(("parallel", "parallel", "arbitrary")),
        name="mla_attn",
    )(q_t, k, v_t)


def _out_kernel(x_ref, oa_ref, ob_ref, ga_ref, gb_ref, w_ref, o_ref):
    oa = _rms(oa_ref[...], ga_ref[...]).astype(BF16)
    ob = _rms(ob_ref[...], gb_ref[...]).astype(BF16)
    o_ref[...] = x_ref[...] + _dot(jnp.concatenate([oa, ob], axis=1), w_ref[...])


def _out_proj(x, oa, ob, gain_a, gain_b, w, *, tm=256):
    t, d = x.shape

    def rows(width):
        return pl.BlockSpec((tm, width), lambda i: (i, 0))

    def full(a):
        return pl.BlockSpec(a.shape, lambda i: (0,) * a.ndim)

    return pl.pallas_call(
        _out_kernel,
        grid=(t // tm,),
        in_specs=[rows(d), rows(oa.shape[1]), rows(ob.shape[1]), full(gain_a), full(gain_b), full(w)],
        out_specs=rows(d),
        out_shape=jax.ShapeDtypeStruct((t, d), F32),
        compiler_params=_cparams(("parallel",)),
        name="out_proj",
    )(x, oa, ob, gain_a, gain_b, w)


def _rot_cols(w):
    half = w.shape[1] // 2
    return jnp.concatenate([-w[:, half:], w[:, :half]], axis=1)


def _layout_w_in(w_in):
    d = w_in.shape[0]
    off = np.cumsum([0, NSA_WIDTH] + [NSA_KV_WIDTH] * 6 + [NSA_HEADS * 3, MLA_Q_RANK, MLA_KV_RANK, MLA_ROPE_DIM])
    q, kc, vc, ks, vs, kw, vw, gt, cq, ckv, kr = [w_in[:, off[i]:off[i + 1]] for i in range(11)]
    zr = jnp.zeros((d, LANES - MLA_ROPE_DIM), w_in.dtype)
    zg = jnp.zeros((d, LANES - NSA_HEADS * 3), w_in.dtype)
    w_n = jnp.concatenate([kc, vc, ks, kw, cq, ckv, kr, zr, _rot_cols(kr), zr], axis=1).astype(BF16)
    w_t = jnp.concatenate([q, vs, vw, gt, zg], axis=1).astype(BF16).T
    return w_n, w_t


def _layout_mla_weights(w_uq, w_ukv):
    r = w_uq.shape[0]
    zr = jnp.zeros((r, LANES - MLA_ROPE_DIM), w_uq.dtype)
    main, rot, k_cols, v_cols = [], [], [], []
    for hd in range(MLA_HEADS):
        base = hd * MLA_QK_DIM
        rope = w_uq[:, base + MLA_NOPE_DIM:base + MLA_QK_DIM]
        main += [w_uq[:, base:base + MLA_NOPE_DIM], rope, zr]
        rot += [_rot_cols(rope), zr]
        kv = hd * (MLA_NOPE_DIM + MLA_V_DIM)
        k_cols.append(w_ukv[:, kv:kv + MLA_NOPE_DIM])
        v_cols.append(w_ukv[:, kv + MLA_NOPE_DIM:kv + MLA_NOPE_DIM + MLA_V_DIM])
    cat = lambda parts: jnp.concatenate(parts, axis=1).astype(BF16)
    return cat(main).T, cat(rot).T, cat(k_cols), cat(v_cols).T


def _rope_gains(gain):
    half = MLA_ROPE_DIM // 2
    zr = jnp.zeros((LANES - MLA_ROPE_DIM,), gain.dtype)
    g_rope = gain[MLA_NOPE_DIM:]
    return (gain[None, :MLA_NOPE_DIM], jnp.concatenate([g_rope, zr])[None, :],
            jnp.concatenate([g_rope[half:], g_rope[:half], zr])[None, :])


def _rope_tables(s):
    half = MLA_ROPE_DIM // 2
    inv = ROPE_THETA ** (-jnp.arange(half, dtype=F32) / half)
    ang = jnp.arange(s).astype(F32)[:, None] * inv[None, :]
    zr = jnp.zeros((s, LANES - MLA_ROPE_DIM), F32)
    cos, sin = jnp.cos(ang), jnp.sin(ang)
    return jnp.concatenate([cos, cos, zr], axis=1), jnp.concatenate([sin, sin, zr], axis=1)


def kernel(x, ffn1_norm, ffn1_w_gate, ffn1_w_up, ffn1_w_down, mix_norm, w_in, nsa_q_norm, nsa_k_norm, nsa_cmp_pos_k, nsa_cmp_w1_k, nsa_cmp_w2_k, nsa_cmp_pos_v, nsa_cmp_w1_v, nsa_cmp_w2_v, mla_q_a_norm, mla_w_uq, mla_kv_a_norm, mla_w_ukv, mla_q_norm, mla_k_norm, out_norm_nsa, out_norm_mla, w_out, ffn2_norm, ffn2_w_gate, ffn2_w_up, ffn2_w_down):
    b, s, d = x.shape
    depth = ffn1_norm.shape[0]
    tables = _rope_tables(s)
    xt = x.reshape(b * s, d)
    for l in range(depth):
        xt = _ffn(xt, ffn1_norm[l][None, :], ffn1_w_gate[l], ffn1_w_up[l], ffn1_w_down[l])

        w_n, w_t = _layout_w_in(w_in[l])
        q_t, kv_cmp, ks, vs_t, kw, vw_t, gate_t, cq, ckv, kr = _proj(
            xt, mix_norm[l][None, :], w_n, w_t, nsa_q_norm[l], nsa_k_norm[l],
            mla_q_a_norm[l][None, :], mla_kv_a_norm[l][None, :])

        kc, vc_t = _compress(
            kv_cmp.reshape(b, s // CMP_STRIDE, CMP_STRIDE * 2 * NSA_KV_WIDTH),
            nsa_cmp_w1_k[l].astype(BF16), nsa_cmp_w2_k[l].astype(BF16), nsa_cmp_pos_k[l].reshape(1, -1),
            nsa_cmp_w1_v[l].astype(BF16), nsa_cmp_w2_v[l].astype(BF16).T, nsa_cmp_pos_v[l].reshape(1, -1),
            nsa_k_norm[l])

        def bs(a):
            return a.reshape(b, s, a.shape[-1])

        o_a = _nsa(q_t, kc, vc_t, bs(ks), vs_t, bs(kw), vw_t, gate_t)

        q_m, k_m, v_m = _mla_prep(
            bs(cq), bs(ckv), bs(kr), tables, _layout_mla_weights(mla_w_uq[l], mla_w_ukv[l]),
            _rope_gains(mla_q_norm[l]), _rope_gains(mla_k_norm[l]))
        o_b = _mla_attn(q_m, k_m, v_m)

        xt = _out_proj(xt, o_a.reshape(b * s, NSA_WIDTH), o_b.reshape(b * s, MLA_WIDTH),
                       out_norm_nsa[l][None, :], out_norm_mla[l][None, :], w_out[l].astype(BF16))

        xt = _ffn(xt, ffn2_norm[l][None, :], ffn2_w_gate[l], ffn2_w_up[l], ffn2_w_down[l])
    return xt.reshape(b, s, d)
```

```python
import functools

import numpy as np
import jax
import jax.numpy as jnp
from jax import lax
from jax.experimental import pallas as pl
from jax.experimental.pallas import tpu as pltpu

F32 = jnp.float32
BF16 = jnp.bfloat16

HEAD_DIM = 128
NSA_HEADS = 8
NSA_KV_HEADS = 2
NSA_GROUP = NSA_HEADS // NSA_KV_HEADS
CMP_BLOCK = 32
CMP_STRIDE = 16
CMP_HIDDEN = 256
SEL_BLOCK = 64
SEL_SHIFT = 6
SEL_TOP = 8
WINDOW = 512
MLA_HEADS = 8
MLA_Q_RANK = 384
MLA_KV_RANK = 256
MLA_NOPE_DIM = 128
MLA_ROPE_DIM = 64
MLA_V_DIM = 128
MLA_QK_DIM = MLA_NOPE_DIM + MLA_ROPE_DIM
ROPE_THETA = 10000.0
EPS = 1e-6
NEG = -1e30
FORCE_SCORE = 1e4
NSA_WIDTH = NSA_HEADS * HEAD_DIM
MLA_WIDTH = MLA_HEADS * MLA_V_DIM
NSA_KV_WIDTH = NSA_KV_HEADS * HEAD_DIM

LANES = 128
MLA_QK_PAD = 2 * LANES
VMEM_LIMIT = 56 * 1024 * 1024
POS_SPLIT = 16
FEAT_ROWS = 16


def _cparams(sem):
    return pltpu.CompilerParams(dimension_semantics=sem, vmem_limit_bytes=VMEM_LIMIT)


def _rms(x, gain):
    return x * lax.rsqrt(jnp.mean(x * x, axis=-1, keepdims=True) + EPS) * gain


def _dot(a, b):
    return jnp.dot(a, b, preferred_element_type=F32)


def _dot_nt(a, b):
    return lax.dot_general(a, b, (((1,), (1,)), ((), ())), preferred_element_type=F32)


def _sigmoid(x):
    return 1.0 / (1.0 + jnp.exp(-x))


def _ffn_kernel(x_ref, g_ref, wg_ref, wu_ref, wd_ref, *rest, first_tile):
    if first_tile:
        o_ref, wg_out, wu_out, wd_out, h_ref = rest
    else:
        _, o_ref, h_ref = rest

    @pl.when(pl.program_id(1) == 0)
    def _():
        x = x_ref[...]
        h_ref[...] = _rms(x, g_ref[...]).astype(BF16)
        o_ref[...] = x

    wg, wu, wd = wg_ref[...].astype(BF16), wu_ref[...].astype(BF16), wd_ref[...].astype(BF16)
    if first_tile:
        wg_out[...], wu_out[...], wd_out[...] = wg, wu, wd
    h = h_ref[...]
    gate = _dot(h, wg)
    up = _dot(h, wu)
    act = gate * _sigmoid(gate) * up * 0.5
    o_ref[...] += _dot(act.astype(BF16), wd)


def _ffn(x, gain, wg, wu, wd, *, tm=1024, tf_first=256, tf=512):
    t, d = x.shape
    f = wg.shape[1]

    def specs(tfx, row0, x_mode=None):
        return [pl.BlockSpec((tm, d), lambda i, j: (i + row0, 0), pipeline_mode=x_mode),
                pl.BlockSpec((1, d), lambda i, j: (0, 0)),
                pl.BlockSpec((d, tfx), lambda i, j: (0, j)),
                pl.BlockSpec((d, tfx), lambda i, j: (0, j)),
                pl.BlockSpec((tfx, d), lambda i, j: (j, 0))]

    y, wg_b, wu_b, wd_b = pl.pallas_call(
        functools.partial(_ffn_kernel, first_tile=True),
        grid=(1, f // tf_first),
        in_specs=specs(tf_first, 0, pl.Buffered(1)),
        out_specs=[pl.BlockSpec((tm, d), lambda i, j: (0, 0))] + specs(tf_first, 0)[2:],
        out_shape=[jax.ShapeDtypeStruct((t, d), F32), jax.ShapeDtypeStruct((d, f), BF16),
                   jax.ShapeDtypeStruct((d, f), BF16), jax.ShapeDtypeStruct((f, d), BF16)],
        scratch_shapes=[pltpu.VMEM((tm, d), BF16)],
        compiler_params=_cparams(("arbitrary", "arbitrary")),
        name="ffn_first",
    )(x, gain, wg, wu, wd)
    return pl.pallas_call(
        functools.partial(_ffn_kernel, first_tile=False),
        grid=(t // tm - 1, f // tf),
        in_specs=specs(tf, 1) + [pl.BlockSpec(memory_space=pl.ANY)],
        out_specs=pl.BlockSpec((tm, d), lambda i, j: (i + 1, 0)),
        out_shape=jax.ShapeDtypeStruct((t, d), F32),
        scratch_shapes=[pltpu.VMEM((tm, d), BF16)],
        input_output_aliases={5: 0},
        compiler_params=_cparams(("parallel", "arbitrary")),
        name="ffn_rest",
    )(x, gain, wg_b, wu_b, wd_b, y)


_N_CMP = 0
_N_KS = _N_CMP + 2 * NSA_KV_WIDTH
_N_KW = _N_KS + NSA_KV_WIDTH
_N_CQ = _N_KW + NSA_KV_WIDTH
_N_CKV = _N_CQ + MLA_Q_RANK
_N_KR = _N_CKV + MLA_KV_RANK
_N_END = _N_KR + 2 * LANES
_T_Q = 0
_T_VS = _T_Q + NSA_WIDTH
_T_VW = _T_VS + NSA_KV_WIDTH
_T_GATE = _T_VW + NSA_KV_WIDTH
_T_END = _T_GATE + LANES


def _proj_kernel(x_ref, g_ref, wn_ref, wt_ref, qg_ref, kg_ref, qag_ref, kvag_ref,
                 q_t_ref, cmp_ref, ks_ref, vs_t_ref, kw_ref, vw_t_ref, gate_t_ref, cq_ref, ckv_ref, kr_ref):
    h = _rms(x_ref[...], g_ref[...]).astype(BF16)
    p_n = _dot(h, wn_ref[...])
    p_t = _dot_nt(wt_ref[...], h)

    def proj(lo, hi):
        return p_n[:, lo:hi]

    def proj_t(lo, hi):
        return p_t[lo:hi, :]

    qscale = HEAD_DIM ** -0.5
    for hd in range(NSA_HEADS):
        q = proj_t(_T_Q + hd * HEAD_DIM, _T_Q + (hd + 1) * HEAD_DIM)
        inv = lax.rsqrt(jnp.mean(q * q, axis=0, keepdims=True) + EPS) * qscale
        q_t_ref[hd * HEAD_DIM:(hd + 1) * HEAD_DIM, :] = (q * inv * qg_ref[...]).astype(BF16)
    cmp_ref[...] = proj(_N_CMP, _N_KS).astype(BF16)
    for g in range(NSA_KV_HEADS):
        sl = slice(g * HEAD_DIM, (g + 1) * HEAD_DIM)
        ks_ref[:, sl] = _rms(proj(_N_KS + g * HEAD_DIM, _N_KS + (g + 1) * HEAD_DIM), kg_ref[1:2, :]).astype(BF16)
        kw_ref[:, sl] = _rms(proj(_N_KW + g * HEAD_DIM, _N_KW + (g + 1) * HEAD_DIM), kg_ref[2:3, :]).astype(BF16)
    vs_t_ref[...] = proj_t(_T_VS, _T_VW).astype(BF16)
    vw_t_ref[...] = proj_t(_T_VW, _T_GATE).astype(BF16)
    cq_ref[...] = _rms(proj(_N_CQ, _N_CKV), qag_ref[...]).astype(BF16)
    ckv_ref[...] = _rms(proj(_N_CKV, _N_KR), kvag_ref[...]).astype(BF16)
    kr_ref[...] = proj(_N_KR, _N_END)
    gate_t_ref[...] = _sigmoid(proj_t(_T_GATE, _T_END))


def _proj(x, gain, w_n, w_t, q_gain, k_gains, qa_gain, kva_gain, *, tm=512):
    t, d = x.shape
    q_gain = jnp.broadcast_to(q_gain.reshape(HEAD_DIM, 1), (HEAD_DIM, tm))

    def full(a):
        return pl.BlockSpec(a.shape, lambda i: (0,) * a.ndim)

    def rows(width):
        return pl.BlockSpec((tm, width), lambda i: (i, 0))

    def cols(height):
        return pl.BlockSpec((height, tm), lambda i: (0, i))

    outs = [(cols, NSA_WIDTH, BF16), (rows, 2 * NSA_KV_WIDTH, BF16), (rows, NSA_KV_WIDTH, BF16),
            (cols, NSA_KV_WIDTH, BF16), (rows, NSA_KV_WIDTH, BF16), (cols, NSA_KV_WIDTH, BF16),
            (cols, LANES, F32), (rows, MLA_Q_RANK, BF16), (rows, MLA_KV_RANK, BF16), (rows, 2 * LANES, F32)]
    return pl.pallas_call(
        _proj_kernel,
        grid=(t // tm,),
        in_specs=[rows(d), full(gain), full(w_n), full(w_t), full(q_gain), full(k_gains), full(qa_gain),
                  full(kva_gain)],
        out_specs=[kind(n) for kind, n, _ in outs],
        out_shape=[jax.ShapeDtypeStruct((t, n) if kind is rows else (n, t), dt) for kind, n, dt in outs],
        compiler_params=_cparams(("parallel",)),
        name="proj",
    )(x, gain, w_n, w_t, q_gain, k_gains, qa_gain, kva_gain)


def _gelu_tanh(x):
    return 0.5 * x * (1.0 + jnp.tanh(np.sqrt(2.0 / np.pi) * (x + 0.044715 * (x * x * x))))


def _compress_kernel(x_ref, w1k_ref, w2k_ref, pk_ref, w1v_ref, w2v_ref, pv_ref, kg_ref, kc_ref, vc_t_ref):
    half = CMP_STRIDE * HEAD_DIM
    row_w = 2 * NSA_KV_WIDTH
    for which, (w1_ref, w2_ref, p_ref) in enumerate(((w1k_ref, w2k_ref, pk_ref), (w1v_ref, w2v_ref, pv_ref))):
        pos_bias = _dot(jnp.broadcast_to(p_ref[...], (8, 2 * half)).astype(BF16), w1_ref[...])[0:1, :]
        for g in range(NSA_KV_HEADS):
            col = which * NSA_KV_WIDTH + g * HEAD_DIM
            xg = jnp.concatenate(
                [x_ref[0, :, l * row_w + col:l * row_w + col + HEAD_DIM] for l in range(CMP_STRIDE)], axis=1)
            first = _dot(xg, w1_ref[0:half, :])
            second = _dot(xg, w1_ref[half:2 * half, :])
            nrow = second.shape[0]
            pre = first + pltpu.roll(second, nrow - 1, 0) + pos_bias
            hidden = _gelu_tanh(pre).astype(BF16)
            if which == 0:
                kc_ref[0, g] = _rms(_dot(hidden, w2_ref[...]), kg_ref[0:1, :]).astype(BF16)
            else:
                vc_t_ref[0, g] = _dot_nt(w2_ref[...], hidden).astype(BF16)


def _compress(xc, w1k, w2k, pk, w1v, w2v, pv, k_gains):
    b, nrow, width = xc.shape

    def full(a):
        return pl.BlockSpec(a.shape, lambda i: (0,) * a.ndim)

    out_sds = jax.ShapeDtypeStruct((b, NSA_KV_HEADS, nrow, HEAD_DIM), BF16)
    out_spec = pl.BlockSpec((1, NSA_KV_HEADS, nrow, HEAD_DIM), lambda i: (i, 0, 0, 0))
    return pl.pallas_call(
        _compress_kernel,
        grid=(b,),
        in_specs=[pl.BlockSpec((1, nrow, width), lambda i: (i, 0, 0)),
                  full(w1k), full(w2k), full(pk), full(w1v), full(w2v), full(pv), full(k_gains)],
        out_specs=[out_spec, out_spec],
        out_shape=[out_sds, out_sds],
        compiler_params=_cparams(("parallel",)),
        name="compress",
    )(xc, w1k, w2k, pk, w1v, w2v, pv, k_gains)


def _softmax_cols(s):
    m = jnp.max(s, axis=0, keepdims=True)
    p = jnp.exp(s - m)
    return m, jnp.sum(p, axis=0, keepdims=True), p


def _nsa_kernel(q_t_ref, kc_ref, vc_t_ref, ks_ref, vs_t_ref, kw_ref, vw_t_ref, gate_t_ref, kfeat_ref, cfeat_ref,
                slope_ref, o_ref, *, tq, ck):
    qi = pl.program_id(1)
    q0 = qi * tq
    ng = NSA_KV_HEADS
    wide = NSA_GROUP * tq
    n_sel = ks_ref.shape[1] // SEL_BLOCK
    ncmp = kc_ref.shape[2]
    n_win = WINDOW + tq

    def tile4(a):
        return jnp.concatenate([a] * NSA_GROUP, axis=1)

    def masked(s, mask, fill=NEG):
        return jnp.concatenate([jnp.where(mask, s[:, r * tq:(r + 1) * tq], fill) for r in range(NSA_GROUP)], axis=1)

    def q_aug(g, extra):
        q = jnp.concatenate([q_t_ref[(g * NSA_GROUP + r) * HEAD_DIM:(g * NSA_GROUP + r + 1) * HEAD_DIM, :]
                             for r in range(NSA_GROUP)], axis=1)
        rows = [q, slope_ref[g]]
        used = slope_ref.shape[1]
        if extra is not None:
            rows.append(extra)
            used += extra.shape[0]
        rows.append(jnp.zeros((HEAD_DIM - used, wide), BF16))
        return jnp.concatenate(rows, axis=0)

    def gsl(g):
        return slice(g * HEAD_DIM, (g + 1) * HEAD_DIM)

    lane_q = q0 + lax.broadcasted_iota(jnp.int32, (1, tq), 1)

    qa_plain = [q_aug(g, None) for g in range(ng)]
    w0 = pl.multiple_of(jnp.maximum(q0 - WINDOW, 0), tq)
    s_cmp = [_dot(jnp.concatenate([kc_ref[0, g], cfeat_ref[...]], axis=1), qa_plain[g]) for g in range(ng)]
    s_win = [_dot(jnp.concatenate([kw_ref[0, pl.ds(w0, n_win), gsl(g)], kfeat_ref[pl.ds(w0, n_win), :]], axis=1),
                  qa_plain[g]) for g in range(ng)]

    n_row = lax.broadcasted_iota(jnp.int32, (ncmp, tq), 0)
    mask_c = n_row * CMP_STRIDE + (CMP_BLOCK - 1) <= lane_q
    s_row = lax.broadcasted_iota(jnp.int32, (n_sel, ncmp), 0)
    n_col = lax.broadcasted_iota(jnp.int32, (n_sel, ncmp), 1)
    overlap_t = jnp.where((n_col * CMP_STRIDE < (s_row + 1) * SEL_BLOCK)
                          & (n_col * CMP_STRIDE + (CMP_BLOCK - 1) >= s_row * SEL_BLOCK), 1.0, 0.0).astype(BF16)
    p_cmp, imp = [], []
    for g in range(ng):
        s = masked(s_cmp[g], mask_c)
        m = jnp.max(s, axis=0, keepdims=True)
        p = masked(jnp.exp(s - m), mask_c, 0.0)
        l = jnp.sum(p, axis=0, keepdims=True)
        p = p / jnp.where(l > 0.0, l, 1.0)
        p_cmp.append(p.astype(BF16))
        p_sum = p[:, 0:tq]
        for r in range(1, NSA_GROUP):
            p_sum = p_sum + p[:, r * tq:(r + 1) * tq]
        p_hi = p_sum.astype(BF16)
        p_lo = (p_sum - p_hi.astype(F32)).astype(BF16)
        imp.append(_dot(overlap_t, p_hi) + _dot(overlap_t, p_lo))
    o_cmp = [_dot(vc_t_ref[0, g], p_cmp[g]) for g in range(ng)]

    blk = lax.broadcasted_iota(jnp.int32, (n_sel, tq), 0)
    blk_f = blk.astype(F32)
    forced = (blk == 0) | (blk == jnp.right_shift(lane_q, SEL_SHIFT))
    future = blk * SEL_BLOCK > lane_q
    penalty = []
    for g in range(ng):
        v = jnp.where(future, NEG, jnp.where(forced, FORCE_SCORE, imp[g]))
        unselected = jnp.full((n_sel, tq), NEG, F32)
        for _ in range(min(SEL_TOP, n_sel)):
            mx = jnp.max(v, axis=0, keepdims=True)
            first = jnp.min(jnp.where(v == mx, blk_f, float(n_sel)), axis=0, keepdims=True)
            pick = blk_f == first
            unselected = jnp.where(pick, 0.0, unselected)
            v = jnp.where(pick, -jnp.inf, v)
        penalty.append(tile4(unselected.astype(BF16)))

    rel = lax.broadcasted_iota(jnp.int32, (n_win, tq), 0) - lax.broadcasted_iota(jnp.int32, (n_win, tq), 1)
    off = q0 - w0
    mask_w = (rel <= off) & (rel > off - WINDOW)
    o_win = []
    p_win = []
    for g in range(ng):
        _, l, p = _softmax_cols(masked(s_win[g], mask_w))
        p_win.append((l, p.astype(BF16)))
    for g in range(ng):
        l, p = p_win[g]
        o_win.append(_dot(vw_t_ref[gsl(g), pl.ds(w0, n_win)], p) / l)

    qa_sel = [q_aug(g, penalty[g]) for g in range(ng)]
    n_full = q0 // ck
    c_diag = pl.multiple_of(n_full * ck, ck)
    rel_d = lax.broadcasted_iota(jnp.int32, (ck, tq), 0) - lax.broadcasted_iota(jnp.int32, (ck, tq), 1)
    mask_d = rel_d <= q0 - c_diag

    def sel_scores(g, c0):
        return _dot(jnp.concatenate([ks_ref[0, pl.ds(c0, ck), gsl(g)], kfeat_ref[pl.ds(c0, ck), :]], axis=1),
                    qa_sel[g])

    s_d = [sel_scores(g, c_diag) for g in range(ng)]
    stats = [_softmax_cols(masked(s_d[g], mask_d)) for g in range(ng)]
    carry = []
    for g in range(ng):
        m, l, p = stats[g]
        carry += [m, l, _dot(vs_t_ref[gsl(g), pl.ds(c_diag, ck)], p.astype(BF16))]

    def body(c, carry):
        c0 = pl.multiple_of(c * ck, ck)
        ss = [sel_scores(g, c0) for g in range(ng)]
        staged = []
        for g in range(ng):
            m, l, acc = carry[3 * g:3 * g + 3]
            m_new = jnp.maximum(m, jnp.max(ss[g], axis=0, keepdims=True))
            p = jnp.exp(ss[g] - m_new)
            alpha = jnp.exp(m - m_new)
            staged.append((m_new, alpha * l + jnp.sum(p, axis=0, keepdims=True), alpha * acc, p.astype(BF16)))
        out = []
        for g in range(ng):
            m_new, l_new, acc_scaled, p = staged[g]
            out += [m_new, l_new, acc_scaled + _dot(vs_t_ref[gsl(g), pl.ds(c0, ck)], p)]
        return tuple(out)

    carry = lax.fori_loop(0, n_full, body, tuple(carry))

    for g in range(ng):
        o_sel = carry[3 * g + 2] / carry[3 * g + 1]
        for r in range(NSA_GROUP):
            hd = g * NSA_GROUP + r
            cs = slice(r * tq, (r + 1) * tq)
            mixed = (gate_t_ref[3 * hd:3 * hd + 1, :] * o_cmp[g][:, cs]
                     + gate_t_ref[3 * hd + 1:3 * hd + 2, :] * o_sel[:, cs]
                     + gate_t_ref[3 * hd + 2:3 * hd + 3, :] * o_win[g][:, cs])
            o_ref[0, :, hd * HEAD_DIM:(hd + 1) * HEAD_DIM] = mixed.T


def _nsa(q_t, kc, vc_t, ks, vs_t, kw, vw_t, gate_t, *, tq=128, ck=512):
    b, s, _ = ks.shape
    nq = s // tq
    pos = np.arange(s)
    kfeat = np.zeros((s, LANES), np.float32)
    kfeat[:, 0] = POS_SPLIT * (pos // POS_SPLIT)
    kfeat[:, 1] = pos % POS_SPLIT
    kfeat[pos, FEAT_ROWS + pos // SEL_BLOCK] = 1.0
    ncmp = kc.shape[2]
    cfeat = np.zeros((ncmp, LANES), np.float32)
    cfeat[:, 0] = CMP_STRIDE * np.arange(ncmp)
    cfeat[:, 1] = 0.5 * (CMP_BLOCK - 1)
    slope = np.zeros((NSA_KV_HEADS, FEAT_ROWS, NSA_GROUP * tq), np.float32)
    for hd in range(NSA_HEADS):
        g, r = divmod(hd, NSA_GROUP)
        slope[g, 0:2, r * tq:(r + 1) * tq] = 2.0 ** (-8.0 * (hd + 1) / NSA_HEADS)
    assert FEAT_ROWS + s // SEL_BLOCK <= LANES and s // SEL_BLOCK % FEAT_ROWS == 0 and s % ck == 0 and ck % tq == 0

    def full(a):
        return pl.BlockSpec(a.shape, lambda i, j: (0,) * a.ndim)

    def per_b(a):
        return pl.BlockSpec((1,) + a.shape[1:], lambda i, j: (i,) + (0,) * (a.ndim - 1))

    def per_b_cols(a):
        return pl.BlockSpec((a.shape[0], s), lambda i, j: (0, i))

    def q_cols(a):
        return pl.BlockSpec((a.shape[0], tq), lambda i, j: (0, i * nq + j))

    consts = [jnp.asarray(kfeat, BF16), jnp.asarray(cfeat, BF16), jnp.asarray(slope, BF16)]
    return pl.pallas_call(
        functools.partial(_nsa_kernel, tq=tq, ck=ck),
        grid=(b, nq),
        in_specs=[q_cols(q_t), per_b(kc), per_b(vc_t), per_b(ks), per_b_cols(vs_t), per_b(kw), per_b_cols(vw_t),
                  q_cols(gate_t)] + [full(c) for c in consts],
        out_specs=pl.BlockSpec((1, tq, NSA_WIDTH), lambda i, j: (i, j, 0)),
        out_shape=jax.ShapeDtypeStruct((b, s, NSA_WIDTH), F32),
        compiler_params=_cparams(("parallel", "arbitrary")),
        name="nsa",
    )(q_t, kc, vc_t, ks, vs_t, kw, vw_t, gate_t, *consts)


def _mla_prep_kernel(cq_ref, ckv_ref, kr_ref, cos_ref, sin_ref, cos_t_ref, sin_t_ref, wq_t_ref, wqr_t_ref, wk_ref,
                     wv_t_ref, qgn_ref, qgr_ref, qgt_ref, kgn_ref, kgr_ref, kgt_ref, q_t_ref, k_ref, v_t_ref):
    cq = cq_ref[0]
    ckv = ckv_ref[0]
    k_rope = kr_ref[0, :, 0:LANES]
    k_rot = kr_ref[0, :, LANES:2 * LANES]
    k_rope_ssq = jnp.sum(k_rope * k_rope, axis=-1, keepdims=True)
    k_rope_emb = k_rope * kgr_ref[...] * cos_ref[...] + k_rot * kgt_ref[...] * sin_ref[...]
    q_cos = qgr_ref[...] * cos_t_ref[...]
    q_sin = qgt_ref[...] * sin_t_ref[...]
    qscale = MLA_QK_DIM ** -0.5
    qx_all = _dot_nt(wq_t_ref[...], cq)
    q_rot_all = _dot_nt(wqr_t_ref[...], cq)
    k_nope_all = _dot(ckv, wk_ref[...])
    v_all = _dot_nt(wv_t_ref[...], ckv)
    for hd in range(MLA_HEADS):
        q_nope = qx_all[hd * MLA_QK_PAD:hd * MLA_QK_PAD + LANES]
        q_rope = qx_all[hd * MLA_QK_PAD + LANES:(hd + 1) * MLA_QK_PAD]
        q_rot = q_rot_all[hd * LANES:(hd + 1) * LANES]
        ssq = jnp.sum(q_nope * q_nope, axis=0, keepdims=True) + jnp.sum(q_rope * q_rope, axis=0, keepdims=True)
        inv = lax.rsqrt(ssq / MLA_QK_DIM + EPS) * qscale
        q_t_ref[0, hd, 0:LANES, :] = (q_nope * inv * qgn_ref[...]).astype(BF16)
        q_t_ref[0, hd, LANES:2 * LANES, :] = ((q_rope * q_cos + q_rot * q_sin) * inv).astype(BF16)

        k_nope = k_nope_all[:, hd * LANES:(hd + 1) * LANES]
        ssq = jnp.sum(k_nope * k_nope, axis=-1, keepdims=True) + k_rope_ssq
        inv = lax.rsqrt(ssq / MLA_QK_DIM + EPS)
        k_ref[0, hd, :, 0:LANES] = (k_nope * inv * kgn_ref[...]).astype(BF16)
        k_ref[0, hd, :, LANES:2 * LANES] = (k_rope_emb * inv).astype(BF16)
        v_t_ref[0, hd] = v_all[hd * LANES:(hd + 1) * LANES].astype(BF16)


def _mla_prep(cq, ckv, kr, tables, weights, q_gains, k_gains, *, tm=256):
    b, s, _ = cq.shape
    cos, sin = tables
    q_gains = tuple(jnp.broadcast_to(g.reshape(LANES, 1), (LANES, tm)) for g in q_gains)

    def full(a):
        return pl.BlockSpec(a.shape, lambda i, j: (0,) * a.ndim)

    def rows(width):
        return pl.BlockSpec((1, tm, width), lambda i, j: (i, j, 0))

    pos_rows = pl.BlockSpec((tm, LANES), lambda i, j: (j, 0))
    pos_cols = pl.BlockSpec((LANES, tm), lambda i, j: (0, j))
    return pl.pallas_call(
        _mla_prep_kernel,
        grid=(b, s // tm),
        in_specs=[rows(MLA_Q_RANK), rows(MLA_KV_RANK), rows(2 * LANES), pos_rows, pos_rows, pos_cols, pos_cols]
                 + [full(w) for w in weights] + [full(g) for g in q_gains + k_gains],
        out_specs=[pl.BlockSpec((1, MLA_HEADS, MLA_QK_PAD, tm), lambda i, j: (i, 0, 0, j)),
                   pl.BlockSpec((1, MLA_HEADS, tm, MLA_QK_PAD), lambda i, j: (i, 0, j, 0)),
                   pl.BlockSpec((1, MLA_HEADS, MLA_V_DIM, tm), lambda i, j: (i, 0, 0, j))],
        out_shape=[jax.ShapeDtypeStruct((b, MLA_HEADS, MLA_QK_PAD, s), BF16),
                   jax.ShapeDtypeStruct((b, MLA_HEADS, s, MLA_QK_PAD), BF16),
                   jax.ShapeDtypeStruct((b, MLA_HEADS, MLA_V_DIM, s), BF16)],
        compiler_params=_cparams(("parallel", "parallel")),
        name="mla_prep",
    )(cq, ckv, kr, cos, sin, cos.T, sin.T, *weights, *q_gains, *k_gains)


def _mla_attn_kernel(q_t_ref, k_ref, v_t_ref, o_ref, *, tq, ck, hb):
    q0 = pl.program_id(2) * tq
    n_full = q0 // ck
    rel = lax.broadcasted_iota(jnp.int32, (ck, tq), 0) - lax.broadcasted_iota(jnp.int32, (ck, tq), 1)
    diag_mask = rel <= q0 - n_full * ck

    def scores(h, c0):
        return _dot(k_ref[0, h, pl.ds(c0, ck), :], q_t_ref[0, h])

    c_diag = pl.multiple_of(n_full * ck, ck)
    ss = [scores(h, c_diag) for h in range(hb)]
    stats = [_softmax_cols(jnp.where(diag_mask, ss[h], NEG)) for h in range(hb)]
    carry = []
    for h in range(hb):
        m, l, p = stats[h]
        carry += [m, l, _dot(v_t_ref[0, h, :, pl.ds(c_diag, ck)], p.astype(BF16))]

    def body(c, carry):
        c0 = pl.multiple_of(c * ck, ck)
        ss = [scores(h, c0) for h in range(hb)]
        staged = []
        for h in range(hb):
            m, l, acc = carry[3 * h:3 * h + 3]
            m_new = jnp.maximum(m, jnp.max(ss[h], axis=0, keepdims=True))
            p = jnp.exp(ss[h] - m_new)
            alpha = jnp.exp(m - m_new)
            staged.append((m_new, alpha * l + jnp.sum(p, axis=0, keepdims=True), alpha * acc, p.astype(BF16)))
        out = []
        for h in range(hb):
            m_new, l_new, acc_scaled, p = staged[h]
            out += [m_new, l_new, acc_scaled + _dot(v_t_ref[0, h, :, pl.ds(c0, ck)], p)]
        return tuple(out)

    carry = lax.fori_loop(0, n_full, body, tuple(carry))
    d_v = v_t_ref.shape[2]
    for h in range(hb):
        o_ref[0, :, h * d_v:(h + 1) * d_v] = (carry[3 * h + 2] / carry[3 * h + 1]).T


def _mla_attn(q_t, k, v_t, *, tq=256, ck=512, hb=4):
    b, h, s, dq = k.shape
    dv = v_t.shape[2]
    return pl.pallas_call(
        functools.partial(_mla_attn_kernel, tq=tq, ck=ck, hb=hb),
        grid=(b, h // hb, s // tq),
        in_specs=[pl.BlockSpec((1, hb, dq, tq), lambda i, j, l: (i, j, 0, l)),
                  pl.BlockSpec((1, hb, s, dq), lambda i, j, l: (i, j, 0, 0)),
                  pl.BlockSpec((1, hb, dv, s), lambda i, j, l: (i, j, 0, 0))],
        out_specs=pl.BlockSpec((1, tq, hb * dv), lambda i, j, l: (i, l, j)),
        out_shape=jax.ShapeDtypeStruct((b, s, h * dv), F32),
        compiler_params=_cparams(("parallel", "parallel", "arbitrary")),
        name="mla_attn",
    )(q_t, k, v_t)


def _out_kernel(x_ref, oa_ref, ob_ref, ga_ref, gb_ref, w_ref, o_ref):
    oa = _rms(oa_ref[...], ga_ref[...]).astype(BF16)
    ob = _rms(ob_ref[...], gb_ref[...]).astype(BF16)
    o_ref[...] = x_ref[...] + _dot(jnp.concatenate([oa, ob], axis=1), w_ref[...])


def _out_proj(x, oa, ob, gain_a, gain_b, w, *, tm=256):
    t, d = x.shape

    def rows(width):
        return pl.BlockSpec((tm, width), lambda i: (i, 0))

    def full(a):
        return pl.BlockSpec(a.shape, lambda i: (0,) * a.ndim)

    return pl.pallas_call(
        _out_kernel,
        grid=(t // tm,),
        in_specs=[rows(d), rows(oa.shape[1]), rows(ob.shape[1]), full(gain_a), full(gain_b), full(w)],
        out_specs=rows(d),
        out_shape=jax.ShapeDtypeStruct((t, d), F32),
        compiler_params=_cparams(("parallel",)),
        name="out_proj",
    )(x, oa, ob, gain_a, gain_b, w)


def _rot_cols(w):
    half = w.shape[1] // 2
    return jnp.concatenate([-w[:, half:], w[:, :half]], axis=1)


def _layout_w_in(w_in):
    d = w_in.shape[0]
    off = np.cumsum([0, NSA_WIDTH] + [NSA_KV_WIDTH] * 6 + [NSA_HEADS * 3, MLA_Q_RANK, MLA_KV_RANK, MLA_ROPE_DIM])
    q, kc, vc, ks, vs, kw, vw, gt, cq, ckv, kr = [w_in[:, off[i]:off[i + 1]] for i in range(11)]
    zr = jnp.zeros((d, LANES - MLA_ROPE_DIM), w_in.dtype)
    zg = jnp.zeros((d, LANES - NSA_HEADS * 3), w_in.dtype)
    w_n = jnp.concatenate([kc, vc, ks, kw, cq, ckv, kr, zr, _rot_cols(kr), zr], axis=1).astype(BF16)
    w_t = jnp.concatenate([q, vs, vw, gt, zg], axis=1).astype(BF16).T
    return w_n, w_t


def _layout_mla_weights(w_uq, w_ukv):
    r = w_uq.shape[0]
    zr = jnp.zeros((r, LANES - MLA_ROPE_DIM), w_uq.dtype)
    main, rot, k_cols, v_cols = [], [], [], []
    for hd in range(MLA_HEADS):
        base = hd * MLA_QK_DIM
        rope = w_uq[:, base + MLA_NOPE_DIM:base + MLA_QK_DIM]
        main += [w_uq[:, base:base + MLA_NOPE_DIM], rope, zr]
        rot += [_rot_cols(rope), zr]
        kv = hd * (MLA_NOPE_DIM + MLA_V_DIM)
        k_cols.append(w_ukv[:, kv:kv + MLA_NOPE_DIM])
        v_cols.append(w_ukv[:, kv + MLA_NOPE_DIM:kv + MLA_NOPE_DIM + MLA_V_DIM])
    cat = lambda parts: jnp.concatenate(parts, axis=1).astype(BF16)
    return cat(main).T, cat(rot).T, cat(k_cols), cat(v_cols).T


def _rope_gains(gain):
    half = MLA_ROPE_DIM // 2
    zr = jnp.zeros((LANES - MLA_ROPE_DIM,), gain.dtype)
    g_rope = gain[MLA_NOPE_DIM:]
    return (gain[None, :MLA_NOPE_DIM], jnp.concatenate([g_rope, zr])[None, :],
            jnp.concatenate([g_rope[half:], g_rope[:half], zr])[None, :])


def _rope_tables(s):
    half = MLA_ROPE_DIM // 2
    inv = ROPE_THETA ** (-jnp.arange(half, dtype=F32) / half)
    ang = jnp.arange(s).astype(F32)[:, None] * inv[None, :]
    zr = jnp.zeros((s, LANES - MLA_ROPE_DIM), F32)
    cos, sin = jnp.cos(ang), jnp.sin(ang)
    return jnp.concatenate([cos, cos, zr], axis=1), jnp.concatenate([sin, sin, zr], axis=1)


def kernel(x, ffn1_norm, ffn1_w_gate, ffn1_w_up, ffn1_w_down, mix_norm, w_in, nsa_q_norm, nsa_k_norm, nsa_cmp_pos_k, nsa_cmp_w1_k, nsa_cmp_w2_k, nsa_cmp_pos_v, nsa_cmp_w1_v, nsa_cmp_w2_v, mla_q_a_norm, mla_w_uq, mla_kv_a_norm, mla_w_ukv, mla_q_norm, mla_k_norm, out_norm_nsa, out_norm_mla, w_out, ffn2_norm, ffn2_w_gate, ffn2_w_up, ffn2_w_down):
    b, s, d = x.shape
    depth = ffn1_norm.shape[0]
    tables = _rope_tables(s)
    xt = x.reshape(b * s, d)
    for l in range(depth):
        xt = _ffn(xt, ffn1_norm[l][None, :], ffn1_w_gate[l], ffn1_w_up[l], ffn1_w_down[l])

        w_n, w_t = _layout_w_in(w_in[l])
        q_t, kv_cmp, ks, vs_t, kw, vw_t, gate_t, cq, ckv, kr = _proj(
            xt, mix_norm[l][None, :], w_n, w_t, nsa_q_norm[l], nsa_k_norm[l],
            mla_q_a_norm[l][None, :], mla_kv_a_norm[l][None, :])

        kc, vc_t = _compress(
            kv_cmp.reshape(b, s // CMP_STRIDE, CMP_STRIDE * 2 * NSA_KV_WIDTH),
            nsa_cmp_w1_k[l].astype(BF16), nsa_cmp_w2_k[l].astype(BF16), nsa_cmp_pos_k[l].reshape(1, -1),
            nsa_cmp_w1_v[l].astype(BF16), nsa_cmp_w2_v[l].astype(BF16).T, nsa_cmp_pos_v[l].reshape(1, -1),
            nsa_k_norm[l])

        def bs(a):
            return a.reshape(b, s, a.shape[-1])

        o_a = _nsa(q_t, kc, vc_t, bs(ks), vs_t, bs(kw), vw_t, gate_t)

        q_m, k_m, v_m = _mla_prep(
            bs(cq), bs(ckv), bs(kr), tables, _layout_mla_weights(mla_w_uq[l], mla_w_ukv[l]),
            _rope_gains(mla_q_norm[l]), _rope_gains(mla_k_norm[l]))
        o_b = _mla_attn(q_m, k_m, v_m)

        xt = _out_proj(xt, o_a.reshape(b * s, NSA_WIDTH), o_b.reshape(b * s, MLA_WIDTH),
                       out_norm_nsa[l][None, :], out_norm_mla[l][None, :], w_out[l].astype(BF16))

        xt = _ffn(xt, ffn2_norm[l][None, :], ffn2_w_gate[l], ffn2_w_up[l], ffn2_w_down[l])
    return xt.reshape(b, s, d)
```

```python
import functools

import numpy as np
import jax
import jax.numpy as jnp
from jax import lax
from jax.experimental import pallas as pl
from jax.experimental.pallas import tpu as pltpu

F32 = jnp.float32
BF16 = jnp.bfloat16

HEAD_DIM = 128
NSA_HEADS = 8
NSA_KV_HEADS = 2
NSA_GROUP = NSA_HEADS // NSA_KV_HEADS
CMP_BLOCK = 32
CMP_STRIDE = 16
CMP_HIDDEN = 256
SEL_BLOCK = 64
SEL_SHIFT = 6
SEL_TOP = 8
WINDOW = 512
MLA_HEADS = 8
MLA_Q_RANK = 384
MLA_KV_RANK = 256
MLA_NOPE_DIM = 128
MLA_ROPE_DIM = 64
MLA_V_DIM = 128
MLA_QK_DIM = MLA_NOPE_DIM + MLA_ROPE_DIM
ROPE_THETA = 10000.0
EPS = 1e-6
NEG = -1e30
FORCE_SCORE = 1e4
NSA_WIDTH = NSA_HEADS * HEAD_DIM
MLA_WIDTH = MLA_HEADS * MLA_V_DIM
NSA_KV_WIDTH = NSA_KV_HEADS * HEAD_DIM

LANES = 128
MLA_QK_PAD = 2 * LANES
VMEM_LIMIT = 58 * 1024 * 1024
POS_SPLIT = 16
FEAT_ROWS = 16


def _cparams(sem):
    return pltpu.CompilerParams(dimension_semantics=sem, vmem_limit_bytes=VMEM_LIMIT)


def _rms(x, gain):
    return x * lax.rsqrt(jnp.mean(x * x, axis=-1, keepdims=True) + EPS) * gain


def _dot(a, b):
    return jnp.dot(a, b, preferred_element_type=F32)


def _dot_nt(a, b):
    return lax.dot_general(a, b, (((1,), (1,)), ((), ())), preferred_element_type=F32)


def _sigmoid(x):
    return 1.0 / (1.0 + jnp.exp(-x))


def _ffn_accumulate(x_ref, g_ref, weights, o_ref, h_ref):
    @pl.when(pl.program_id(1) == 0)
    def _():
        x = x_ref[...]
        h_ref[...] = _rms(x, g_ref[...]).astype(BF16)
        o_ref[...] = x

    wg, wu, wd = weights()
    h = h_ref[...]
    gate = _dot(h, wg)
    up = _dot(h, wu)
    act = gate * _sigmoid(gate) * up * 0.5
    o_ref[...] += _dot(act.astype(BF16), wd)


def _ffn_first_kernel(x_ref, g_ref, wg_ref, wu_ref, wd_ref, o_ref, wg_out, wu_out, wd_out, h_ref):
    def weights():
        wg, wu, wd = wg_ref[...].astype(BF16), wu_ref[...].astype(BF16), wd_ref[...].astype(BF16)
        wg_out[...], wu_out[...], wd_out[...] = wg, wu, wd
        return wg, wu, wd

    _ffn_accumulate(x_ref, g_ref, weights, o_ref, h_ref)


def _ffn_rest_kernel(x_ref, g_ref, wg_ref, wu_ref, wd_ref, o_ref, h_ref):
    _ffn_accumulate(x_ref, g_ref, lambda: (wg_ref[...], wu_ref[...], wd_ref[...]), o_ref, h_ref)


def _ffn(x, gain, wg, wu, wd, *, in_place, tm=1024, tf_first=256, tf=512):
    t, d = x.shape
    f = wg.shape[1]
    alias = {0: 0} if in_place else {}

    def w_specs(tfx):
        return [pl.BlockSpec((d, tfx), lambda i, j: (0, j)),
                pl.BlockSpec((d, tfx), lambda i, j: (0, j)),
                pl.BlockSpec((tfx, d), lambda i, j: (j, 0))]

    y, wg_b, wu_b, wd_b = pl.pallas_call(
        _ffn_first_kernel,
        grid=(1, f // tf_first),
        in_specs=[pl.BlockSpec((tm, d), lambda i, j: (0, 0), pipeline_mode=pl.Buffered(1)),
                  pl.BlockSpec((1, d), lambda i, j: (0, 0))] + w_specs(tf_first),
        out_specs=[pl.BlockSpec((tm, d), lambda i, j: (0, 0))] + w_specs(tf_first),
        out_shape=[jax.ShapeDtypeStruct((t if in_place else tm, d), F32), jax.ShapeDtypeStruct((d, f), BF16),
                   jax.ShapeDtypeStruct((d, f), BF16), jax.ShapeDtypeStruct((f, d), BF16)],
        scratch_shapes=[pltpu.VMEM((tm, d), BF16)],
        input_output_aliases=alias,
        compiler_params=_cparams(("arbitrary", "arbitrary")),
        name="ffn_first",
    )(x, gain, wg, wu, wd)

    out_row0 = 1 if in_place else 0
    rest = pl.pallas_call(
        _ffn_rest_kernel,
        grid=(t // tm - 1, f // tf),
        in_specs=[pl.BlockSpec((tm, d), lambda i, j: (i + 1, 0)),
                  pl.BlockSpec((1, d), lambda i, j: (0, 0))] + w_specs(tf),
        out_specs=pl.BlockSpec((tm, d), lambda i, j: (i + out_row0, 0)),
        out_shape=jax.ShapeDtypeStruct((t if in_place else t - tm, d), F32),
        scratch_shapes=[pltpu.VMEM((tm, d), BF16)],
        input_output_aliases=alias,
        compiler_params=_cparams(("parallel", "arbitrary")),
        name="ffn_rest",
    )(y if in_place else x, gain, wg_b, wu_b, wd_b)
    return rest if in_place else (y, rest)


_N_CMP = 0
_N_KS = _N_CMP + 2 * NSA_KV_WIDTH
_N_KW = _N_KS + NSA_KV_WIDTH
_N_CQ = _N_KW + NSA_KV_WIDTH
_N_CKV = _N_CQ + MLA_Q_RANK
_N_KR = _N_CKV + MLA_KV_RANK
_N_END = _N_KR + 2 * LANES
_T_Q = 0
_T_VS = _T_Q + NSA_WIDTH
_T_VW = _T_VS + NSA_KV_WIDTH
_T_GATE = _T_VW + NSA_KV_WIDTH
_T_END = _T_GATE + LANES


def _two_part_specs(parts, tm):
    first, rest = parts
    nb_first = first.shape[0] // tm
    d = first.shape[1]
    return nb_first, [pl.BlockSpec((tm, d), lambda i: (jnp.minimum(i, nb_first - 1), 0)),
                      pl.BlockSpec((tm, d), lambda i: (jnp.maximum(i - nb_first, 0), 0))]


def _two_part_rows(first_ref, rest_ref, nb_first):
    return jnp.where(pl.program_id(0) < nb_first, first_ref[...], rest_ref[...])


def _proj_kernel(xa_ref, xb_ref, g_ref, wn_ref, wt_ref, qg_ref, kg_ref, qag_ref, kvag_ref,
                 q_t_ref, cmp_ref, ks_ref, vs_t_ref, kw_ref, vw_t_ref, gate_t_ref, cq_ref, ckv_ref, kr_ref, *,
                 nb_first):
    h = _rms(_two_part_rows(xa_ref, xb_ref, nb_first), g_ref[...]).astype(BF16)
    p_n = _dot(h, wn_ref[...])
    p_t = _dot_nt(wt_ref[...], h)

    def proj(lo, hi):
        return p_n[:, lo:hi]

    def proj_t(lo, hi):
        return p_t[lo:hi, :]

    qscale = HEAD_DIM ** -0.5
    for hd in range(NSA_HEADS):
        q = proj_t(_T_Q + hd * HEAD_DIM, _T_Q + (hd + 1) * HEAD_DIM)
        inv = lax.rsqrt(jnp.mean(q * q, axis=0, keepdims=True) + EPS) * qscale
        q_t_ref[hd * HEAD_DIM:(hd + 1) * HEAD_DIM, :] = (q * inv * qg_ref[...]).astype(BF16)
    cmp_ref[...] = proj(_N_CMP, _N_KS).astype(BF16)
    for g in range(NSA_KV_HEADS):
        sl = slice(g * HEAD_DIM, (g + 1) * HEAD_DIM)
        ks_ref[:, sl] = _rms(proj(_N_KS + g * HEAD_DIM, _N_KS + (g + 1) * HEAD_DIM), kg_ref[1:2, :]).astype(BF16)
        kw_ref[:, sl] = _rms(proj(_N_KW + g * HEAD_DIM, _N_KW + (g + 1) * HEAD_DIM), kg_ref[2:3, :]).astype(BF16)
    vs_t_ref[...] = proj_t(_T_VS, _T_VW).astype(BF16)
    vw_t_ref[...] = proj_t(_T_VW, _T_GATE).astype(BF16)
    cq_ref[...] = _rms(proj(_N_CQ, _N_CKV), qag_ref[...]).astype(BF16)
    ckv_ref[...] = _rms(proj(_N_CKV, _N_KR), kvag_ref[...]).astype(BF16)
    kr_ref[...] = proj(_N_KR, _N_END)
    gate_t_ref[...] = _sigmoid(proj_t(_T_GATE, _T_END))


def _proj(x_parts, gain, w_n, w_t, q_gain, k_gains, qa_gain, kva_gain, *, tm=512):
    t = x_parts[0].shape[0] + x_parts[1].shape[0]
    nb_first, x_specs = _two_part_specs(x_parts, tm)
    q_gain = jnp.broadcast_to(q_gain.reshape(HEAD_DIM, 1), (HEAD_DIM, tm))

    def full(a):
        return pl.BlockSpec(a.shape, lambda i: (0,) * a.ndim)

    def rows(width):
        return pl.BlockSpec((tm, width), lambda i: (i, 0))

    def cols(height):
        return pl.BlockSpec((height, tm), lambda i: (0, i))

    outs = [(cols, NSA_WIDTH, BF16), (rows, 2 * NSA_KV_WIDTH, BF16), (rows, NSA_KV_WIDTH, BF16),
            (cols, NSA_KV_WIDTH, BF16), (rows, NSA_KV_WIDTH, BF16), (cols, NSA_KV_WIDTH, BF16),
            (cols, LANES, F32), (rows, MLA_Q_RANK, BF16), (rows, MLA_KV_RANK, BF16), (rows, 2 * LANES, F32)]
    return pl.pallas_call(
        functools.partial(_proj_kernel, nb_first=nb_first),
        grid=(t // tm,),
        in_specs=x_specs + [full(gain), full(w_n), full(w_t), full(q_gain), full(k_gains), full(qa_gain),
                            full(kva_gain)],
        out_specs=[kind(n) for kind, n, _ in outs],
        out_shape=[jax.ShapeDtypeStruct((t, n) if kind is rows else (n, t), dt) for kind, n, dt in outs],
        compiler_params=_cparams(("parallel",)),
        name="proj",
    )(*x_parts, gain, w_n, w_t, q_gain, k_gains, qa_gain, kva_gain)


def _gelu_tanh(x):
    return 0.5 * x * (1.0 + jnp.tanh(np.sqrt(2.0 / np.pi) * (x + 0.044715 * (x * x * x))))


def _compress_kernel(x_ref, w1k_ref, w2k_ref, pk_ref, w1v_ref, w2v_ref, pv_ref, kg_ref, kc_ref, vc_t_ref):
    half = CMP_STRIDE * HEAD_DIM
    row_w = 2 * NSA_KV_WIDTH
    for which, (w1_ref, w2_ref, p_ref) in enumerate(((w1k_ref, w2k_ref, pk_ref), (w1v_ref, w2v_ref, pv_ref))):
        pos_bias = _dot(jnp.broadcast_to(p_ref[...], (8, 2 * half)).astype(BF16), w1_ref[...])[0:1, :]
        for g in range(NSA_KV_HEADS):
            col = which * NSA_KV_WIDTH + g * HEAD_DIM
            xg = jnp.concatenate(
                [x_ref[0, :, l * row_w + col:l * row_w + col + HEAD_DIM] for l in range(CMP_STRIDE)], axis=1)
            first = _dot(xg, w1_ref[0:half, :])
            second = _dot(xg, w1_ref[half:2 * half, :])
            nrow = second.shape[0]
            pre = first + pltpu.roll(second, nrow - 1, 0) + pos_bias
            hidden = _gelu_tanh(pre).astype(BF16)
            if which == 0:
                kc_ref[0, g] = _rms(_dot(hidden, w2_ref[...]), kg_ref[0:1, :]).astype(BF16)
            else:
                vc_t_ref[0, g] = _dot_nt(w2_ref[...], hidden).astype(BF16)


def _compress(xc, w1k, w2k, pk, w1v, w2v, pv, k_gains):
    b, nrow, width = xc.shape

    def full(a):
        return pl.BlockSpec(a.shape, lambda i: (0,) * a.ndim)

    out_sds = jax.ShapeDtypeStruct((b, NSA_KV_HEADS, nrow, HEAD_DIM), BF16)
    out_spec = pl.BlockSpec((1, NSA_KV_HEADS, nrow, HEAD_DIM), lambda i: (i, 0, 0, 0))
    return pl.pallas_call(
        _compress_kernel,
        grid=(b,),
        in_specs=[pl.BlockSpec((1, nrow, width), lambda i: (i, 0, 0)),
                  full(w1k), full(w2k), full(pk), full(w1v), full(w2v), full(pv), full(k_gains)],
        out_specs=[out_spec, out_spec],
        out_shape=[out_sds, out_sds],
        compiler_params=_cparams(("parallel",)),
        name="compress",
    )(xc, w1k, w2k, pk, w1v, w2v, pv, k_gains)


def _softmax_cols(s):
    m = jnp.max(s, axis=0, keepdims=True)
    p = jnp.exp(s - m)
    return m, jnp.sum(p, axis=0, keepdims=True), p


def _nsa_kernel(q_t_ref, kc_ref, vc_t_ref, ks_ref, vs_t_ref, kw_ref, vw_t_ref, gate_t_ref, kfeat_ref, cfeat_ref,
                slope_ref, o_ref, *, tq, ck):
    qi = pl.program_id(1)
    q0 = qi * tq
    ng = NSA_KV_HEADS
    wide = NSA_GROUP * tq
    n_sel = ks_ref.shape[1] // SEL_BLOCK
    ncmp = kc_ref.shape[2]
    n_win = WINDOW + tq

    def tile4(a):
        return jnp.concatenate([a] * NSA_GROUP, axis=1)

    def masked(s, mask, fill=NEG):
        return jnp.concatenate([jnp.where(mask, s[:, r * tq:(r + 1) * tq], fill) for r in range(NSA_GROUP)], axis=1)

    def q_aug(g, extra):
        q = jnp.concatenate([q_t_ref[(g * NSA_GROUP + r) * HEAD_DIM:(g * NSA_GROUP + r + 1) * HEAD_DIM, :]
                             for r in range(NSA_GROUP)], axis=1)
        rows = [q, slope_ref[g]]
        used = slope_ref.shape[1]
        if extra is not None:
            rows.append(extra)
            used += extra.shape[0]
        rows.append(jnp.zeros((HEAD_DIM - used, wide), BF16))
        return jnp.concatenate(rows, axis=0)

    def gsl(g):
        return slice(g * HEAD_DIM, (g + 1) * HEAD_DIM)

    lane_q = q0 + lax.broadcasted_iota(jnp.int32, (1, tq), 1)

    qa_plain = [q_aug(g, None) for g in range(ng)]
    w0 = pl.multiple_of(jnp.maximum(q0 - WINDOW, 0), tq)
    s_cmp = [_dot(jnp.concatenate([kc_ref[0, g], cfeat_ref[...]], axis=1), qa_plain[g]) for g in range(ng)]
    s_win = [_dot(jnp.concatenate([kw_ref[0, pl.ds(w0, n_win), gsl(g)], kfeat_ref[pl.ds(w0, n_win), :]], axis=1),
                  qa_plain[g]) for g in range(ng)]

    n_row = lax.broadcasted_iota(jnp.int32, (ncmp, tq), 0)
    mask_c = n_row * CMP_STRIDE + (CMP_BLOCK - 1) <= lane_q
    s_row = lax.broadcasted_iota(jnp.int32, (n_sel, ncmp), 0)
    n_col = lax.broadcasted_iota(jnp.int32, (n_sel, ncmp), 1)
    overlap_t = jnp.where((n_col * CMP_STRIDE < (s_row + 1) * SEL_BLOCK)
                          & (n_col * CMP_STRIDE + (CMP_BLOCK - 1) >= s_row * SEL_BLOCK), 1.0, 0.0).astype(BF16)
    p_cmp, imp = [], []
    for g in range(ng):
        s = masked(s_cmp[g], mask_c)
        m = jnp.max(s, axis=0, keepdims=True)
        p = masked(jnp.exp(s - m), mask_c, 0.0)
        l = jnp.sum(p, axis=0, keepdims=True)
        p = p / jnp.where(l > 0.0, l, 1.0)
        p_cmp.append(p.astype(BF16))
        p_sum = p[:, 0:tq]
        for r in range(1, NSA_GROUP):
            p_sum = p_sum + p[:, r * tq:(r + 1) * tq]
        p_hi = p_sum.astype(BF16)
        p_lo = (p_sum - p_hi.astype(F32)).astype(BF16)
        imp.append(_dot(overlap_t, p_hi) + _dot(overlap_t, p_lo))
    o_cmp = [_dot(vc_t_ref[0, g], p_cmp[g]) for g in range(ng)]

    blk = lax.broadcasted_iota(jnp.int32, (n_sel, tq), 0)
    blk_f = blk.astype(F32)
    forced = (blk == 0) | (blk == jnp.right_shift(lane_q, SEL_SHIFT))
    future = blk * SEL_BLOCK > lane_q
    penalty = []
    for g in range(ng):
        v = jnp.where(future, NEG, jnp.where(forced, FORCE_SCORE, imp[g]))
        unselected = jnp.full((n_sel, tq), NEG, F32)
        for _ in range(min(SEL_TOP, n_sel)):
            mx = jnp.max(v, axis=0, keepdims=True)
            first = jnp.min(jnp.where(v == mx, blk_f, float(n_sel)), axis=0, keepdims=True)
            pick = blk_f == first
            unselected = jnp.where(pick, 0.0, unselected)
            v = jnp.where(pick, -jnp.inf, v)
        penalty.append(tile4(unselected.astype(BF16)))

    rel = lax.broadcasted_iota(jnp.int32, (n_win, tq), 0) - lax.broadcasted_iota(jnp.int32, (n_win, tq), 1)
    off = q0 - w0
    mask_w = (rel <= off) & (rel > off - WINDOW)
    o_win = []
    p_win = []
    for g in range(ng):
        _, l, p = _softmax_cols(masked(s_win[g], mask_w))
        p_win.append((l, p.astype(BF16)))
    for g in range(ng):
        l, p = p_win[g]
        o_win.append(_dot(vw_t_ref[gsl(g), pl.ds(w0, n_win)], p) / l)

    partial = []
    for g in range(ng):
        for r in range(NSA_GROUP):
            hd = g * NSA_GROUP + r
            cs = slice(r * tq, (r + 1) * tq)
            partial.append(gate_t_ref[3 * hd:3 * hd + 1, :] * o_cmp[g][:, cs]
                           + gate_t_ref[3 * hd + 2:3 * hd + 3, :] * o_win[g][:, cs])

    qa_sel = [q_aug(g, penalty[g]) for g in range(ng)]
    rel_d = lax.broadcasted_iota(jnp.int32, (ck, tq), 0) - lax.broadcasted_iota(jnp.int32, (ck, tq), 1)

    def select_and_store(n_before):
        nk = n_before + ck
        mask_d = rel_d <= q0 - n_before
        ss = [_dot(jnp.concatenate([ks_ref[0, 0:nk, gsl(g)], kfeat_ref[0:nk, :]], axis=1), qa_sel[g])
              for g in range(ng)]
        for g in range(ng):
            s_diag = masked(ss[g][n_before:nk], mask_d)
            m = jnp.max(s_diag, axis=0, keepdims=True)
            if n_before:
                m = jnp.maximum(m, jnp.max(ss[g][0:n_before], axis=0, keepdims=True))
            p_diag = jnp.exp(s_diag - m)
            l = jnp.sum(p_diag, axis=0, keepdims=True)
            p = p_diag.astype(BF16)
            if n_before:
                p_before = jnp.exp(ss[g][0:n_before] - m)
                l = l + jnp.sum(p_before, axis=0, keepdims=True)
                p = jnp.concatenate([p_before.astype(BF16), p], axis=0)
            o_sel = _dot(vs_t_ref[gsl(g), 0:nk], p) / l
            for r in range(NSA_GROUP):
                hd = g * NSA_GROUP + r
                mixed = partial[hd] + gate_t_ref[3 * hd + 1:3 * hd + 2, :] * o_sel[:, r * tq:(r + 1) * tq]
                o_ref[0, :, hd * HEAD_DIM:(hd + 1) * HEAD_DIM] = mixed.T

    per = ck // tq
    for v in range(ks_ref.shape[1] // ck):
        pl.when(qi // per == v)(functools.partial(select_and_store, v * ck))


def _nsa(q_t, kc, vc_t, ks, vs_t, kw, vw_t, gate_t, *, tq=128, ck=256):
    b, s, _ = ks.shape
    nq = s // tq
    pos = np.arange(s)
    kfeat = np.zeros((s, LANES), np.float32)
    kfeat[:, 0] = POS_SPLIT * (pos // POS_SPLIT)
    kfeat[:, 1] = pos % POS_SPLIT
    kfeat[pos, FEAT_ROWS + pos // SEL_BLOCK] = 1.0
    ncmp = kc.shape[2]
    cfeat = np.zeros((ncmp, LANES), np.float32)
    cfeat[:, 0] = CMP_STRIDE * np.arange(ncmp)
    cfeat[:, 1] = 0.5 * (CMP_BLOCK - 1)
    slope = np.zeros((NSA_KV_HEADS, FEAT_ROWS, NSA_GROUP * tq), np.float32)
    for hd in range(NSA_HEADS):
        g, r = divmod(hd, NSA_GROUP)
        slope[g, 0:2, r * tq:(r + 1) * tq] = 2.0 ** (-8.0 * (hd + 1) / NSA_HEADS)
    assert FEAT_ROWS + s // SEL_BLOCK <= LANES and s // SEL_BLOCK % FEAT_ROWS == 0 and s % ck == 0 and ck % tq == 0

    def full(a):
        return pl.BlockSpec(a.shape, lambda i, j: (0,) * a.ndim)

    def per_b(a):
        return pl.BlockSpec((1,) + a.shape[1:], lambda i, j: (i,) + (0,) * (a.ndim - 1))

    def per_b_cols(a):
        return pl.BlockSpec((a.shape[0], s), lambda i, j: (0, i))

    def q_cols(a):
        return pl.BlockSpec((a.shape[0], tq), lambda i, j: (0, i * nq + j))

    consts = [jnp.asarray(kfeat, BF16), jnp.asarray(cfeat, BF16), jnp.asarray(slope, BF16)]
    return pl.pallas_call(
        functools.partial(_nsa_kernel, tq=tq, ck=ck),
        grid=(b, nq),
        in_specs=[q_cols(q_t), per_b(kc), per_b(vc_t), per_b(ks), per_b_cols(vs_t), per_b(kw), per_b_cols(vw_t),
                  q_cols(gate_t)] + [full(c) for c in consts],
        out_specs=pl.BlockSpec((1, tq, NSA_WIDTH), lambda i, j: (i, j, 0)),
        out_shape=jax.ShapeDtypeStruct((b, s, NSA_WIDTH), F32),
        compiler_params=_cparams(("parallel", "arbitrary")),
        name="nsa",
    )(q_t, kc, vc_t, ks, vs_t, kw, vw_t, gate_t, *consts)


def _mla_prep_kernel(cq_ref, ckv_ref, kr_ref, cos_ref, sin_ref, cos_t_ref, sin_t_ref, wq_t_ref, wqr_t_ref, wk_ref,
                     wv_t_ref, qgn_ref, qgr_ref, qgt_ref, kgn_ref, kgr_ref, kgt_ref, q_t_ref, k_ref, v_t_ref):
    cq = cq_ref[0]
    ckv = ckv_ref[0]
    k_rope = kr_ref[0, :, 0:LANES]
    k_rot = kr_ref[0, :, LANES:2 * LANES]
    k_rope_ssq = jnp.sum(k_rope * k_rope, axis=-1, keepdims=True)
    k_rope_emb = k_rope * kgr_ref[...] * cos_ref[...] + k_rot * kgt_ref[...] * sin_ref[...]
    q_cos = qgr_ref[...] * cos_t_ref[...]
    q_sin = qgt_ref[...] * sin_t_ref[...]
    qscale = MLA_QK_DIM ** -0.5
    qx_all = _dot_nt(wq_t_ref[...], cq)
    q_rot_all = _dot_nt(wqr_t_ref[...], cq)
    k_nope_all = _dot(ckv, wk_ref[...])
    v_all = _dot_nt(wv_t_ref[...], ckv)
    for hd in range(MLA_HEADS):
        q_nope = qx_all[hd * MLA_QK_PAD:hd * MLA_QK_PAD + LANES]
        q_rope = qx_all[hd * MLA_QK_PAD + LANES:(hd + 1) * MLA_QK_PAD]
        q_rot = q_rot_all[hd * LANES:(hd + 1) * LANES]
        ssq = jnp.sum(q_nope * q_nope, axis=0, keepdims=True) + jnp.sum(q_rope * q_rope, axis=0, keepdims=True)
        inv = lax.rsqrt(ssq / MLA_QK_DIM + EPS) * qscale
        q_t_ref[0, hd, 0:LANES, :] = (q_nope * inv * qgn_ref[...]).astype(BF16)
        q_t_ref[0, hd, LANES:2 * LANES, :] = ((q_rope * q_cos + q_rot * q_sin) * inv).astype(BF16)

        k_nope = k_nope_all[:, hd * LANES:(hd + 1) * LANES]
        ssq = jnp.sum(k_nope * k_nope, axis=-1, keepdims=True) + k_rope_ssq
        inv = lax.rsqrt(ssq / MLA_QK_DIM + EPS)
        k_ref[0, hd, :, 0:LANES] = (k_nope * inv * kgn_ref[...]).astype(BF16)
        k_ref[0, hd, :, LANES:2 * LANES] = (k_rope_emb * inv).astype(BF16)
        v_t_ref[0, hd] = v_all[hd * LANES:(hd + 1) * LANES].astype(BF16)


def _mla_prep(cq, ckv, kr, tables, weights, q_gains, k_gains, *, tm=256):
    b, s, _ = cq.shape
    cos, sin = tables
    q_gains = tuple(jnp.broadcast_to(g.reshape(LANES, 1), (LANES, tm)) for g in q_gains)

    def full(a):
        return pl.BlockSpec(a.shape, lambda i, j: (0,) * a.ndim)

    def rows(width):
        return pl.BlockSpec((1, tm, width), lambda i, j: (i, j, 0))

    pos_rows = pl.BlockSpec((tm, LANES), lambda i, j: (j, 0))
    pos_cols = pl.BlockSpec((LANES, tm), lambda i, j: (0, j))
    return pl.pallas_call(
        _mla_prep_kernel,
        grid=(b, s // tm),
        in_specs=[rows(MLA_Q_RANK), rows(MLA_KV_RANK), rows(2 * LANES), pos_rows, pos_rows, pos_cols, pos_cols]
                 + [full(w) for w in weights] + [full(g) for g in q_gains + k_gains],
        out_specs=[pl.BlockSpec((1, MLA_HEADS, MLA_QK_PAD, tm), lambda i, j: (i, 0, 0, j)),
                   pl.BlockSpec((1, MLA_HEADS, tm, MLA_QK_PAD), lambda i, j: (i, 0, j, 0)),
                   pl.BlockSpec((1, MLA_HEADS, MLA_V_DIM, tm), lambda i, j: (i, 0, 0, j))],
        out_shape=[jax.ShapeDtypeStruct((b, MLA_HEADS, MLA_QK_PAD, s), BF16),
                   jax.ShapeDtypeStruct((b, MLA_HEADS, s, MLA_QK_PAD), BF16),
                   jax.ShapeDtypeStruct((b, MLA_HEADS, MLA_V_DIM, s), BF16)],
        compiler_params=_cparams(("parallel", "parallel")),
        name="mla_prep",
    )(cq, ckv, kr, cos, sin, cos.T, sin.T, *weights, *q_gains, *k_gains)


def _mla_attn_kernel(q_t_ref, k_ref, v_t_ref, o_ref, *, tq, hb, n_q):
    qi = pl.program_id(2)
    d_v = v_t_ref.shape[2]
    tri = lax.broadcasted_iota(jnp.int32, (tq, tq), 0) <= lax.broadcasted_iota(jnp.int32, (tq, tq), 1)

    def attend(n_before):
        nk = n_before + tq
        ss = [_dot(k_ref[0, h, 0:nk, :], q_t_ref[0, h]) for h in range(hb)]
        for h in range(hb):
            s_diag = jnp.where(tri, ss[h][n_before:nk], NEG)
            m = jnp.max(s_diag, axis=0, keepdims=True)
            if n_before:
                m = jnp.maximum(m, jnp.max(ss[h][0:n_before], axis=0, keepdims=True))
            p_diag = jnp.exp(s_diag - m)
            l = jnp.sum(p_diag, axis=0, keepdims=True)
            p = p_diag.astype(BF16)
            if n_before:
                p_before = jnp.exp(ss[h][0:n_before] - m)
                l = l + jnp.sum(p_before, axis=0, keepdims=True)
                p = jnp.concatenate([p_before.astype(BF16), p], axis=0)
            o_ref[0, :, h * d_v:(h + 1) * d_v] = (_dot(v_t_ref[0, h, :, 0:nk], p) / l).T

    for v in range(n_q):
        pl.when(qi == v)(functools.partial(attend, v * tq))


def _mla_attn(q_t, k, v_t, *, tq=256, hb=4):
    b, h, s, dq = k.shape
    dv = v_t.shape[2]
    return pl.pallas_call(
        functools.partial(_mla_attn_kernel, tq=tq, hb=hb, n_q=s // tq),
        grid=(b, h // hb, s // tq),
        in_specs=[pl.BlockSpec((1, hb, dq, tq), lambda i, j, l: (i, j, 0, l)),
                  pl.BlockSpec((1, hb, s, dq), lambda i, j, l: (i, j, 0, 0)),
                  pl.BlockSpec((1, hb, dv, s), lambda i, j, l: (i, j, 0, 0))],
        out_specs=pl.BlockSpec((1, tq, hb * dv), lambda i, j, l: (i, l, j)),
        out_shape=jax.ShapeDtypeStruct((b, s, h * dv), F32),
        compiler_params=_cparams(("parallel", "parallel", "arbitrary")),
        name="mla_attn",
    )(q_t, k, v_t)


def _out_kernel(xa_ref, xb_ref, oa_ref, ob_ref, ga_ref, gb_ref, w_ref, o_ref, *, nb_first):
    oa = _rms(oa_ref[...], ga_ref[...]).astype(BF16)
    ob = _rms(ob_ref[...], gb_ref[...]).astype(BF16)
    o_ref[...] = (_two_part_rows(xa_ref, xb_ref, nb_first)
                  + _dot(jnp.concatenate([oa, ob], axis=1), w_ref[...]))


def _out_proj(x_parts, oa, ob, gain_a, gain_b, w, *, tm=256):
    t, d = oa.shape[0], x_parts[0].shape[1]
    nb_first, x_specs = _two_part_specs(x_parts, tm)

    def rows(width):
        return pl.BlockSpec((tm, width), lambda i: (i, 0))

    def full(a):
        return pl.BlockSpec(a.shape, lambda i: (0,) * a.ndim)

    return pl.pallas_call(
        functools.partial(_out_kernel, nb_first=nb_first),
        grid=(t // tm,),
        in_specs=x_specs + [rows(oa.shape[1]), rows(ob.shape[1]), full(gain_a), full(gain_b), full(w)],
        out_specs=rows(d),
        out_shape=jax.ShapeDtypeStruct((t, d), F32),
        compiler_params=_cparams(("parallel",)),
        name="out_proj",
    )(*x_parts, oa, ob, gain_a, gain_b, w)


def _rot_cols(w):
    half = w.shape[1] // 2
    return jnp.concatenate([-w[:, half:], w[:, :half]], axis=1)


def _layout_w_in(w_in):
    d = w_in.shape[0]
    off = np.cumsum([0, NSA_WIDTH] + [NSA_KV_WIDTH] * 6 + [NSA_HEADS * 3, MLA_Q_RANK, MLA_KV_RANK, MLA_ROPE_DIM])
    q, kc, vc, ks, vs, kw, vw, gt, cq, ckv, kr = [w_in[:, off[i]:off[i + 1]] for i in range(11)]
    zr = jnp.zeros((d, LANES - MLA_ROPE_DIM), w_in.dtype)
    zg = jnp.zeros((d, LANES - NSA_HEADS * 3), w_in.dtype)
    w_n = jnp.concatenate([kc, vc, ks, kw, cq, ckv, kr, zr, _rot_cols(kr), zr], axis=1).astype(BF16)
    w_t = jnp.concatenate([q, vs, vw, gt, zg], axis=1).astype(BF16).T
    return w_n, w_t


def _layout_mla_weights(w_uq, w_ukv):
    r = w_uq.shape[0]
    zr = jnp.zeros((r, LANES - MLA_ROPE_DIM), w_uq.dtype)
    main, rot, k_cols, v_cols = [], [], [], []
    for hd in range(MLA_HEADS):
        base = hd * MLA_QK_DIM
        rope = w_uq[:, base + MLA_NOPE_DIM:base + MLA_QK_DIM]
        main += [w_uq[:, base:base + MLA_NOPE_DIM], rope, zr]
        rot += [_rot_cols(rope), zr]
        kv = hd * (MLA_NOPE_DIM + MLA_V_DIM)
        k_cols.append(w_ukv[:, kv:kv + MLA_NOPE_DIM])
        v_cols.append(w_ukv[:, kv + MLA_NOPE_DIM:kv + MLA_NOPE_DIM + MLA_V_DIM])
    cat = lambda parts: jnp.concatenate(parts, axis=1).astype(BF16)
    return cat(main).T, cat(rot).T, cat(k_cols), cat(v_cols).T


def _rope_gains(gain):
    half = MLA_ROPE_DIM // 2
    zr = jnp.zeros((LANES - MLA_ROPE_DIM,), gain.dtype)
    g_rope = gain[MLA_NOPE_DIM:]
    return (gain[None, :MLA_NOPE_DIM], jnp.concatenate([g_rope, zr])[None, :],
            jnp.concatenate([g_rope[half:], g_rope[:half], zr])[None, :])


def _rope_tables(s):
    half = MLA_ROPE_DIM // 2
    inv = ROPE_THETA ** (-jnp.arange(half, dtype=F32) / half)
    ang = jnp.arange(s).astype(F32)[:, None] * inv[None, :]
    zr = jnp.zeros((s, LANES - MLA_ROPE_DIM), F32)
    cos, sin = jnp.cos(ang), jnp.sin(ang)
    return jnp.concatenate([cos, cos, zr], axis=1), jnp.concatenate([sin, sin, zr], axis=1)


def kernel(x, ffn1_norm, ffn1_w_gate, ffn1_w_up, ffn1_w_down, mix_norm, w_in, nsa_q_norm, nsa_k_norm, nsa_cmp_pos_k, nsa_cmp_w1_k, nsa_cmp_w2_k, nsa_cmp_pos_v, nsa_cmp_w1_v, nsa_cmp_w2_v, mla_q_a_norm, mla_w_uq, mla_kv_a_norm, mla_w_ukv, mla_q_norm, mla_k_norm, out_norm_nsa, out_norm_mla, w_out, ffn2_norm, ffn2_w_gate, ffn2_w_up, ffn2_w_down):
    b, s, d = x.shape
    depth = ffn1_norm.shape[0]
    tables = _rope_tables(s)
    xt = x.reshape(b * s, d)
    for l in range(depth):
        x_parts = _ffn(xt, ffn1_norm[l][None, :], ffn1_w_gate[l], ffn1_w_up[l], ffn1_w_down[l], in_place=False)

        w_n, w_t = _layout_w_in(w_in[l])
        q_t, kv_cmp, ks, vs_t, kw, vw_t, gate_t, cq, ckv, kr = _proj(
            x_parts, mix_norm[l][None, :], w_n, w_t, nsa_q_norm[l], nsa_k_norm[l],
            mla_q_a_norm[l][None, :], mla_kv_a_norm[l][None, :])

        kc, vc_t = _compress(
            kv_cmp.reshape(b, s // CMP_STRIDE, CMP_STRIDE * 2 * NSA_KV_WIDTH),
            nsa_cmp_w1_k[l].astype(BF16), nsa_cmp_w2_k[l].astype(BF16), nsa_cmp_pos_k[l].reshape(1, -1),
            nsa_cmp_w1_v[l].astype(BF16), nsa_cmp_w2_v[l].astype(BF16).T, nsa_cmp_pos_v[l].reshape(1, -1),
            nsa_k_norm[l])

        def bs(a):
            return a.reshape(b, s, a.shape[-1])

        o_a = _nsa(q_t, kc, vc_t, bs(ks), vs_t, bs(kw), vw_t, gate_t)

        q_m, k_m, v_m = _mla_prep(
            bs(cq), bs(ckv), bs(kr), tables, _layout_mla_weights(mla_w_uq[l], mla_w_ukv[l]),
            _rope_gains(mla_q_norm[l]), _rope_gains(mla_k_norm[l]))
        o_b = _mla_attn(q_m, k_m, v_m)

        xt = _out_proj(x_parts, o_a.reshape(b * s, NSA_WIDTH), o_b.reshape(b * s, MLA_WIDTH),
                       out_norm_nsa[l][None, :], out_norm_mla[l][None, :], w_out[l].astype(BF16))

        xt = _ffn(xt, ffn2_norm[l][None, :], ffn2_w_gate[l], ffn2_w_up[l], ffn2_w_down[l], in_place=True)
    return xt.reshape(b, s, d)
```

```python
import functools

import numpy as np
import jax
import jax.numpy as jnp
from jax import lax
from jax.experimental import pallas as pl
from jax.experimental.pallas import tpu as pltpu

F32 = jnp.float32
BF16 = jnp.bfloat16

HEAD_DIM = 128
NSA_HEADS = 8
NSA_KV_HEADS = 2
NSA_GROUP = NSA_HEADS // NSA_KV_HEADS
CMP_BLOCK = 32
CMP_STRIDE = 16
CMP_HIDDEN = 256
SEL_BLOCK = 64
SEL_SHIFT = 6
SEL_TOP = 8
WINDOW = 512
MLA_HEADS = 8
MLA_Q_RANK = 384
MLA_KV_RANK = 256
MLA_NOPE_DIM = 128
MLA_ROPE_DIM = 64
MLA_V_DIM = 128
MLA_QK_DIM = MLA_NOPE_DIM + MLA_ROPE_DIM
ROPE_THETA = 10000.0
EPS = 1e-6
NEG = -1e30
FORCE_SCORE = 1e4
NSA_WIDTH = NSA_HEADS * HEAD_DIM
MLA_WIDTH = MLA_HEADS * MLA_V_DIM
NSA_KV_WIDTH = NSA_KV_HEADS * HEAD_DIM

LANES = 128
MLA_QK_PAD = 2 * LANES
VMEM_LIMIT = 58 * 1024 * 1024
POS_SPLIT = 16
FEAT_ROWS = 16


def _cparams(sem):
    return pltpu.CompilerParams(dimension_semantics=sem, vmem_limit_bytes=VMEM_LIMIT)


def _rms(x, gain):
    return x * lax.rsqrt(jnp.mean(x * x, axis=-1, keepdims=True) + EPS) * gain


def _dot(a, b):
    return jnp.dot(a, b, preferred_element_type=F32)


def _dot_nt(a, b):
    return lax.dot_general(a, b, (((1,), (1,)), ((), ())), preferred_element_type=F32)


def _sigmoid(x):
    return 1.0 / (1.0 + jnp.exp(-x))


def _ffn_accumulate(x_ref, g_ref, weights, o_ref, h_ref):
    @pl.when(pl.program_id(1) == 0)
    def _():
        x = x_ref[...]
        h_ref[...] = _rms(x, g_ref[...]).astype(BF16)
        o_ref[...] = x

    wg, wu, wd = weights()
    h = h_ref[...]
    gate = _dot(h, wg)
    up = _dot(h, wu)
    act = gate * _sigmoid(gate) * up * 0.5
    o_ref[...] += _dot(act.astype(BF16), wd)


def _ffn_first_kernel(x_ref, g_ref, wg_ref, wu_ref, wd_ref, o_ref, wg_out, wu_out, wd_out, h_ref):
    def weights():
        wg, wu, wd = wg_ref[...].astype(BF16), wu_ref[...].astype(BF16), wd_ref[...].astype(BF16)
        wg_out[...], wu_out[...], wd_out[...] = wg, wu, wd
        return wg, wu, wd

    _ffn_accumulate(x_ref, g_ref, weights, o_ref, h_ref)


def _ffn_rest_kernel(x_ref, g_ref, wg_ref, wu_ref, wd_ref, o_ref, h_ref):
    _ffn_accumulate(x_ref, g_ref, lambda: (wg_ref[...], wu_ref[...], wd_ref[...]), o_ref, h_ref)


def _ffn(x, gain, wg, wu, wd, *, in_place, tm=1024, tf_first=256, tf=512):
    t, d = x.shape
    f = wg.shape[1]
    alias = {0: 0} if in_place else {}

    def w_specs(tfx):
        return [pl.BlockSpec((d, tfx), lambda i, j: (0, j)),
                pl.BlockSpec((d, tfx), lambda i, j: (0, j)),
                pl.BlockSpec((tfx, d), lambda i, j: (j, 0))]

    y, wg_b, wu_b, wd_b = pl.pallas_call(
        _ffn_first_kernel,
        grid=(1, f // tf_first),
        in_specs=[pl.BlockSpec((tm, d), lambda i, j: (0, 0), pipeline_mode=pl.Buffered(1)),
                  pl.BlockSpec((1, d), lambda i, j: (0, 0))] + w_specs(tf_first),
        out_specs=[pl.BlockSpec((tm, d), lambda i, j: (0, 0))] + w_specs(tf_first),
        out_shape=[jax.ShapeDtypeStruct((t if in_place else tm, d), F32), jax.ShapeDtypeStruct((d, f), BF16),
                   jax.ShapeDtypeStruct((d, f), BF16), jax.ShapeDtypeStruct((f, d), BF16)],
        scratch_shapes=[pltpu.VMEM((tm, d), BF16)],
        input_output_aliases=alias,
        compiler_params=_cparams(("arbitrary", "arbitrary")),
        name="ffn_first",
    )(x, gain, wg, wu, wd)

    out_row0 = 1 if in_place else 0
    rest = pl.pallas_call(
        _ffn_rest_kernel,
        grid=(t // tm - 1, f // tf),
        in_specs=[pl.BlockSpec((tm, d), lambda i, j: (i + 1, 0)),
                  pl.BlockSpec((1, d), lambda i, j: (0, 0))] + w_specs(tf),
        out_specs=pl.BlockSpec((tm, d), lambda i, j: (i + out_row0, 0)),
        out_shape=jax.ShapeDtypeStruct((t if in_place else t - tm, d), F32),
        scratch_shapes=[pltpu.VMEM((tm, d), BF16)],
        input_output_aliases=alias,
        compiler_params=_cparams(("parallel", "arbitrary")),
        name="ffn_rest",
    )(y if in_place else x, gain, wg_b, wu_b, wd_b)
    return rest if in_place else (y, rest)


_N_CMP = 0
_N_KS = _N_CMP + 2 * NSA_KV_WIDTH
_N_KW = _N_KS + NSA_KV_WIDTH
_N_CQ = _N_KW + NSA_KV_WIDTH
_N_CKV = _N_CQ + MLA_Q_RANK
_N_KR = _N_CKV + MLA_KV_RANK
_N_END = _N_KR + 2 * LANES
_T_Q = 0
_T_VS = _T_Q + NSA_WIDTH
_T_VW = _T_VS + NSA_KV_WIDTH
_T_GATE = _T_VW + NSA_KV_WIDTH
_T_END = _T_GATE + LANES


def _two_part_specs(parts, tm):
    first, rest = parts
    nb_first = first.shape[0] // tm
    d = first.shape[1]
    return nb_first, [pl.BlockSpec((tm, d), lambda i: (jnp.minimum(i, nb_first - 1), 0)),
                      pl.BlockSpec((tm, d), lambda i: (jnp.maximum(i - nb_first, 0), 0))]


def _two_part_rows(first_ref, rest_ref, nb_first):
    return jnp.where(pl.program_id(0) < nb_first, first_ref[...], rest_ref[...])


def _proj_kernel(xa_ref, xb_ref, g_ref, wn_ref, wt_ref, qg_ref, kg_ref, qag_ref, kvag_ref,
                 q_t_ref, cmp_ref, ks_ref, vs_t_ref, kw_ref, vw_t_ref, gate_t_ref, cq_ref, ckv_ref, kr_ref, *,
                 nb_first):
    tq = q_t_ref.shape[2]
    h = _rms(_two_part_rows(xa_ref, xb_ref, nb_first), g_ref[...]).astype(BF16)
    p_n = _dot(h, wn_ref[...])
    p_t = _dot_nt(wt_ref[...], h)

    def proj(lo, hi):
        return p_n[:, lo:hi]

    def proj_t(lo, hi):
        return p_t[lo:hi, :]

    qscale = HEAD_DIM ** -0.5
    for hd in range(NSA_HEADS):
        q = proj_t(_T_Q + hd * HEAD_DIM, _T_Q + (hd + 1) * HEAD_DIM)
        inv = lax.rsqrt(jnp.mean(q * q, axis=0, keepdims=True) + EPS) * qscale
        qn = (q * inv * qg_ref[...]).astype(BF16)
        for c in range(q_t_ref.shape[0]):
            q_t_ref[c, hd * HEAD_DIM:(hd + 1) * HEAD_DIM, :] = qn[:, c * tq:(c + 1) * tq]
    for c in range(cmp_ref.shape[0]):
        cmp_ref[c] = proj(_N_CMP + c * HEAD_DIM, _N_CMP + (c + 1) * HEAD_DIM)
    for g in range(NSA_KV_HEADS):
        sl = slice(g * HEAD_DIM, (g + 1) * HEAD_DIM)
        ks_ref[:, sl] = _rms(proj(_N_KS + g * HEAD_DIM, _N_KS + (g + 1) * HEAD_DIM), kg_ref[1:2, :]).astype(BF16)
        kw_ref[:, sl] = _rms(proj(_N_KW + g * HEAD_DIM, _N_KW + (g + 1) * HEAD_DIM), kg_ref[2:3, :]).astype(BF16)
    vs_t_ref[...] = proj_t(_T_VS, _T_VW).astype(BF16)
    vw_t_ref[...] = proj_t(_T_VW, _T_GATE).astype(BF16)
    cq_ref[...] = _rms(proj(_N_CQ, _N_CKV), qag_ref[...]).astype(BF16)
    ckv_ref[...] = _rms(proj(_N_CKV, _N_KR), kvag_ref[...]).astype(BF16)
    kr_ref[...] = proj(_N_KR, _N_END)
    gates = _sigmoid(proj_t(_T_GATE, _T_END))
    for c in range(gate_t_ref.shape[0]):
        gate_t_ref[c] = gates[:, c * tq:(c + 1) * tq]


def _proj(x_parts, gain, w_n, w_t, q_gain, k_gains, qa_gain, kva_gain, *, tm=512, tq=128):
    t = x_parts[0].shape[0] + x_parts[1].shape[0]
    nb_first, x_specs = _two_part_specs(x_parts, tm)
    q_gain = jnp.broadcast_to(q_gain.reshape(HEAD_DIM, 1), (HEAD_DIM, tm))

    def full(a):
        return pl.BlockSpec(a.shape, lambda i: (0,) * a.ndim)

    def rows(width):
        return (t, width), pl.BlockSpec((tm, width), lambda i: (i, 0))

    def cols(height):
        return (height, t), pl.BlockSpec((height, tm), lambda i: (0, i))

    def q_tiles(height):
        return (t // tq, height, tq), pl.BlockSpec((tm // tq, height, tq), lambda i: (i, 0, 0))

    n_cmp_cols = 2 * NSA_KV_WIDTH // HEAD_DIM
    cmp_out = (n_cmp_cols, t, HEAD_DIM), pl.BlockSpec((n_cmp_cols, tm, HEAD_DIM), lambda i: (0, i, 0))
    outs = [(q_tiles(NSA_WIDTH), BF16), (cmp_out, F32), (rows(NSA_KV_WIDTH), BF16),
            (cols(NSA_KV_WIDTH), BF16), (rows(NSA_KV_WIDTH), BF16), (cols(NSA_KV_WIDTH), BF16),
            (q_tiles(LANES), F32), (rows(MLA_Q_RANK), BF16), (rows(MLA_KV_RANK), BF16), (rows(2 * LANES), F32)]
    return pl.pallas_call(
        functools.partial(_proj_kernel, nb_first=nb_first),
        grid=(t // tm,),
        in_specs=x_specs + [full(gain), full(w_n), full(w_t), full(q_gain), full(k_gains), full(qa_gain),
                            full(kva_gain)],
        out_specs=[spec for (_, spec), _ in outs],
        out_shape=[jax.ShapeDtypeStruct(shape, dt) for (shape, _), dt in outs],
        compiler_params=_cparams(("parallel",)),
        name="proj",
    )(*x_parts, gain, w_n, w_t, q_gain, k_gains, qa_gain, kva_gain)


def _gelu_tanh(x):
    return 0.5 * x * (1.0 + jnp.tanh(np.sqrt(2.0 / np.pi) * (x + 0.044715 * (x * x * x))))


def _compress_kernel(x_ref, w1k_ref, w2k_ref, pk_ref, w1v_ref, w2v_ref, pv_ref, kg_ref, kc_ref, vc_t_ref):
    half = CMP_STRIDE * HEAD_DIM
    nrow = x_ref.shape[1] // CMP_STRIDE
    for which, (w1_ref, w2_ref, p_ref) in enumerate(((w1k_ref, w2k_ref, pk_ref), (w1v_ref, w2v_ref, pv_ref))):
        pos_bias = _dot(jnp.broadcast_to(p_ref[...], (8, 2 * half)).astype(BF16), w1_ref[...])[0:1, :]
        for g in range(NSA_KV_HEADS):
            col = which * NSA_KV_HEADS + g
            xg = jnp.concatenate(
                [x_ref[col, pl.ds(l, nrow, stride=CMP_STRIDE), :].astype(BF16) for l in range(CMP_STRIDE)], axis=1)
            first = _dot(xg, w1_ref[0:half, :])
            second = _dot(xg, w1_ref[half:2 * half, :])
            pre = first + pltpu.roll(second, nrow - 1, 0) + pos_bias
            hidden = _gelu_tanh(pre).astype(BF16)
            if which == 0:
                kc_ref[0, g] = _rms(_dot(hidden, w2_ref[...]), kg_ref[0:1, :]).astype(BF16)
            else:
                vc_t_ref[0, g] = _dot_nt(w2_ref[...], hidden).astype(BF16)


def _compress(xc, seq, w1k, w2k, pk, w1v, w2v, pv, k_gains):
    ncol, t, width = xc.shape
    b, nrow = t // seq, seq // CMP_STRIDE

    def full(a):
        return pl.BlockSpec(a.shape, lambda i: (0,) * a.ndim)

    out_sds = jax.ShapeDtypeStruct((b, NSA_KV_HEADS, nrow, HEAD_DIM), BF16)
    out_spec = pl.BlockSpec((1, NSA_KV_HEADS, nrow, HEAD_DIM), lambda i: (i, 0, 0, 0))
    return pl.pallas_call(
        _compress_kernel,
        grid=(b,),
        in_specs=[pl.BlockSpec((ncol, seq, width), lambda i: (0, i, 0)),
                  full(w1k), full(w2k), full(pk), full(w1v), full(w2v), full(pv), full(k_gains)],
        out_specs=[out_spec, out_spec],
        out_shape=[out_sds, out_sds],
        compiler_params=_cparams(("parallel",)),
        name="compress",
    )(xc, w1k, w2k, pk, w1v, w2v, pv, k_gains)


def _softmax_cols(s):
    m = jnp.max(s, axis=0, keepdims=True)
    p = jnp.exp(s - m)
    return m, jnp.sum(p, axis=0, keepdims=True), p


def _nsa_kernel(q_t_ref, kc_ref, vc_t_ref, ks_ref, vs_t_ref, kw_ref, vw_t_ref, gate_t_ref, kfeat_ref, cfeat_ref,
                slope_ref, o_ref, *, tq, ck):
    qi = pl.program_id(1)
    q0 = qi * tq
    ng = NSA_KV_HEADS
    wide = NSA_GROUP * tq
    n_sel = ks_ref.shape[1] // SEL_BLOCK
    ncmp = kc_ref.shape[2]
    n_win = WINDOW + tq

    def tile4(a):
        return jnp.concatenate([a] * NSA_GROUP, axis=1)

    def masked(s, mask, fill=NEG):
        return jnp.concatenate([jnp.where(mask, s[:, r * tq:(r + 1) * tq], fill) for r in range(NSA_GROUP)], axis=1)

    def q_aug(g, extra):
        q = jnp.concatenate([q_t_ref[0, (g * NSA_GROUP + r) * HEAD_DIM:(g * NSA_GROUP + r + 1) * HEAD_DIM, :]
                             for r in range(NSA_GROUP)], axis=1)
        rows = [q, slope_ref[g]]
        used = slope_ref.shape[1]
        if extra is not None:
            rows.append(extra)
            used += extra.shape[0]
        rows.append(jnp.zeros((HEAD_DIM - used, wide), BF16))
        return jnp.concatenate(rows, axis=0)

    def gsl(g):
        return slice(g * HEAD_DIM, (g + 1) * HEAD_DIM)

    lane_q = q0 + lax.broadcasted_iota(jnp.int32, (1, tq), 1)

    qa_plain = [q_aug(g, None) for g in range(ng)]
    w0 = pl.multiple_of(jnp.maximum(q0 - WINDOW, 0), tq)
    s_cmp = [_dot(jnp.concatenate([kc_ref[0, g], cfeat_ref[...]], axis=1), qa_plain[g]) for g in range(ng)]
    s_win = [_dot(jnp.concatenate([kw_ref[0, pl.ds(w0, n_win), gsl(g)], kfeat_ref[pl.ds(w0, n_win), :]], axis=1),
                  qa_plain[g]) for g in range(ng)]

    n_row = lax.broadcasted_iota(jnp.int32, (ncmp, tq), 0)
    mask_c = n_row * CMP_STRIDE + (CMP_BLOCK - 1) <= lane_q
    s_row = lax.broadcasted_iota(jnp.int32, (n_sel, ncmp), 0)
    n_col = lax.broadcasted_iota(jnp.int32, (n_sel, ncmp), 1)
    overlap_t = jnp.where((n_col * CMP_STRIDE < (s_row + 1) * SEL_BLOCK)
                          & (n_col * CMP_STRIDE + (CMP_BLOCK - 1) >= s_row * SEL_BLOCK), 1.0, 0.0).astype(BF16)
    p_cmp, imp = [], []
    for g in range(ng):
        s = masked(s_cmp[g], mask_c)
        m = jnp.max(s, axis=0, keepdims=True)
        p = masked(jnp.exp(s - m), mask_c, 0.0)
        l = jnp.sum(p, axis=0, keepdims=True)
        p = p / jnp.where(l > 0.0, l, 1.0)
        p_cmp.append(p.astype(BF16))
        p_sum = p[:, 0:tq]
        for r in range(1, NSA_GROUP):
            p_sum = p_sum + p[:, r * tq:(r + 1) * tq]
        p_hi = p_sum.astype(BF16)
        p_lo = (p_sum - p_hi.astype(F32)).astype(BF16)
        imp.append(_dot(overlap_t, p_hi) + _dot(overlap_t, p_lo))
    o_cmp = [_dot(vc_t_ref[0, g], p_cmp[g]) for g in range(ng)]

    blk = lax.broadcasted_iota(jnp.int32, (n_sel, tq), 0)
    blk_f = blk.astype(F32)
    forced = (blk == 0) | (blk == jnp.right_shift(lane_q, SEL_SHIFT))
    future = blk * SEL_BLOCK > lane_q
    penalty = []
    for g in range(ng):
        v = jnp.where(future, NEG, jnp.where(forced, FORCE_SCORE, imp[g]))
        unselected = jnp.full((n_sel, tq), NEG, F32)
        for _ in range(min(SEL_TOP, n_sel)):
            mx = jnp.max(v, axis=0, keepdims=True)
            first = jnp.min(jnp.where(v == mx, blk_f, float(n_sel)), axis=0, keepdims=True)
            pick = blk_f == first
            unselected = jnp.where(pick, 0.0, unselected)
            v = jnp.where(pick, -jnp.inf, v)
        penalty.append(tile4(unselected.astype(BF16)))

    rel = lax.broadcasted_iota(jnp.int32, (n_win, tq), 0) - lax.broadcasted_iota(jnp.int32, (n_win, tq), 1)
    off = q0 - w0
    mask_w = (rel <= off) & (rel > off - WINDOW)
    o_win = []
    p_win = []
    for g in range(ng):
        _, l, p = _softmax_cols(masked(s_win[g], mask_w))
        p_win.append((l, p.astype(BF16)))
    for g in range(ng):
        l, p = p_win[g]
        o_win.append(_dot(vw_t_ref[gsl(g), pl.ds(w0, n_win)], p) / l)

    partial = []
    for g in range(ng):
        for r in range(NSA_GROUP):
            hd = g * NSA_GROUP + r
            cs = slice(r * tq, (r + 1) * tq)
            partial.append(gate_t_ref[0, 3 * hd:3 * hd + 1, :] * o_cmp[g][:, cs]
                           + gate_t_ref[0, 3 * hd + 2:3 * hd + 3, :] * o_win[g][:, cs])

    qa_sel = [q_aug(g, penalty[g]) for g in range(ng)]
    rel_d = lax.broadcasted_iota(jnp.int32, (ck, tq), 0) - lax.broadcasted_iota(jnp.int32, (ck, tq), 1)

    def select_and_store(n_before):
        nk = n_before + ck
        mask_d = rel_d <= q0 - n_before
        ss = [_dot(jnp.concatenate([ks_ref[0, 0:nk, gsl(g)], kfeat_ref[0:nk, :]], axis=1), qa_sel[g])
              for g in range(ng)]
        for g in range(ng):
            s_diag = masked(ss[g][n_before:nk], mask_d)
            m = jnp.max(s_diag, axis=0, keepdims=True)
            if n_before:
                m = jnp.maximum(m, jnp.max(ss[g][0:n_before], axis=0, keepdims=True))
            p_diag = jnp.exp(s_diag - m)
            l = jnp.sum(p_diag, axis=0, keepdims=True)
            p = p_diag.astype(BF16)
            if n_before:
                p_before = jnp.exp(ss[g][0:n_before] - m)
                l = l + jnp.sum(p_before, axis=0, keepdims=True)
                p = jnp.concatenate([p_before.astype(BF16), p], axis=0)
            o_sel = _dot(vs_t_ref[gsl(g), 0:nk], p) / l
            for r in range(NSA_GROUP):
                hd = g * NSA_GROUP + r
                mixed = partial[hd] + gate_t_ref[0, 3 * hd + 1:3 * hd + 2, :] * o_sel[:, r * tq:(r + 1) * tq]
                o_ref[0, :, hd * HEAD_DIM:(hd + 1) * HEAD_DIM] = mixed.T

    per = ck // tq
    for v in range(ks_ref.shape[1] // ck):
        pl.when(qi // per == v)(functools.partial(select_and_store, v * ck))


def _nsa(q_t, kc, vc_t, ks, vs_t, kw, vw_t, gate_t, *, tq=128, ck=256):
    b, s, _ = ks.shape
    nq = s // tq
    pos = np.arange(s)
    kfeat = np.zeros((s, LANES), np.float32)
    kfeat[:, 0] = POS_SPLIT * (pos // POS_SPLIT)
    kfeat[:, 1] = pos % POS_SPLIT
    kfeat[pos, FEAT_ROWS + pos // SEL_BLOCK] = 1.0
    ncmp = kc.shape[2]
    cfeat = np.zeros((ncmp, LANES), np.float32)
    cfeat[:, 0] = CMP_STRIDE * np.arange(ncmp)
    cfeat[:, 1] = 0.5 * (CMP_BLOCK - 1)
    slope = np.zeros((NSA_KV_HEADS, FEAT_ROWS, NSA_GROUP * tq), np.float32)
    for hd in range(NSA_HEADS):
        g, r = divmod(hd, NSA_GROUP)
        slope[g, 0:2, r * tq:(r + 1) * tq] = 2.0 ** (-8.0 * (hd + 1) / NSA_HEADS)
    assert FEAT_ROWS + s // SEL_BLOCK <= LANES and s // SEL_BLOCK % FEAT_ROWS == 0 and s % ck == 0 and ck % tq == 0

    def full(a):
        return pl.BlockSpec(a.shape, lambda i, j: (0,) * a.ndim)

    def per_b(a):
        return pl.BlockSpec((1,) + a.shape[1:], lambda i, j: (i,) + (0,) * (a.ndim - 1))

    def per_b_cols(a):
        return pl.BlockSpec((a.shape[0], s), lambda i, j: (0, i))

    def q_cols(a):
        assert a.shape[0] == b * nq and a.shape[2] == tq
        return pl.BlockSpec((1, a.shape[1], tq), lambda i, j: (i * nq + j, 0, 0))

    consts = [jnp.asarray(kfeat, BF16), jnp.asarray(cfeat, BF16), jnp.asarray(slope, BF16)]
    return pl.pallas_call(
        functools.partial(_nsa_kernel, tq=tq, ck=ck),
        grid=(b, nq),
        in_specs=[q_cols(q_t), per_b(kc), per_b(vc_t), per_b(ks), per_b_cols(vs_t), per_b(kw), per_b_cols(vw_t),
                  q_cols(gate_t)] + [full(c) for c in consts],
        out_specs=pl.BlockSpec((1, tq, NSA_WIDTH), lambda i, j: (i, j, 0)),
        out_shape=jax.ShapeDtypeStruct((b, s, NSA_WIDTH), F32),
        compiler_params=_cparams(("parallel", "arbitrary")),
        name="nsa",
    )(q_t, kc, vc_t, ks, vs_t, kw, vw_t, gate_t, *consts)


def _mla_prep_kernel(cq_ref, ckv_ref, kr_ref, cos_ref, sin_ref, cos_t_ref, sin_t_ref, wq_t_ref, wqr_t_ref, wk_ref,
                     wv_t_ref, qgn_ref, qgr_ref, qgt_ref, kgn_ref, kgr_ref, kgt_ref, q_t_ref, k_ref, v_t_ref):
    cq = cq_ref[0]
    ckv = ckv_ref[0]
    k_rope = kr_ref[0, :, 0:LANES]
    k_rot = kr_ref[0, :, LANES:2 * LANES]
    k_rope_ssq = jnp.sum(k_rope * k_rope, axis=-1, keepdims=True)
    k_rope_emb = k_rope * kgr_ref[...] * cos_ref[...] + k_rot * kgt_ref[...] * sin_ref[...]
    q_cos = qgr_ref[...] * cos_t_ref[...]
    q_sin = qgt_ref[...] * sin_t_ref[...]
    qscale = MLA_QK_DIM ** -0.5
    qx_all = _dot_nt(wq_t_ref[...], cq)
    q_rot_all = _dot_nt(wqr_t_ref[...], cq)
    k_nope_all = _dot(ckv, wk_ref[...])
    v_all = _dot_nt(wv_t_ref[...], ckv)
    for hd in range(MLA_HEADS):
        q_nope = qx_all[hd * MLA_QK_PAD:hd * MLA_QK_PAD + LANES]
        q_rope = qx_all[hd * MLA_QK_PAD + LANES:(hd + 1) * MLA_QK_PAD]
        q_rot = q_rot_all[hd * LANES:(hd + 1) * LANES]
        ssq = jnp.sum(q_nope * q_nope, axis=0, keepdims=True) + jnp.sum(q_rope * q_rope, axis=0, keepdims=True)
        inv = lax.rsqrt(ssq / MLA_QK_DIM + EPS) * qscale
        q_t_ref[0, 0, hd, 0:LANES, :] = (q_nope * inv * qgn_ref[...]).astype(BF16)
        q_t_ref[0, 0, hd, LANES:2 * LANES, :] = ((q_rope * q_cos + q_rot * q_sin) * inv).astype(BF16)

        k_nope = k_nope_all[:, hd * LANES:(hd + 1) * LANES]
        ssq = jnp.sum(k_nope * k_nope, axis=-1, keepdims=True) + k_rope_ssq
        inv = lax.rsqrt(ssq / MLA_QK_DIM + EPS)
        k_ref[0, hd, :, 0:LANES] = (k_nope * inv * kgn_ref[...]).astype(BF16)
        k_ref[0, hd, :, LANES:2 * LANES] = (k_rope_emb * inv).astype(BF16)
        v_t_ref[0, hd] = v_all[hd * LANES:(hd + 1) * LANES].astype(BF16)


def _mla_prep(cq, ckv, kr, tables, weights, q_gains, k_gains, *, tm=256):
    b, s, _ = cq.shape
    cos, sin = tables
    q_gains = tuple(jnp.broadcast_to(g.reshape(LANES, 1), (LANES, tm)) for g in q_gains)

    def full(a):
        return pl.BlockSpec(a.shape, lambda i, j: (0,) * a.ndim)

    def rows(width):
        return pl.BlockSpec((1, tm, width), lambda i, j: (i, j, 0))

    pos_rows = pl.BlockSpec((tm, LANES), lambda i, j: (j, 0))
    pos_cols = pl.BlockSpec((LANES, tm), lambda i, j: (0, j))
    return pl.pallas_call(
        _mla_prep_kernel,
        grid=(b, s // tm),
        in_specs=[rows(MLA_Q_RANK), rows(MLA_KV_RANK), rows(2 * LANES), pos_rows, pos_rows, pos_cols, pos_cols]
                 + [full(w) for w in weights] + [full(g) for g in q_gains + k_gains],
        out_specs=[pl.BlockSpec((1, 1, MLA_HEADS, MLA_QK_PAD, tm), lambda i, j: (i, j, 0, 0, 0)),
                   pl.BlockSpec((1, MLA_HEADS, tm, MLA_QK_PAD), lambda i, j: (i, 0, j, 0)),
                   pl.BlockSpec((1, MLA_HEADS, MLA_V_DIM, tm), lambda i, j: (i, 0, 0, j))],
        out_shape=[jax.ShapeDtypeStruct((b, s // tm, MLA_HEADS, MLA_QK_PAD, tm), BF16),
                   jax.ShapeDtypeStruct((b, MLA_HEADS, s, MLA_QK_PAD), BF16),
                   jax.ShapeDtypeStruct((b, MLA_HEADS, MLA_V_DIM, s), BF16)],
        compiler_params=_cparams(("parallel", "parallel")),
        name="mla_prep",
    )(cq, ckv, kr, cos, sin, cos.T, sin.T, *weights, *q_gains, *k_gains)


def _mla_attn_kernel(q_t_ref, k_ref, v_t_ref, o_ref, *, tq, hb, n_q):
    qi = pl.program_id(2)
    d_v = v_t_ref.shape[2]
    tri = lax.broadcasted_iota(jnp.int32, (tq, tq), 0) <= lax.broadcasted_iota(jnp.int32, (tq, tq), 1)

    def attend(n_before):
        nk = n_before + tq
        ss = [_dot(k_ref[0, h, 0:nk, :], q_t_ref[0, 0, h]) for h in range(hb)]
        for h in range(hb):
            s_diag = jnp.where(tri, ss[h][n_before:nk], NEG)
            m = jnp.max(s_diag, axis=0, keepdims=True)
            if n_before:
                m = jnp.maximum(m, jnp.max(ss[h][0:n_before], axis=0, keepdims=True))
            p_diag = jnp.exp(s_diag - m)
            l = jnp.sum(p_diag, axis=0, keepdims=True)
            p = p_diag.astype(BF16)
            if n_before:
                p_before = jnp.exp(ss[h][0:n_before] - m)
                l = l + jnp.sum(p_before, axis=0, keepdims=True)
                p = jnp.concatenate([p_before.astype(BF16), p], axis=0)
            o_ref[0, :, h * d_v:(h + 1) * d_v] = (_dot(v_t_ref[0, h, :, 0:nk], p) / l).T

    for v in range(n_q):
        pl.when(qi == v)(functools.partial(attend, v * tq))


def _mla_attn(q_t, k, v_t, *, tq=256, hb=4):
    b, h, s, dq = k.shape
    dv = v_t.shape[2]
    assert q_t.shape == (b, s // tq, h, dq, tq)
    return pl.pallas_call(
        functools.partial(_mla_attn_kernel, tq=tq, hb=hb, n_q=s // tq),
        grid=(b, h // hb, s // tq),
        in_specs=[pl.BlockSpec((1, 1, hb, dq, tq), lambda i, j, l: (i, l, j, 0, 0)),
                  pl.BlockSpec((1, hb, s, dq), lambda i, j, l: (i, j, 0, 0)),
                  pl.BlockSpec((1, hb, dv, s), lambda i, j, l: (i, j, 0, 0))],
        out_specs=pl.BlockSpec((1, tq, hb * dv), lambda i, j, l: (i, l, j)),
        out_shape=jax.ShapeDtypeStruct((b, s, h * dv), F32),
        compiler_params=_cparams(("parallel", "parallel", "arbitrary")),
        name="mla_attn",
    )(q_t, k, v_t)


def _out_kernel(xa_ref, xb_ref, oa_ref, ob_ref, ga_ref, gb_ref, w_ref, o_ref, *, nb_first):
    oa = _rms(oa_ref[...], ga_ref[...]).astype(BF16)
    ob = _rms(ob_ref[...], gb_ref[...]).astype(BF16)
    o_ref[...] = (_two_part_rows(xa_ref, xb_ref, nb_first)
                  + _dot(jnp.concatenate([oa, ob], axis=1), w_ref[...]))


def _out_proj(x_parts, oa, ob, gain_a, gain_b, w, *, tm=512):
    t, d = oa.shape[0], x_parts[0].shape[1]
    nb_first, x_specs = _two_part_specs(x_parts, tm)

    def rows(width):
        return pl.BlockSpec((tm, width), lambda i: (i, 0))

    def full(a):
        return pl.BlockSpec(a.shape, lambda i: (0,) * a.ndim)

    return pl.pallas_call(
        functools.partial(_out_kernel, nb_first=nb_first),
        grid=(t // tm,),
        in_specs=x_specs + [rows(oa.shape[1]), rows(ob.shape[1]), full(gain_a), full(gain_b), full(w)],
        out_specs=rows(d),
        out_shape=jax.ShapeDtypeStruct((t, d), F32),
        compiler_params=_cparams(("parallel",)),
        name="out_proj",
    )(*x_parts, oa, ob, gain_a, gain_b, w)


def _rot_cols(w):
    half = w.shape[1] // 2
    return jnp.concatenate([-w[:, half:], w[:, :half]], axis=1)


def _layout_w_in(w_in):
    d = w_in.shape[0]
    off = np.cumsum([0, NSA_WIDTH] + [NSA_KV_WIDTH] * 6 + [NSA_HEADS * 3, MLA_Q_RANK, MLA_KV_RANK, MLA_ROPE_DIM])
    q, kc, vc, ks, vs, kw, vw, gt, cq, ckv, kr = [w_in[:, off[i]:off[i + 1]] for i in range(11)]
    zr = jnp.zeros((d, LANES - MLA_ROPE_DIM), w_in.dtype)
    zg = jnp.zeros((d, LANES - NSA_HEADS * 3), w_in.dtype)
    w_n = jnp.concatenate([kc, vc, ks, kw, cq, ckv, kr, zr, _rot_cols(kr), zr], axis=1).astype(BF16)
    w_t = jnp.concatenate([q, vs, vw, gt, zg], axis=1).astype(BF16).T
    return w_n, w_t


def _layout_mla_weights(w_uq, w_ukv):
    r = w_uq.shape[0]
    zr = jnp.zeros((r, LANES - MLA_ROPE_DIM), w_uq.dtype)
    main, rot, k_cols, v_cols = [], [], [], []
    for hd in range(MLA_HEADS):
        base = hd * MLA_QK_DIM
        rope = w_uq[:, base + MLA_NOPE_DIM:base + MLA_QK_DIM]
        main += [w_uq[:, base:base + MLA_NOPE_DIM], rope, zr]
        rot += [_rot_cols(rope), zr]
        kv = hd * (MLA_NOPE_DIM + MLA_V_DIM)
        k_cols.append(w_ukv[:, kv:kv + MLA_NOPE_DIM])
        v_cols.append(w_ukv[:, kv + MLA_NOPE_DIM:kv + MLA_NOPE_DIM + MLA_V_DIM])
    cat = lambda parts: jnp.concatenate(parts, axis=1).astype(BF16)
    return cat(main).T, cat(rot).T, cat(k_cols), cat(v_cols).T


def _rope_gains(gain):
    half = MLA_ROPE_DIM // 2
    zr = jnp.zeros((LANES - MLA_ROPE_DIM,), gain.dtype)
    g_rope = gain[MLA_NOPE_DIM:]
    return (gain[None, :MLA_NOPE_DIM], jnp.concatenate([g_rope, zr])[None, :],
            jnp.concatenate([g_rope[half:], g_rope[:half], zr])[None, :])


def _rope_tables(s):
    half = MLA_ROPE_DIM // 2
    inv = ROPE_THETA ** (-jnp.arange(half, dtype=F32) / half)
    ang = jnp.arange(s).astype(F32)[:, None] * inv[None, :]
    zr = jnp.zeros((s, LANES - MLA_ROPE_DIM), F32)
    cos, sin = jnp.cos(ang), jnp.sin(ang)
    return jnp.concatenate([cos, cos, zr], axis=1), jnp.concatenate([sin, sin, zr], axis=1)


def kernel(x, ffn1_norm, ffn1_w_gate, ffn1_w_up, ffn1_w_down, mix_norm, w_in, nsa_q_norm, nsa_k_norm, nsa_cmp_pos_k, nsa_cmp_w1_k, nsa_cmp_w2_k, nsa_cmp_pos_v, nsa_cmp_w1_v, nsa_cmp_w2_v, mla_q_a_norm, mla_w_uq, mla_kv_a_norm, mla_w_ukv, mla_q_norm, mla_k_norm, out_norm_nsa, out_norm_mla, w_out, ffn2_norm, ffn2_w_gate, ffn2_w_up, ffn2_w_down):
    b, s, d = x.shape
    depth = ffn1_norm.shape[0]
    tables = _rope_tables(s)
    xt = x.reshape(b * s, d)
    for l in range(depth):
        x_parts = _ffn(xt, ffn1_norm[l][None, :], ffn1_w_gate[l], ffn1_w_up[l], ffn1_w_down[l], in_place=False)

        w_n, w_t = _layout_w_in(w_in[l])
        q_t, kv_cmp, ks, vs_t, kw, vw_t, gate_t, cq, ckv, kr = _proj(
            x_parts, mix_norm[l][None, :], w_n, w_t, nsa_q_norm[l], nsa_k_norm[l],
            mla_q_a_norm[l][None, :], mla_kv_a_norm[l][None, :])

        kc, vc_t = _compress(
            kv_cmp, s,
            nsa_cmp_w1_k[l].astype(BF16), nsa_cmp_w2_k[l].astype(BF16), nsa_cmp_pos_k[l].reshape(1, -1),
            nsa_cmp_w1_v[l].astype(BF16), nsa_cmp_w2_v[l].astype(BF16).T, nsa_cmp_pos_v[l].reshape(1, -1),
            nsa_k_norm[l])

        def bs(a):
            return a.reshape(b, s, a.shape[-1])

        o_a = _nsa(q_t, kc, vc_t, bs(ks), vs_t, bs(kw), vw_t, gate_t)

        q_m, k_m, v_m = _mla_prep(
            bs(cq), bs(ckv), bs(kr), tables, _layout_mla_weights(mla_w_uq[l], mla_w_ukv[l]),
            _rope_gains(mla_q_norm[l]), _rope_gains(mla_k_norm[l]))
        o_b = _mla_attn(q_m, k_m, v_m)

        xt = _out_proj(x_parts, o_a.reshape(b * s, NSA_WIDTH), o_b.reshape(b * s, MLA_WIDTH),
                       out_norm_nsa[l][None, :], out_norm_mla[l][None, :], w_out[l].astype(BF16))

        xt = _ffn(xt, ffn2_norm[l][None, :], ffn2_w_gate[l], ffn2_w_up[l], ffn2_w_down[l], in_place=True)
    return xt.reshape(b, s, d)
```

```python
import functools

import numpy as np
import jax
import jax.numpy as jnp
from jax import lax
from jax.experimental import pallas as pl
from jax.experimental.pallas import tpu as pltpu

F32 = jnp.float32
BF16 = jnp.bfloat16

HEAD_DIM = 128
NSA_HEADS = 8
NSA_KV_HEADS = 2
NSA_GROUP = NSA_HEADS // NSA_KV_HEADS
CMP_BLOCK = 32
CMP_STRIDE = 16
CMP_HIDDEN = 256
SEL_BLOCK = 64
SEL_SHIFT = 6
SEL_TOP = 8
WINDOW = 512
MLA_HEADS = 8
MLA_Q_RANK = 384
MLA_KV_RANK = 256
MLA_NOPE_DIM = 128
MLA_ROPE_DIM = 64
MLA_V_DIM = 128
MLA_QK_DIM = MLA_NOPE_DIM + MLA_ROPE_DIM
ROPE_THETA = 10000.0
EPS = 1e-6
NEG = -1e30
FORCE_SCORE = 1e4
NSA_WIDTH = NSA_HEADS * HEAD_DIM
MLA_WIDTH = MLA_HEADS * MLA_V_DIM
NSA_KV_WIDTH = NSA_KV_HEADS * HEAD_DIM

LANES = 128
MLA_QK_PAD = 2 * LANES
VMEM_LIMIT = 58 * 1024 * 1024
POS_SPLIT = 16
FEAT_ROWS = 16
ONES_ROWS = 16


def _cparams(sem):
    return pltpu.CompilerParams(dimension_semantics=sem, vmem_limit_bytes=VMEM_LIMIT)


def _rms(x, gain):
    return x * lax.rsqrt(jnp.mean(x * x, axis=-1, keepdims=True) + EPS) * gain


def _dot(a, b):
    return jnp.dot(a, b, preferred_element_type=F32)


def _dot_nt(a, b):
    return lax.dot_general(a, b, (((1,), (1,)), ((), ())), preferred_element_type=F32)


def _sigmoid(x):
    return 1.0 / (1.0 + jnp.exp(-x))


def _exp_bf16(x):
    return jnp.exp(x.astype(BF16))


def _ffn_accumulate(x_ref, g_ref, weights, o_ref, h_ref):
    @pl.when(pl.program_id(1) == 0)
    def _():
        x = x_ref[...]
        h_ref[...] = _rms(x, g_ref[...]).astype(BF16)
        o_ref[...] = x

    wg, wu, wd = weights()
    h = h_ref[...]
    gate = _dot(h, wg)
    up = _dot(h, wu)
    act = gate * _sigmoid(gate) * up * 0.5
    o_ref[...] += _dot(act.astype(BF16), wd)


def _ffn_first_kernel(x_ref, g_ref, wg_ref, wu_ref, wd_ref, o_ref, wg_out, wu_out, wd_out, h_ref):
    def weights():
        wg, wu, wd = wg_ref[...].astype(BF16), wu_ref[...].astype(BF16), wd_ref[...].astype(BF16)
        wg_out[...], wu_out[...], wd_out[...] = wg, wu, wd
        return wg, wu, wd

    _ffn_accumulate(x_ref, g_ref, weights, o_ref, h_ref)


def _ffn_rest_kernel(x_ref, g_ref, wg_ref, wu_ref, wd_ref, o_ref, h_ref):
    _ffn_accumulate(x_ref, g_ref, lambda: (wg_ref[...], wu_ref[...], wd_ref[...]), o_ref, h_ref)


def _ffn(x, gain, wg, wu, wd, *, in_place, tm=1024, tf_first=256, tf=512):
    t, d = x.shape
    f = wg.shape[1]
    alias = {0: 0} if in_place else {}

    def w_specs(tfx):
        return [pl.BlockSpec((d, tfx), lambda i, j: (0, j)),
                pl.BlockSpec((d, tfx), lambda i, j: (0, j)),
                pl.BlockSpec((tfx, d), lambda i, j: (j, 0))]

    y, wg_b, wu_b, wd_b = pl.pallas_call(
        _ffn_first_kernel,
        grid=(1, f // tf_first),
        in_specs=[pl.BlockSpec((tm, d), lambda i, j: (0, 0), pipeline_mode=pl.Buffered(1)),
                  pl.BlockSpec((1, d), lambda i, j: (0, 0))] + w_specs(tf_first),
        out_specs=[pl.BlockSpec((tm, d), lambda i, j: (0, 0))] + w_specs(tf_first),
        out_shape=[jax.ShapeDtypeStruct((t if in_place else tm, d), F32), jax.ShapeDtypeStruct((d, f), BF16),
                   jax.ShapeDtypeStruct((d, f), BF16), jax.ShapeDtypeStruct((f, d), BF16)],
        scratch_shapes=[pltpu.VMEM((tm, d), BF16)],
        input_output_aliases=alias,
        compiler_params=_cparams(("arbitrary", "arbitrary")),
        name="ffn_first",
    )(x, gain, wg, wu, wd)

    out_row0 = 1 if in_place else 0
    rest = pl.pallas_call(
        _ffn_rest_kernel,
        grid=(t // tm - 1, f // tf),
        in_specs=[pl.BlockSpec((tm, d), lambda i, j: (i + 1, 0)),
                  pl.BlockSpec((1, d), lambda i, j: (0, 0))] + w_specs(tf),
        out_specs=pl.BlockSpec((tm, d), lambda i, j: (i + out_row0, 0)),
        out_shape=jax.ShapeDtypeStruct((t if in_place else t - tm, d), F32),
        scratch_shapes=[pltpu.VMEM((tm, d), BF16)],
        input_output_aliases=alias,
        compiler_params=_cparams(("parallel", "arbitrary")),
        name="ffn_rest",
    )(y if in_place else x, gain, wg_b, wu_b, wd_b)
    return rest if in_place else (y, rest)


_N_CMP = 0
_N_KS = _N_CMP + 2 * NSA_KV_WIDTH
_N_KW = _N_KS + NSA_KV_WIDTH
_N_CQ = _N_KW + NSA_KV_WIDTH
_N_CKV = _N_CQ + MLA_Q_RANK
_N_KR = _N_CKV + MLA_KV_RANK
_N_END = _N_KR + 2 * LANES
_T_Q = 0
_T_VS = _T_Q + NSA_WIDTH
_T_VW = _T_VS + NSA_KV_WIDTH
_T_GATE = _T_VW + NSA_KV_WIDTH
_T_END = _T_GATE + LANES


def _two_part_specs(parts, tm):
    first, rest = parts
    nb_first = first.shape[0] // tm
    d = first.shape[1]
    return nb_first, [pl.BlockSpec((tm, d), lambda i: (jnp.minimum(i, nb_first - 1), 0)),
                      pl.BlockSpec((tm, d), lambda i: (jnp.maximum(i - nb_first, 0), 0))]


def _two_part_rows(first_ref, rest_ref, nb_first):
    return jnp.where(pl.program_id(0) < nb_first, first_ref[...], rest_ref[...])


def _proj_kernel(xa_ref, xb_ref, g_ref, wn_ref, wt_ref, qg_ref, kg_ref, qag_ref, kvag_ref,
                 q_t_ref, cmp_ref, ks_ref, vs_t_ref, kw_ref, vw_t_ref, gate_t_ref, cq_ref, ckv_ref, kr_ref, *,
                 nb_first):
    tq = q_t_ref.shape[2]
    h = _rms(_two_part_rows(xa_ref, xb_ref, nb_first), g_ref[...]).astype(BF16)
    p_n = _dot(h, wn_ref[...])
    p_t = _dot_nt(wt_ref[...], h)

    def proj(lo, hi):
        return p_n[:, lo:hi]

    def proj_t(lo, hi):
        return p_t[lo:hi, :]

    qscale = HEAD_DIM ** -0.5
    for hd in range(NSA_HEADS):
        q = proj_t(_T_Q + hd * HEAD_DIM, _T_Q + (hd + 1) * HEAD_DIM)
        inv = lax.rsqrt(jnp.mean(q * q, axis=0, keepdims=True) + EPS) * qscale
        qn = (q * inv * qg_ref[...]).astype(BF16)
        for c in range(q_t_ref.shape[0]):
            q_t_ref[c, hd * HEAD_DIM:(hd + 1) * HEAD_DIM, :] = qn[:, c * tq:(c + 1) * tq]
    for c in range(cmp_ref.shape[0]):
        cmp_ref[c] = proj(_N_CMP + c * HEAD_DIM, _N_CMP + (c + 1) * HEAD_DIM)
    for g in range(NSA_KV_HEADS):
        sl = slice(g * HEAD_DIM, (g + 1) * HEAD_DIM)
        ks_ref[:, sl] = _rms(proj(_N_KS + g * HEAD_DIM, _N_KS + (g + 1) * HEAD_DIM), kg_ref[1:2, :]).astype(BF16)
        kw_ref[:, sl] = _rms(proj(_N_KW + g * HEAD_DIM, _N_KW + (g + 1) * HEAD_DIM), kg_ref[2:3, :]).astype(BF16)
    vs_t_ref[...] = proj_t(_T_VS, _T_VW).astype(BF16)
    vw_t_ref[...] = proj_t(_T_VW, _T_GATE).astype(BF16)
    cq_ref[...] = _rms(proj(_N_CQ, _N_CKV), qag_ref[...]).astype(BF16)
    ckv_ref[...] = _rms(proj(_N_CKV, _N_KR), kvag_ref[...]).astype(BF16)
    kr_ref[...] = proj(_N_KR, _N_END)
    gates = _sigmoid(proj_t(_T_GATE, _T_END))
    for c in range(gate_t_ref.shape[0]):
        gate_t_ref[c] = gates[:, c * tq:(c + 1) * tq]


def _proj(x_parts, gain, w_n, w_t, q_gain, k_gains, qa_gain, kva_gain, *, tm=512, tq=128):
    t = x_parts[0].shape[0] + x_parts[1].shape[0]
    nb_first, x_specs = _two_part_specs(x_parts, tm)
    q_gain = jnp.broadcast_to(q_gain.reshape(HEAD_DIM, 1), (HEAD_DIM, tm))

    def full(a):
        return pl.BlockSpec(a.shape, lambda i: (0,) * a.ndim)

    def rows(width):
        return (t, width), pl.BlockSpec((tm, width), lambda i: (i, 0))

    def cols(height):
        return (height, t), pl.BlockSpec((height, tm), lambda i: (0, i))

    def q_tiles(height):
        return (t // tq, height, tq), pl.BlockSpec((tm // tq, height, tq), lambda i: (i, 0, 0))

    n_cmp_cols = 2 * NSA_KV_WIDTH // HEAD_DIM
    cmp_out = (n_cmp_cols, t, HEAD_DIM), pl.BlockSpec((n_cmp_cols, tm, HEAD_DIM), lambda i: (0, i, 0))
    outs = [(q_tiles(NSA_WIDTH), BF16), (cmp_out, F32), (rows(NSA_KV_WIDTH), BF16),
            (cols(NSA_KV_WIDTH), BF16), (rows(NSA_KV_WIDTH), BF16), (cols(NSA_KV_WIDTH), BF16),
            (q_tiles(LANES), F32), (rows(MLA_Q_RANK), BF16), (rows(MLA_KV_RANK), BF16), (rows(2 * LANES), F32)]
    return pl.pallas_call(
        functools.partial(_proj_kernel, nb_first=nb_first),
        grid=(t // tm,),
        in_specs=x_specs + [full(gain), full(w_n), full(w_t), full(q_gain), full(k_gains), full(qa_gain),
                            full(kva_gain)],
        out_specs=[spec for (_, spec), _ in outs],
        out_shape=[jax.ShapeDtypeStruct(shape, dt) for (shape, _), dt in outs],
        compiler_params=_cparams(("parallel",)),
        name="proj",
    )(*x_parts, gain, w_n, w_t, q_gain, k_gains, qa_gain, kva_gain)


def _gelu_tanh(x):
    return 0.5 * x * (1.0 + jnp.tanh(np.sqrt(2.0 / np.pi) * (x + 0.044715 * (x * x * x))))


def _compress_kernel(x_ref, w1k_ref, w2k_ref, pk_ref, w1v_ref, w2v_ref, pv_ref, kg_ref, kc_ref, vc_t_ref):
    half = CMP_STRIDE * HEAD_DIM
    nrow = x_ref.shape[1] // CMP_STRIDE
    for which, (w1_ref, w2_ref, p_ref) in enumerate(((w1k_ref, w2k_ref, pk_ref), (w1v_ref, w2v_ref, pv_ref))):
        pos_bias = _dot(jnp.broadcast_to(p_ref[...], (8, 2 * half)).astype(BF16), w1_ref[...])[0:1, :]
        for g in range(NSA_KV_HEADS):
            col = which * NSA_KV_HEADS + g
            xg = jnp.concatenate(
                [x_ref[col, pl.ds(l, nrow, stride=CMP_STRIDE), :].astype(BF16) for l in range(CMP_STRIDE)], axis=1)
            first = _dot(xg, w1_ref[0:half, :])
            second = _dot(xg, w1_ref[half:2 * half, :])
            pre = first + pltpu.roll(second, nrow - 1, 0) + pos_bias
            hidden = _gelu_tanh(pre).astype(BF16)
            if which == 0:
                kc_ref[0, g] = _rms(_dot(hidden, w2_ref[...]), kg_ref[0:1, :]).astype(BF16)
            else:
                vc_t_ref[0, g] = _dot_nt(w2_ref[...], hidden).astype(BF16)


def _compress(xc, seq, w1k, w2k, pk, w1v, w2v, pv, k_gains):
    ncol, t, width = xc.shape
    b, nrow = t // seq, seq // CMP_STRIDE

    def full(a):
        return pl.BlockSpec(a.shape, lambda i: (0,) * a.ndim)

    out_sds = jax.ShapeDtypeStruct((b, NSA_KV_HEADS, nrow, HEAD_DIM), BF16)
    out_spec = pl.BlockSpec((1, NSA_KV_HEADS, nrow, HEAD_DIM), lambda i: (i, 0, 0, 0))
    return pl.pallas_call(
        _compress_kernel,
        grid=(b,),
        in_specs=[pl.BlockSpec((ncol, seq, width), lambda i: (0, i, 0)),
                  full(w1k), full(w2k), full(pk), full(w1v), full(w2v), full(pv), full(k_gains)],
        out_specs=[out_spec, out_spec],
        out_shape=[out_sds, out_sds],
        compiler_params=_cparams(("parallel",)),
        name="compress",
    )(xc, w1k, w2k, pk, w1v, w2v, pv, k_gains)


def _softmax_cols(s):
    m = jnp.max(s, axis=0, keepdims=True)
    p = jnp.exp(s - m)
    return m, jnp.sum(p, axis=0, keepdims=True), p


def _nsa_kernel(q_t_ref, kc_ref, vc_t_ref, ks_ref, vs_t_ref, kw_ref, vw_t_ref, gate_t_ref, kfeat_ref, cfeat_ref,
                slope_ref, o_ref, *, tq, ck):
    qi = pl.program_id(1)
    q0 = qi * tq
    ng = NSA_KV_HEADS
    wide = NSA_GROUP * tq
    n_sel = ks_ref.shape[1] // SEL_BLOCK
    ncmp = kc_ref.shape[2]
    n_win = WINDOW + tq

    def tile4(a):
        return jnp.concatenate([a] * NSA_GROUP, axis=1)

    def masked(s, mask, fill=NEG):
        return jnp.concatenate([jnp.where(mask, s[:, r * tq:(r + 1) * tq], fill) for r in range(NSA_GROUP)], axis=1)

    def q_aug(g, extra):
        q = jnp.concatenate([q_t_ref[0, (g * NSA_GROUP + r) * HEAD_DIM:(g * NSA_GROUP + r + 1) * HEAD_DIM, :]
                             for r in range(NSA_GROUP)], axis=1)
        rows = [q, slope_ref[g]]
        used = slope_ref.shape[1]
        if extra is not None:
            rows.append(extra)
            used += extra.shape[0]
        rows.append(jnp.zeros((HEAD_DIM - used, wide), BF16))
        return jnp.concatenate(rows, axis=0)

    def gsl(g):
        return slice(g * HEAD_DIM, (g + 1) * HEAD_DIM)

    lane_q = q0 + lax.broadcasted_iota(jnp.int32, (1, tq), 1)

    qa_plain = [q_aug(g, None) for g in range(ng)]
    w0 = pl.multiple_of(jnp.maximum(q0 - WINDOW, 0), tq)
    s_cmp = [_dot(jnp.concatenate([kc_ref[0, g], cfeat_ref[...]], axis=1), qa_plain[g]) for g in range(ng)]
    s_win = [_dot(jnp.concatenate([kw_ref[0, pl.ds(w0, n_win), gsl(g)], kfeat_ref[pl.ds(w0, n_win), :]], axis=1),
                  qa_plain[g]) for g in range(ng)]

    n_row = lax.broadcasted_iota(jnp.int32, (ncmp, tq), 0)
    mask_c = n_row * CMP_STRIDE + (CMP_BLOCK - 1) <= lane_q
    s_row = lax.broadcasted_iota(jnp.int32, (n_sel, ncmp), 0)
    n_col = lax.broadcasted_iota(jnp.int32, (n_sel, ncmp), 1)
    overlap_t = jnp.where((n_col * CMP_STRIDE < (s_row + 1) * SEL_BLOCK)
                          & (n_col * CMP_STRIDE + (CMP_BLOCK - 1) >= s_row * SEL_BLOCK), 1.0, 0.0).astype(BF16)
    p_cmp, imp = [], []
    for g in range(ng):
        s = masked(s_cmp[g], mask_c)
        m = jnp.max(s, axis=0, keepdims=True)
        p = masked(jnp.exp(s - m), mask_c, 0.0)
        l = jnp.sum(p, axis=0, keepdims=True)
        p = p / jnp.where(l > 0.0, l, 1.0)
        p_cmp.append(p.astype(BF16))
        p_sum = p[:, 0:tq]
        for r in range(1, NSA_GROUP):
            p_sum = p_sum + p[:, r * tq:(r + 1) * tq]
        p_hi = p_sum.astype(BF16)
        p_lo = (p_sum - p_hi.astype(F32)).astype(BF16)
        imp.append(_dot(overlap_t, p_hi) + _dot(overlap_t, p_lo))
    o_cmp = [_dot(vc_t_ref[0, g], p_cmp[g]) for g in range(ng)]

    blk = lax.broadcasted_iota(jnp.int32, (n_sel, tq), 0)
    blk_f = blk.astype(F32)
    forced = (blk == 0) | (blk == jnp.right_shift(lane_q, SEL_SHIFT))
    future = blk * SEL_BLOCK > lane_q
    penalty = []
    for g in range(ng):
        v = jnp.where(future, NEG, jnp.where(forced, FORCE_SCORE, imp[g]))
        unselected = jnp.full((n_sel, tq), NEG, F32)
        for _ in range(min(SEL_TOP, n_sel)):
            mx = jnp.max(v, axis=0, keepdims=True)
            first = jnp.min(jnp.where(v == mx, blk_f, float(n_sel)), axis=0, keepdims=True)
            pick = blk_f == first
            unselected = jnp.where(pick, 0.0, unselected)
            v = jnp.where(pick, -jnp.inf, v)
        penalty.append(tile4(unselected.astype(BF16)))

    rel = lax.broadcasted_iota(jnp.int32, (n_win, tq), 0) - lax.broadcasted_iota(jnp.int32, (n_win, tq), 1)
    off = q0 - w0
    mask_w = (rel <= off) & (rel > off - WINDOW)
    o_win = []
    p_win = []
    for g in range(ng):
        _, l, p = _softmax_cols(masked(s_win[g], mask_w))
        p_win.append((l, p.astype(BF16)))
    for g in range(ng):
        l, p = p_win[g]
        o_win.append(_dot(vw_t_ref[gsl(g), pl.ds(w0, n_win)], p) / l)

    partial = []
    for g in range(ng):
        for r in range(NSA_GROUP):
            hd = g * NSA_GROUP + r
            cs = slice(r * tq, (r + 1) * tq)
            partial.append(gate_t_ref[0, 3 * hd:3 * hd + 1, :] * o_cmp[g][:, cs]
                           + gate_t_ref[0, 3 * hd + 2:3 * hd + 3, :] * o_win[g][:, cs])

    qa_sel = [q_aug(g, penalty[g]) for g in range(ng)]
    rel_d = lax.broadcasted_iota(jnp.int32, (ck, tq), 0) - lax.broadcasted_iota(jnp.int32, (ck, tq), 1)

    def select_and_store(n_before):
        nk = n_before + ck
        mask_d = rel_d <= q0 - n_before
        ss = [_dot(jnp.concatenate([ks_ref[0, 0:nk, gsl(g)], kfeat_ref[0:nk, :]], axis=1), qa_sel[g])
              for g in range(ng)]
        for g in range(ng):
            s_diag = masked(ss[g][n_before:nk], mask_d)
            m = jnp.max(s_diag, axis=0, keepdims=True)
            if n_before:
                m = jnp.maximum(m, jnp.max(ss[g][0:n_before], axis=0, keepdims=True))
            p_diag = jnp.exp(s_diag - m)
            l = jnp.sum(p_diag, axis=0, keepdims=True)
            p = p_diag.astype(BF16)
            if n_before:
                p_before = jnp.exp(ss[g][0:n_before] - m)
                l = l + jnp.sum(p_before, axis=0, keepdims=True)
                p = jnp.concatenate([p_before.astype(BF16), p], axis=0)
            o_sel = _dot(vs_t_ref[gsl(g), 0:nk], p) / l
            for r in range(NSA_GROUP):
                hd = g * NSA_GROUP + r
                mixed = partial[hd] + gate_t_ref[0, 3 * hd + 1:3 * hd + 2, :] * o_sel[:, r * tq:(r + 1) * tq]
                o_ref[0, :, hd * HEAD_DIM:(hd + 1) * HEAD_DIM] = mixed.T

    per = ck // tq
    for v in range(ks_ref.shape[1] // ck):
        pl.when(qi // per == v)(functools.partial(select_and_store, v * ck))


def _nsa(q_t, kc, vc_t, ks, vs_t, kw, vw_t, gate_t, *, tq=128, ck=256):
    b, s, _ = ks.shape
    nq = s // tq
    pos = np.arange(s)
    kfeat = np.zeros((s, LANES), np.float32)
    kfeat[:, 0] = POS_SPLIT * (pos // POS_SPLIT)
    kfeat[:, 1] = pos % POS_SPLIT
    kfeat[pos, FEAT_ROWS + pos // SEL_BLOCK] = 1.0
    ncmp = kc.shape[2]
    cfeat = np.zeros((ncmp, LANES), np.float32)
    cfeat[:, 0] = CMP_STRIDE * np.arange(ncmp)
    cfeat[:, 1] = 0.5 * (CMP_BLOCK - 1)
    slope = np.zeros((NSA_KV_HEADS, FEAT_ROWS, NSA_GROUP * tq), np.float32)
    for hd in range(NSA_HEADS):
        g, r = divmod(hd, NSA_GROUP)
        slope[g, 0:2, r * tq:(r + 1) * tq] = 2.0 ** (-8.0 * (hd + 1) / NSA_HEADS)
    assert FEAT_ROWS + s // SEL_BLOCK <= LANES and s // SEL_BLOCK % FEAT_ROWS == 0 and s % ck == 0 and ck % tq == 0

    def full(a):
        return pl.BlockSpec(a.shape, lambda i, j: (0,) * a.ndim)

    def per_b(a):
        return pl.BlockSpec((1,) + a.shape[1:], lambda i, j: (i,) + (0,) * (a.ndim - 1))

    def per_b_cols(a):
        return pl.BlockSpec((a.shape[0], s), lambda i, j: (0, i))

    def q_cols(a):
        assert a.shape[0] == b * nq and a.shape[2] == tq
        return pl.BlockSpec((1, a.shape[1], tq), lambda i, j: (i * nq + j, 0, 0))

    consts = [jnp.asarray(kfeat, BF16), jnp.asarray(cfeat, BF16), jnp.asarray(slope, BF16)]
    return pl.pallas_call(
        functools.partial(_nsa_kernel, tq=tq, ck=ck),
        grid=(b, nq),
        in_specs=[q_cols(q_t), per_b(kc), per_b(vc_t), per_b(ks), per_b_cols(vs_t), per_b(kw), per_b_cols(vw_t),
                  q_cols(gate_t)] + [full(c) for c in consts],
        out_specs=pl.BlockSpec((1, tq, NSA_WIDTH), lambda i, j: (i, j, 0)),
        out_shape=jax.ShapeDtypeStruct((b, s, NSA_WIDTH), F32),
        compiler_params=_cparams(("parallel", "arbitrary")),
        name="nsa",
    )(q_t, kc, vc_t, ks, vs_t, kw, vw_t, gate_t, *consts)


def _mla_prep_kernel(cq_ref, ckv_ref, kr_ref, cos_ref, sin_ref, cos_t_ref, sin_t_ref, wq_t_ref, wqr_t_ref, wk_ref,
                     wv_t_ref, qgn_ref, qgr_ref, qgt_ref, kgn_ref, kgr_ref, kgt_ref, q_t_ref, k_ref, v_t_ref):
    cq = cq_ref[0]
    ckv = ckv_ref[0]
    k_rope = kr_ref[0, :, 0:LANES]
    k_rot = kr_ref[0, :, LANES:2 * LANES]
    k_rope_ssq = jnp.sum(k_rope * k_rope, axis=-1, keepdims=True)
    k_rope_emb = k_rope * kgr_ref[...] * cos_ref[...] + k_rot * kgt_ref[...] * sin_ref[...]
    q_cos = qgr_ref[...] * cos_t_ref[...]
    q_sin = qgt_ref[...] * sin_t_ref[...]
    qscale = MLA_QK_DIM ** -0.5
    qx_all = _dot_nt(wq_t_ref[...], cq)
    q_rot_all = _dot_nt(wqr_t_ref[...], cq)
    k_nope_all = _dot(ckv, wk_ref[...])
    v_all = _dot_nt(wv_t_ref[...], ckv)
    for hd in range(MLA_HEADS):
        q_nope = qx_all[hd * MLA_QK_PAD:hd * MLA_QK_PAD + LANES]
        q_rope = qx_all[hd * MLA_QK_PAD + LANES:(hd + 1) * MLA_QK_PAD]
        q_rot = q_rot_all[hd * LANES:(hd + 1) * LANES]
        ssq = jnp.sum(q_nope * q_nope, axis=0, keepdims=True) + jnp.sum(q_rope * q_rope, axis=0, keepdims=True)
        inv = lax.rsqrt(ssq / MLA_QK_DIM + EPS) * qscale
        q_t_ref[0, 0, hd, 0:LANES, :] = (q_nope * inv * qgn_ref[...]).astype(BF16)
        q_t_ref[0, 0, hd, LANES:2 * LANES, :] = ((q_rope * q_cos + q_rot * q_sin) * inv).astype(BF16)

        k_nope = k_nope_all[:, hd * LANES:(hd + 1) * LANES]
        ssq = jnp.sum(k_nope * k_nope, axis=-1, keepdims=True) + k_rope_ssq
        inv = lax.rsqrt(ssq / MLA_QK_DIM + EPS)
        k_ref[0, hd, :, 0:LANES] = (k_nope * inv * kgn_ref[...]).astype(BF16)
        k_ref[0, hd, :, LANES:2 * LANES] = (k_rope_emb * inv).astype(BF16)
        v_t_ref[0, hd, 0:MLA_V_DIM, :] = v_all[hd * LANES:(hd + 1) * LANES].astype(BF16)
        v_t_ref[0, hd, MLA_V_DIM:MLA_V_DIM + ONES_ROWS, :] = jnp.ones((ONES_ROWS, v_all.shape[1]), BF16)


def _mla_prep(cq, ckv, kr, tables, weights, q_gains, k_gains, *, tm=256):
    b, s, _ = cq.shape
    cos, sin = tables
    q_gains = tuple(jnp.broadcast_to(g.reshape(LANES, 1), (LANES, tm)) for g in q_gains)

    def full(a):
        return pl.BlockSpec(a.shape, lambda i, j: (0,) * a.ndim)

    def rows(width):
        return pl.BlockSpec((1, tm, width), lambda i, j: (i, j, 0))

    pos_rows = pl.BlockSpec((tm, LANES), lambda i, j: (j, 0))
    pos_cols = pl.BlockSpec((LANES, tm), lambda i, j: (0, j))
    return pl.pallas_call(
        _mla_prep_kernel,
        grid=(b, s // tm),
        in_specs=[rows(MLA_Q_RANK), rows(MLA_KV_RANK), rows(2 * LANES), pos_rows, pos_rows, pos_cols, pos_cols]
                 + [full(w) for w in weights] + [full(g) for g in q_gains + k_gains],
        out_specs=[pl.BlockSpec((1, 1, MLA_HEADS, MLA_QK_PAD, tm), lambda i, j: (i, j, 0, 0, 0)),
                   pl.BlockSpec((1, MLA_HEADS, tm, MLA_QK_PAD), lambda i, j: (i, 0, j, 0)),
                   pl.BlockSpec((1, MLA_HEADS, MLA_V_DIM + ONES_ROWS, tm), lambda i, j: (i, 0, 0, j))],
        out_shape=[jax.ShapeDtypeStruct((b, s // tm, MLA_HEADS, MLA_QK_PAD, tm), BF16),
                   jax.ShapeDtypeStruct((b, MLA_HEADS, s, MLA_QK_PAD), BF16),
                   jax.ShapeDtypeStruct((b, MLA_HEADS, MLA_V_DIM + ONES_ROWS, s), BF16)],
        compiler_params=_cparams(("parallel", "parallel")),
        name="mla_prep",
    )(cq, ckv, kr, cos, sin, cos.T, sin.T, *weights, *q_gains, *k_gains)


def _mla_attn_kernel(q_t_ref, k_ref, v_t_ref, o_ref, *, tq, hb, n_q):
    qi = pl.program_id(2)
    d_v = v_t_ref.shape[2] - ONES_ROWS
    tri = lax.broadcasted_iota(jnp.int32, (tq, tq), 0) <= lax.broadcasted_iota(jnp.int32, (tq, tq), 1)

    def attend(n_before):
        nk = n_before + tq
        ss = [_dot(k_ref[0, h, 0:nk, :], q_t_ref[0, 0, h]) for h in range(hb)]
        for h in range(hb):
            s_diag = jnp.where(tri, ss[h][n_before:nk], NEG)
            m = jnp.max(s_diag, axis=0, keepdims=True)
            if n_before:
                m = jnp.maximum(m, jnp.max(ss[h][0:n_before], axis=0, keepdims=True))
            p = _exp_bf16(s_diag - m)
            if n_before:
                p = jnp.concatenate([_exp_bf16(ss[h][0:n_before] - m), p], axis=0)
            ov = _dot(v_t_ref[0, h, :, 0:nk], p)
            o_ref[0, :, h * d_v:(h + 1) * d_v] = (ov[0:d_v] / ov[d_v:d_v + 1]).T

    for v in range(n_q):
        pl.when(qi == v)(functools.partial(attend, v * tq))


def _mla_attn(q_t, k, v_t, *, tq=256, hb=4):
    b, h, s, dq = k.shape
    dv_rows = v_t.shape[2]
    dv = dv_rows - ONES_ROWS
    assert q_t.shape == (b, s // tq, h, dq, tq)
    return pl.pallas_call(
        functools.partial(_mla_attn_kernel, tq=tq, hb=hb, n_q=s // tq),
        grid=(b, h // hb, s // tq),
        in_specs=[pl.BlockSpec((1, 1, hb, dq, tq), lambda i, j, l: (i, l, j, 0, 0)),
                  pl.BlockSpec((1, hb, s, dq), lambda i, j, l: (i, j, 0, 0)),
                  pl.BlockSpec((1, hb, dv_rows, s), lambda i, j, l: (i, j, 0, 0))],
        out_specs=pl.BlockSpec((1, tq, hb * dv), lambda i, j, l: (i, l, j)),
        out_shape=jax.ShapeDtypeStruct((b, s, h * dv), F32),
        compiler_params=_cparams(("parallel", "parallel", "arbitrary")),
        name="mla_attn",
    )(q_t, k, v_t)


def _out_kernel(xa_ref, xb_ref, oa_ref, ob_ref, ga_ref, gb_ref, w_ref, o_ref, *, nb_first):
    oa = _rms(oa_ref[...], ga_ref[...]).astype(BF16)
    ob = _rms(ob_ref[...], gb_ref[...]).astype(BF16)
    o_ref[...] = (_two_part_rows(xa_ref, xb_ref, nb_first)
                  + _dot(jnp.concatenate([oa, ob], axis=1), w_ref[...]))


def _out_proj(x_parts, oa, ob, gain_a, gain_b, w, *, tm=512):
    t, d = oa.shape[0], x_parts[0].shape[1]
    nb_first, x_specs = _two_part_specs(x_parts, tm)

    def rows(width):
        return pl.BlockSpec((tm, width), lambda i: (i, 0))

    def full(a):
        return pl.BlockSpec(a.shape, lambda i: (0,) * a.ndim)

    return pl.pallas_call(
        functools.partial(_out_kernel, nb_first=nb_first),
        grid=(t // tm,),
        in_specs=x_specs + [rows(oa.shape[1]), rows(ob.shape[1]), full(gain_a), full(gain_b), full(w)],
        out_specs=rows(d),
        out_shape=jax.ShapeDtypeStruct((t, d), F32),
        compiler_params=_cparams(("parallel",)),
        name="out_proj",
    )(*x_parts, oa, ob, gain_a, gain_b, w)


def _rot_cols(w):
    half = w.shape[1] // 2
    return jnp.concatenate([-w[:, half:], w[:, :half]], axis=1)


def _layout_w_in(w_in):
    d = w_in.shape[0]
    off = np.cumsum([0, NSA_WIDTH] + [NSA_KV_WIDTH] * 6 + [NSA_HEADS * 3, MLA_Q_RANK, MLA_KV_RANK, MLA_ROPE_DIM])
    q, kc, vc, ks, vs, kw, vw, gt, cq, ckv, kr = [w_in[:, off[i]:off[i + 1]] for i in range(11)]
    zr = jnp.zeros((d, LANES - MLA_ROPE_DIM), w_in.dtype)
    zg = jnp.zeros((d, LANES - NSA_HEADS * 3), w_in.dtype)
    w_n = jnp.concatenate([kc, vc, ks, kw, cq, ckv, kr, zr, _rot_cols(kr), zr], axis=1).astype(BF16)
    w_t = jnp.concatenate([q, vs, vw, gt, zg], axis=1).astype(BF16).T
    return w_n, w_t


def _layout_mla_weights(w_uq, w_ukv):
    r = w_uq.shape[0]
    zr = jnp.zeros((r, LANES - MLA_ROPE_DIM), w_uq.dtype)
    main, rot, k_cols, v_cols = [], [], [], []
    for hd in range(MLA_HEADS):
        base = hd * MLA_QK_DIM
        rope = w_uq[:, base + MLA_NOPE_DIM:base + MLA_QK_DIM]
        main += [w_uq[:, base:base + MLA_NOPE_DIM], rope, zr]
        rot += [_rot_cols(rope), zr]
        kv = hd * (MLA_NOPE_DIM + MLA_V_DIM)
        k_cols.append(w_ukv[:, kv:kv + MLA_NOPE_DIM])
        v_cols.append(w_ukv[:, kv + MLA_NOPE_DIM:kv + MLA_NOPE_DIM + MLA_V_DIM])
    cat = lambda parts: jnp.concatenate(parts, axis=1).astype(BF16)
    return cat(main).T, cat(rot).T, cat(k_cols), cat(v_cols).T


def _rope_gains(gain):
    half = MLA_ROPE_DIM // 2
    zr = jnp.zeros((LANES - MLA_ROPE_DIM,), gain.dtype)
    g_rope = gain[MLA_NOPE_DIM:]
    return (gain[None, :MLA_NOPE_DIM], jnp.concatenate([g_rope, zr])[None, :],
            jnp.concatenate([g_rope[half:], g_rope[:half], zr])[None, :])


def _rope_tables(s):
    half = MLA_ROPE_DIM // 2
    inv = ROPE_THETA ** (-jnp.arange(half, dtype=F32) / half)
    ang = jnp.arange(s).astype(F32)[:, None] * inv[None, :]
    zr = jnp.zeros((s, LANES - MLA_ROPE_DIM), F32)
    cos, sin = jnp.cos(ang), jnp.sin(ang)
    return jnp.concatenate([cos, cos, zr], axis=1), jnp.concatenate([sin, sin, zr], axis=1)


def kernel(x, ffn1_norm, ffn1_w_gate, ffn1_w_up, ffn1_w_down, mix_norm, w_in, nsa_q_norm, nsa_k_norm, nsa_cmp_pos_k, nsa_cmp_w1_k, nsa_cmp_w2_k, nsa_cmp_pos_v, nsa_cmp_w1_v, nsa_cmp_w2_v, mla_q_a_norm, mla_w_uq, mla_kv_a_norm, mla_w_ukv, mla_q_norm, mla_k_norm, out_norm_nsa, out_norm_mla, w_out, ffn2_norm, ffn2_w_gate, ffn2_w_up, ffn2_w_down):
    b, s, d = x.shape
    depth = ffn1_norm.shape[0]
    tables = _rope_tables(s)
    xt = x.reshape(b * s, d)
    for l in range(depth):
        x_parts = _ffn(xt, ffn1_norm[l][None, :], ffn1_w_gate[l], ffn1_w_up[l], ffn1_w_down[l], in_place=False)

        w_n, w_t = _layout_w_in(w_in[l])
        q_t, kv_cmp, ks, vs_t, kw, vw_t, gate_t, cq, ckv, kr = _proj(
            x_parts, mix_norm[l][None, :], w_n, w_t, nsa_q_norm[l], nsa_k_norm[l],
            mla_q_a_norm[l][None, :], mla_kv_a_norm[l][None, :])

        kc, vc_t = _compress(
            kv_cmp, s,
            nsa_cmp_w1_k[l].astype(BF16), nsa_cmp_w2_k[l].astype(BF16), nsa_cmp_pos_k[l].reshape(1, -1),
            nsa_cmp_w1_v[l].astype(BF16), nsa_cmp_w2_v[l].astype(BF16).T, nsa_cmp_pos_v[l].reshape(1, -1),
            nsa_k_norm[l])

        def bs(a):
            return a.reshape(b, s, a.shape[-1])

        o_a = _nsa(q_t, kc, vc_t, bs(ks), vs_t, bs(kw), vw_t, gate_t)

        q_m, k_m, v_m = _mla_prep(
            bs(cq), bs(ckv), bs(kr), tables, _layout_mla_weights(mla_w_uq[l], mla_w_ukv[l]),
            _rope_gains(mla_q_norm[l]), _rope_gains(mla_k_norm[l]))
        o_b = _mla_attn(q_m, k_m, v_m)

        xt = _out_proj(x_parts, o_a.reshape(b * s, NSA_WIDTH), o_b.reshape(b * s, MLA_WIDTH),
                       out_norm_nsa[l][None, :], out_norm_mla[l][None, :], w_out[l].astype(BF16))

        xt = _ffn(xt, ffn2_norm[l][None, :], ffn2_w_gate[l], ffn2_w_up[l], ffn2_w_down[l], in_place=True)
    return xt.reshape(b, s, d)
```

```python
import functools

import numpy as np
import jax
import jax.numpy as jnp
from jax import lax
from jax.experimental import pallas as pl
from jax.experimental.pallas import tpu as pltpu

F32 = jnp.float32
BF16 = jnp.bfloat16

HEAD_DIM = 128
NSA_HEADS = 8
NSA_KV_HEADS = 2
NSA_GROUP = NSA_HEADS // NSA_KV_HEADS
CMP_BLOCK = 32
CMP_STRIDE = 16
CMP_HIDDEN = 256
SEL_BLOCK = 64
SEL_SHIFT = 6
SEL_TOP = 8
WINDOW = 512
MLA_HEADS = 8
MLA_Q_RANK = 384
MLA_KV_RANK = 256
MLA_NOPE_DIM = 128
MLA_ROPE_DIM = 64
MLA_V_DIM = 128
MLA_QK_DIM = MLA_NOPE_DIM + MLA_ROPE_DIM
ROPE_THETA = 10000.0
EPS = 1e-6
NEG = -1e30
FORCE_SCORE = 1e4
NSA_WIDTH = NSA_HEADS * HEAD_DIM
MLA_WIDTH = MLA_HEADS * MLA_V_DIM
NSA_KV_WIDTH = NSA_KV_HEADS * HEAD_DIM

LANES = 128
MLA_QK_PAD = 2 * LANES
VMEM_LIMIT = 58 * 1024 * 1024
POS_SPLIT = 16
FEAT_ROWS = 16


def _cparams(sem):
    return pltpu.CompilerParams(dimension_semantics=sem, vmem_limit_bytes=VMEM_LIMIT)


def _rms(x, gain):
    return x * lax.rsqrt(jnp.mean(x * x, axis=-1, keepdims=True) + EPS) * gain


def _dot(a, b):
    return jnp.dot(a, b, preferred_element_type=F32)


def _dot_nt(a, b):
    return lax.dot_general(a, b, (((1,), (1,)), ((), ())), preferred_element_type=F32)


def _sigmoid(x):
    return 1.0 / (1.0 + jnp.exp(-x))


def _ffn_accumulate(x_ref, g_ref, weights, o_ref, h_ref):
    @pl.when(pl.program_id(1) == 0)
    def _():
        x = x_ref[...]
        h_ref[...] = _rms(x, g_ref[...]).astype(BF16)
        o_ref[...] = x

    wg, wu, wd = weights()
    h = h_ref[...]
    gate = _dot(h, wg)
    up = _dot(h, wu)
    act = gate * _sigmoid(gate) * up * 0.5
    o_ref[...] += _dot(act.astype(BF16), wd)


def _ffn_first_kernel(x_ref, g_ref, wg_ref, wu_ref, wd_ref, o_ref, wg_out, wu_out, wd_out, h_ref):
    def weights():
        wg, wu, wd = wg_ref[...].astype(BF16), wu_ref[...].astype(BF16), wd_ref[...].astype(BF16)
        wg_out[...], wu_out[...], wd_out[...] = wg, wu, wd
        return wg, wu, wd

    _ffn_accumulate(x_ref, g_ref, weights, o_ref, h_ref)


def _ffn_rest_kernel(x_ref, g_ref, wg_ref, wu_ref, wd_ref, o_ref, h_ref):
    _ffn_accumulate(x_ref, g_ref, lambda: (wg_ref[...], wu_ref[...], wd_ref[...]), o_ref, h_ref)


def _ffn(x, gain, wg, wu, wd, *, in_place, tm=1024, tf_first=256, tf=512):
    t, d = x.shape
    f = wg.shape[1]
    alias = {0: 0} if in_place else {}

    def w_specs(tfx):
        return [pl.BlockSpec((d, tfx), lambda i, j: (0, j)),
                pl.BlockSpec((d, tfx), lambda i, j: (0, j)),
                pl.BlockSpec((tfx, d), lambda i, j: (j, 0))]

    y, wg_b, wu_b, wd_b = pl.pallas_call(
        _ffn_first_kernel,
        grid=(1, f // tf_first),
        in_specs=[pl.BlockSpec((tm, d), lambda i, j: (0, 0), pipeline_mode=pl.Buffered(1)),
                  pl.BlockSpec((1, d), lambda i, j: (0, 0))] + w_specs(tf_first),
        out_specs=[pl.BlockSpec((tm, d), lambda i, j: (0, 0))] + w_specs(tf_first),
        out_shape=[jax.ShapeDtypeStruct((t if in_place else tm, d), F32), jax.ShapeDtypeStruct((d, f), BF16),
                   jax.ShapeDtypeStruct((d, f), BF16), jax.ShapeDtypeStruct((f, d), BF16)],
        scratch_shapes=[pltpu.VMEM((tm, d), BF16)],
        input_output_aliases=alias,
        compiler_params=_cparams(("arbitrary", "arbitrary")),
        name="ffn_first",
    )(x, gain, wg, wu, wd)

    out_row0 = 1 if in_place else 0
    rest = pl.pallas_call(
        _ffn_rest_kernel,
        grid=(t // tm - 1, f // tf),
        in_specs=[pl.BlockSpec((tm, d), lambda i, j: (i + 1, 0)),
                  pl.BlockSpec((1, d), lambda i, j: (0, 0))] + w_specs(tf),
        out_specs=pl.BlockSpec((tm, d), lambda i, j: (i + out_row0, 0)),
        out_shape=jax.ShapeDtypeStruct((t if in_place else t - tm, d), F32),
        scratch_shapes=[pltpu.VMEM((tm, d), BF16)],
        input_output_aliases=alias,
        compiler_params=_cparams(("parallel", "arbitrary")),
        name="ffn_rest",
    )(y if in_place else x, gain, wg_b, wu_b, wd_b)
    return rest if in_place else (y, rest)


_N_CMP = 0
_N_KS = _N_CMP + 2 * NSA_KV_WIDTH
_N_KW = _N_KS + NSA_KV_WIDTH
_N_CQ = _N_KW + NSA_KV_WIDTH
_N_CKV = _N_CQ + MLA_Q_RANK
_N_KR = _N_CKV + MLA_KV_RANK
_N_END = _N_KR + 2 * LANES
_T_Q = 0
_T_VS = _T_Q + NSA_WIDTH
_T_VW = _T_VS + NSA_KV_WIDTH
_T_GATE = _T_VW + NSA_KV_WIDTH
_T_END = _T_GATE + LANES


def _two_part_specs(parts, tm):
    first, rest = parts
    nb_first = first.shape[0] // tm
    d = first.shape[1]
    return nb_first, [pl.BlockSpec((tm, d), lambda i: (jnp.minimum(i, nb_first - 1), 0)),
                      pl.BlockSpec((tm, d), lambda i: (jnp.maximum(i - nb_first, 0), 0))]


def _two_part_rows(first_ref, rest_ref, nb_first):
    return jnp.where(pl.program_id(0) < nb_first, first_ref[...], rest_ref[...])


def _proj_kernel(xa_ref, xb_ref, g_ref, wn_ref, wt_ref, qg_ref, kg_ref, qag_ref, kvag_ref,
                 q_t_ref, cmp_ref, ks_ref, vs_t_ref, kw_ref, vw_t_ref, gate_t_ref, cq_ref, ckv_ref, kr_ref, *,
                 nb_first):
    tq = q_t_ref.shape[2]
    h = _rms(_two_part_rows(xa_ref, xb_ref, nb_first), g_ref[...]).astype(BF16)
    p_n = _dot(h, wn_ref[...])
    p_t = lax.dot_general(wt_ref[...], h, (((0,), (1,)), ((), ())), preferred_element_type=F32)

    def proj(lo, hi):
        return p_n[:, lo:hi]

    def proj_t(lo, hi):
        return p_t[lo:hi, :]

    qscale = HEAD_DIM ** -0.5
    for hd in range(NSA_HEADS):
        q = proj_t(_T_Q + hd * HEAD_DIM, _T_Q + (hd + 1) * HEAD_DIM)
        inv = lax.rsqrt(jnp.mean(q * q, axis=0, keepdims=True) + EPS) * qscale
        qn = (q * inv * qg_ref[...]).astype(BF16)
        for c in range(q_t_ref.shape[0]):
            q_t_ref[c, hd * HEAD_DIM:(hd + 1) * HEAD_DIM, :] = qn[:, c * tq:(c + 1) * tq]
    for c in range(cmp_ref.shape[0]):
        cmp_ref[c] = proj(_N_CMP + c * HEAD_DIM, _N_CMP + (c + 1) * HEAD_DIM)
    for g in range(NSA_KV_HEADS):
        sl = slice(g * HEAD_DIM, (g + 1) * HEAD_DIM)
        ks_ref[:, sl] = _rms(proj(_N_KS + g * HEAD_DIM, _N_KS + (g + 1) * HEAD_DIM), kg_ref[1:2, :]).astype(BF16)
        kw_ref[:, sl] = _rms(proj(_N_KW + g * HEAD_DIM, _N_KW + (g + 1) * HEAD_DIM), kg_ref[2:3, :]).astype(BF16)
    vs_t_ref[...] = proj_t(_T_VS, _T_VW).astype(BF16)
    vw_t_ref[...] = proj_t(_T_VW, _T_GATE).astype(BF16)
    cq_ref[...] = _rms(proj(_N_CQ, _N_CKV), qag_ref[...]).astype(BF16)
    ckv_ref[...] = _rms(proj(_N_CKV, _N_KR), kvag_ref[...]).astype(BF16)
    kr_ref[...] = proj(_N_KR, _N_END)
    gates = _sigmoid(proj_t(_T_GATE, _T_END))
    for c in range(gate_t_ref.shape[0]):
        gate_t_ref[c] = gates[:, c * tq:(c + 1) * tq]


def _proj(x_parts, gain, w_n, w_t, q_gain, k_gains, qa_gain, kva_gain, *, tm=512, tq=128):
    t = x_parts[0].shape[0] + x_parts[1].shape[0]
    nb_first, x_specs = _two_part_specs(x_parts, tm)
    q_gain = jnp.broadcast_to(q_gain.reshape(HEAD_DIM, 1), (HEAD_DIM, tm))

    def full(a):
        return pl.BlockSpec(a.shape, lambda i: (0,) * a.ndim)

    def rows(width):
        return (t, width), pl.BlockSpec((tm, width), lambda i: (i, 0))

    def cols(height):
        return (height, t), pl.BlockSpec((height, tm), lambda i: (0, i))

    def q_tiles(height):
        return (t // tq, height, tq), pl.BlockSpec((tm // tq, height, tq), lambda i: (i, 0, 0))

    n_cmp_cols = 2 * NSA_KV_WIDTH // HEAD_DIM
    cmp_out = (n_cmp_cols, t, HEAD_DIM), pl.BlockSpec((n_cmp_cols, tm, HEAD_DIM), lambda i: (0, i, 0))
    outs = [(q_tiles(NSA_WIDTH), BF16), (cmp_out, F32), (rows(NSA_KV_WIDTH), BF16),
            (cols(NSA_KV_WIDTH), BF16), (rows(NSA_KV_WIDTH), BF16), (cols(NSA_KV_WIDTH), BF16),
            (q_tiles(LANES), F32), (rows(MLA_Q_RANK), BF16), (rows(MLA_KV_RANK), BF16), (rows(2 * LANES), F32)]
    return pl.pallas_call(
        functools.partial(_proj_kernel, nb_first=nb_first),
        grid=(t // tm,),
        in_specs=x_specs + [full(gain), full(w_n), full(w_t), full(q_gain), full(k_gains), full(qa_gain),
                            full(kva_gain)],
        out_specs=[spec for (_, spec), _ in outs],
        out_shape=[jax.ShapeDtypeStruct(shape, dt) for (shape, _), dt in outs],
        compiler_params=_cparams(("parallel",)),
        name="proj",
    )(*x_parts, gain, w_n, w_t, q_gain, k_gains, qa_gain, kva_gain)


def _gelu_tanh(x):
    return 0.5 * x * (1.0 + jnp.tanh(np.sqrt(2.0 / np.pi) * (x + 0.044715 * (x * x * x))))


def _compress_kernel(x_ref, w1k_ref, w2k_ref, pk_ref, w1v_ref, w2v_ref, pv_ref, kg_ref, kc_ref, vc_t_ref):
    half = CMP_STRIDE * HEAD_DIM
    nrow = x_ref.shape[1] // CMP_STRIDE
    for which, (w1_ref, w2_ref, p_ref) in enumerate(((w1k_ref, w2k_ref, pk_ref), (w1v_ref, w2v_ref, pv_ref))):
        pos_bias = _dot(jnp.broadcast_to(p_ref[...], (8, 2 * half)).astype(BF16), w1_ref[...])[0:1, :]
        for g in range(NSA_KV_HEADS):
            col = which * NSA_KV_HEADS + g
            xg = jnp.concatenate(
                [x_ref[col, pl.ds(l, nrow, stride=CMP_STRIDE), :].astype(BF16) for l in range(CMP_STRIDE)], axis=1)
            first = _dot(xg, w1_ref[0:half, :])
            second = _dot(xg, w1_ref[half:2 * half, :])
            pre = first + pltpu.roll(second, nrow - 1, 0) + pos_bias
            hidden = _gelu_tanh(pre).astype(BF16)
            if which == 0:
                kc_ref[0, g] = _rms(_dot(hidden, w2_ref[...]), kg_ref[0:1, :]).astype(BF16)
            else:
                vc_t_ref[0, g] = _dot_nt(w2_ref[...], hidden).astype(BF16)


def _compress(xc, seq, w1k, w2k, pk, w1v, w2v, pv, k_gains):
    ncol, t, width = xc.shape
    b, nrow = t // seq, seq // CMP_STRIDE

    def full(a):
        return pl.BlockSpec(a.shape, lambda i: (0,) * a.ndim)

    out_sds = jax.ShapeDtypeStruct((b, NSA_KV_HEADS, nrow, HEAD_DIM), BF16)
    out_spec = pl.BlockSpec((1, NSA_KV_HEADS, nrow, HEAD_DIM), lambda i: (i, 0, 0, 0))
    return pl.pallas_call(
        _compress_kernel,
        grid=(b,),
        in_specs=[pl.BlockSpec((ncol, seq, width), lambda i: (0, i, 0)),
                  full(w1k), full(w2k), full(pk), full(w1v), full(w2v), full(pv), full(k_gains)],
        out_specs=[out_spec, out_spec],
        out_shape=[out_sds, out_sds],
        compiler_params=_cparams(("parallel",)),
        name="compress",
    )(xc, w1k, w2k, pk, w1v, w2v, pv, k_gains)


def _softmax_cols(s):
    m = jnp.max(s, axis=0, keepdims=True)
    p = jnp.exp(s - m)
    return m, jnp.sum(p, axis=0, keepdims=True), p


def _nsa_kernel(q_t_ref, kc_ref, vc_t_ref, ks_ref, vs_t_ref, kw_ref, vw_t_ref, gate_t_ref, kfeat_ref, cfeat_ref,
                slope_ref, o_ref, *, tq, ck):
    qi = pl.program_id(1)
    q0 = qi * tq
    ng = NSA_KV_HEADS
    wide = NSA_GROUP * tq
    n_sel = ks_ref.shape[1] // SEL_BLOCK
    ncmp = kc_ref.shape[2]
    n_win = WINDOW + tq

    def tile4(a):
        return jnp.concatenate([a] * NSA_GROUP, axis=1)

    def masked(s, mask, fill=NEG):
        return jnp.concatenate([jnp.where(mask, s[:, r * tq:(r + 1) * tq], fill) for r in range(NSA_GROUP)], axis=1)

    def q_aug(g, extra):
        q = jnp.concatenate([q_t_ref[0, (g * NSA_GROUP + r) * HEAD_DIM:(g * NSA_GROUP + r + 1) * HEAD_DIM, :]
                             for r in range(NSA_GROUP)], axis=1)
        rows = [q, slope_ref[g]]
        used = slope_ref.shape[1]
        if extra is not None:
            rows.append(extra)
            used += extra.shape[0]
        rows.append(jnp.zeros((HEAD_DIM - used, wide), BF16))
        return jnp.concatenate(rows, axis=0)

    def gsl(g):
        return slice(g * HEAD_DIM, (g + 1) * HEAD_DIM)

    lane_q = q0 + lax.broadcasted_iota(jnp.int32, (1, tq), 1)

    qa_plain = [q_aug(g, None) for g in range(ng)]
    w0 = pl.multiple_of(jnp.maximum(q0 - WINDOW, 0), tq)
    s_cmp = [_dot(jnp.concatenate([kc_ref[0, g], cfeat_ref[...]], axis=1), qa_plain[g]) for g in range(ng)]
    s_win = [_dot(jnp.concatenate([kw_ref[0, pl.ds(w0, n_win), gsl(g)], kfeat_ref[pl.ds(w0, n_win), :]], axis=1),
                  qa_plain[g]) for g in range(ng)]

    n_row = lax.broadcasted_iota(jnp.int32, (ncmp, tq), 0)
    mask_c = n_row * CMP_STRIDE + (CMP_BLOCK - 1) <= lane_q
    s_row = lax.broadcasted_iota(jnp.int32, (n_sel, ncmp), 0)
    n_col = lax.broadcasted_iota(jnp.int32, (n_sel, ncmp), 1)
    overlap_t = jnp.where((n_col * CMP_STRIDE < (s_row + 1) * SEL_BLOCK)
                          & (n_col * CMP_STRIDE + (CMP_BLOCK - 1) >= s_row * SEL_BLOCK), 1.0, 0.0).astype(BF16)
    p_cmp, imp = [], []
    for g in range(ng):
        s = masked(s_cmp[g], mask_c)
        m = jnp.max(s, axis=0, keepdims=True)
        p = masked(jnp.exp(s - m), mask_c, 0.0)
        l = jnp.sum(p, axis=0, keepdims=True)
        p = p / jnp.where(l > 0.0, l, 1.0)
        p_cmp.append(p.astype(BF16))
        p_sum = p[:, 0:tq]
        for r in range(1, NSA_GROUP):
            p_sum = p_sum + p[:, r * tq:(r + 1) * tq]
        p_hi = p_sum.astype(BF16)
        p_lo = (p_sum - p_hi.astype(F32)).astype(BF16)
        imp.append(_dot(overlap_t, p_hi) + _dot(overlap_t, p_lo))
    o_cmp = [_dot(vc_t_ref[0, g], p_cmp[g]) for g in range(ng)]

    blk = lax.broadcasted_iota(jnp.int32, (n_sel, tq), 0)
    blk_f = blk.astype(F32)
    forced = (blk == 0) | (blk == jnp.right_shift(lane_q, SEL_SHIFT))
    future = blk * SEL_BLOCK > lane_q
    penalty = []
    for g in range(ng):
        v = jnp.where(future, NEG, jnp.where(forced, FORCE_SCORE, imp[g]))
        unselected = jnp.full((n_sel, tq), NEG, F32)
        for _ in range(min(SEL_TOP, n_sel)):
            mx = jnp.max(v, axis=0, keepdims=True)
            first = jnp.min(jnp.where(v == mx, blk_f, float(n_sel)), axis=0, keepdims=True)
            pick = blk_f == first
            unselected = jnp.where(pick, 0.0, unselected)
            v = jnp.where(pick, -jnp.inf, v)
        penalty.append(tile4(unselected.astype(BF16)))

    rel = lax.broadcasted_iota(jnp.int32, (n_win, tq), 0) - lax.broadcasted_iota(jnp.int32, (n_win, tq), 1)
    off = q0 - w0
    mask_w = (rel <= off) & (rel > off - WINDOW)
    o_win = []
    p_win = []
    for g in range(ng):
        _, l, p = _softmax_cols(masked(s_win[g], mask_w))
        p_win.append((l, p.astype(BF16)))
    for g in range(ng):
        l, p = p_win[g]
        o_win.append(_dot(vw_t_ref[gsl(g), pl.ds(w0, n_win)], p) / l)

    partial = []
    for g in range(ng):
        for r in range(NSA_GROUP):
            hd = g * NSA_GROUP + r
            cs = slice(r * tq, (r + 1) * tq)
            partial.append(gate_t_ref[0, 3 * hd:3 * hd + 1, :] * o_cmp[g][:, cs]
                           + gate_t_ref[0, 3 * hd + 2:3 * hd + 3, :] * o_win[g][:, cs])

    qa_sel = [q_aug(g, penalty[g]) for g in range(ng)]
    rel_d = lax.broadcasted_iota(jnp.int32, (ck, tq), 0) - lax.broadcasted_iota(jnp.int32, (ck, tq), 1)

    def select_and_store(n_before):
        nk = n_before + ck
        mask_d = rel_d <= q0 - n_before
        ss = [_dot(jnp.concatenate([ks_ref[0, 0:nk, gsl(g)], kfeat_ref[0:nk, :]], axis=1), qa_sel[g])
              for g in range(ng)]
        for g in range(ng):
            s_diag = masked(ss[g][n_before:nk], mask_d)
            m = jnp.max(s_diag, axis=0, keepdims=True)
            if n_before:
                m = jnp.maximum(m, jnp.max(ss[g][0:n_before], axis=0, keepdims=True))
            p_diag = jnp.exp(s_diag - m)
            l = jnp.sum(p_diag, axis=0, keepdims=True)
            p = p_diag.astype(BF16)
            if n_before:
                p_before = jnp.exp(ss[g][0:n_before] - m)
                l = l + jnp.sum(p_before, axis=0, keepdims=True)
                p = jnp.concatenate([p_before.astype(BF16), p], axis=0)
            o_sel = _dot(vs_t_ref[gsl(g), 0:nk], p) / l
            for r in range(NSA_GROUP):
                hd = g * NSA_GROUP + r
                mixed = partial[hd] + gate_t_ref[0, 3 * hd + 1:3 * hd + 2, :] * o_sel[:, r * tq:(r + 1) * tq]
                o_ref[0, :, hd * HEAD_DIM:(hd + 1) * HEAD_DIM] = mixed.T

    per = ck // tq
    for v in range(ks_ref.shape[1] // ck):
        pl.when(qi // per == v)(functools.partial(select_and_store, v * ck))


def _nsa(q_t, kc, vc_t, ks, vs_t, kw, vw_t, gate_t, *, tq=128, ck=256):
    b, s, _ = ks.shape
    nq = s // tq
    pos = np.arange(s)
    kfeat = np.zeros((s, LANES), np.float32)
    kfeat[:, 0] = POS_SPLIT * (pos // POS_SPLIT)
    kfeat[:, 1] = pos % POS_SPLIT
    kfeat[pos, FEAT_ROWS + pos // SEL_BLOCK] = 1.0
    ncmp = kc.shape[2]
    cfeat = np.zeros((ncmp, LANES), np.float32)
    cfeat[:, 0] = CMP_STRIDE * np.arange(ncmp)
    cfeat[:, 1] = 0.5 * (CMP_BLOCK - 1)
    slope = np.zeros((NSA_KV_HEADS, FEAT_ROWS, NSA_GROUP * tq), np.float32)
    for hd in range(NSA_HEADS):
        g, r = divmod(hd, NSA_GROUP)
        slope[g, 0:2, r * tq:(r + 1) * tq] = 2.0 ** (-8.0 * (hd + 1) / NSA_HEADS)
    assert FEAT_ROWS + s // SEL_BLOCK <= LANES and s // SEL_BLOCK % FEAT_ROWS == 0 and s % ck == 0 and ck % tq == 0

    def full(a):
        return pl.BlockSpec(a.shape, lambda i, j: (0,) * a.ndim)

    def per_b(a):
        return pl.BlockSpec((1,) + a.shape[1:], lambda i, j: (i,) + (0,) * (a.ndim - 1))

    def per_b_cols(a):
        return pl.BlockSpec((a.shape[0], s), lambda i, j: (0, i))

    def q_cols(a):
        assert a.shape[0] == b * nq and a.shape[2] == tq
        return pl.BlockSpec((1, a.shape[1], tq), lambda i, j: (i * nq + j, 0, 0))

    consts = [jnp.asarray(kfeat, BF16), jnp.asarray(cfeat, BF16), jnp.asarray(slope, BF16)]
    return pl.pallas_call(
        functools.partial(_nsa_kernel, tq=tq, ck=ck),
        grid=(b, nq),
        in_specs=[q_cols(q_t), per_b(kc), per_b(vc_t), per_b(ks), per_b_cols(vs_t), per_b(kw), per_b_cols(vw_t),
                  q_cols(gate_t)] + [full(c) for c in consts],
        out_specs=pl.BlockSpec((1, tq, NSA_WIDTH), lambda i, j: (i, j, 0)),
        out_shape=jax.ShapeDtypeStruct((b, s, NSA_WIDTH), F32),
        compiler_params=_cparams(("parallel", "arbitrary")),
        name="nsa",
    )(q_t, kc, vc_t, ks, vs_t, kw, vw_t, gate_t, *consts)


def _mla_prep_kernel(cq_ref, ckv_ref, kr_ref, cos_ref, sin_ref, cos_t_ref, sin_t_ref, wq_t_ref, wqr_t_ref, wk_ref,
                     wv_t_ref, qgn_ref, qgr_ref, qgt_ref, kgn_ref, kgr_ref, kgt_ref, q_t_ref, k_ref, v_t_ref):
    cq = cq_ref[0]
    ckv = ckv_ref[0]
    k_rope = kr_ref[0, :, 0:LANES]
    k_rot = kr_ref[0, :, LANES:2 * LANES]
    k_rope_ssq = jnp.sum(k_rope * k_rope, axis=-1, keepdims=True)
    k_rope_emb = k_rope * kgr_ref[...] * cos_ref[...] + k_rot * kgt_ref[...] * sin_ref[...]
    q_cos = qgr_ref[...] * cos_t_ref[...]
    q_sin = qgt_ref[...] * sin_t_ref[...]
    qscale = MLA_QK_DIM ** -0.5
    qx_all = _dot_nt(wq_t_ref[...], cq)
    q_rot_all = _dot_nt(wqr_t_ref[...], cq)
    k_nope_all = _dot(ckv, wk_ref[...])
    v_all = _dot_nt(wv_t_ref[...], ckv)
    for hd in range(MLA_HEADS):
        q_nope = qx_all[hd * MLA_QK_PAD:hd * MLA_QK_PAD + LANES]
        q_rope = qx_all[hd * MLA_QK_PAD + LANES:(hd + 1) * MLA_QK_PAD]
        q_rot = q_rot_all[hd * LANES:(hd + 1) * LANES]
        ssq = jnp.sum(q_nope * q_nope, axis=0, keepdims=True) + jnp.sum(q_rope * q_rope, axis=0, keepdims=True)
        inv = lax.rsqrt(ssq / MLA_QK_DIM + EPS) * qscale
        q_t_ref[0, 0, hd, 0:LANES, :] = (q_nope * inv * qgn_ref[...]).astype(BF16)
        q_t_ref[0, 0, hd, LANES:2 * LANES, :] = ((q_rope * q_cos + q_rot * q_sin) * inv).astype(BF16)

        k_nope = k_nope_all[:, hd * LANES:(hd + 1) * LANES]
        ssq = jnp.sum(k_nope * k_nope, axis=-1, keepdims=True) + k_rope_ssq
        inv = lax.rsqrt(ssq / MLA_QK_DIM + EPS)
        k_ref[0, hd, :, 0:LANES] = (k_nope * inv * kgn_ref[...]).astype(BF16)
        k_ref[0, hd, :, LANES:2 * LANES] = (k_rope_emb * inv).astype(BF16)
        v_t_ref[0, hd] = v_all[hd * LANES:(hd + 1) * LANES].astype(BF16)


def _mla_prep(cq, ckv, kr, tables, weights, q_gains, k_gains, *, tm=256):
    b, s, _ = cq.shape
    cos, sin = tables
    q_gains = tuple(jnp.broadcast_to(g.reshape(LANES, 1), (LANES, tm)) for g in q_gains)

    def full(a):
        return pl.BlockSpec(a.shape, lambda i, j: (0,) * a.ndim)

    def rows(width):
        return pl.BlockSpec((1, tm, width), lambda i, j: (i, j, 0))

    pos_rows = pl.BlockSpec((tm, LANES), lambda i, j: (j, 0))
    pos_cols = pl.BlockSpec((LANES, tm), lambda i, j: (0, j))
    return pl.pallas_call(
        _mla_prep_kernel,
        grid=(b, s // tm),
        in_specs=[rows(MLA_Q_RANK), rows(MLA_KV_RANK), rows(2 * LANES), pos_rows, pos_rows, pos_cols, pos_cols]
                 + [full(w) for w in weights] + [full(g) for g in q_gains + k_gains],
        out_specs=[pl.BlockSpec((1, 1, MLA_HEADS, MLA_QK_PAD, tm), lambda i, j: (i, j, 0, 0, 0)),
                   pl.BlockSpec((1, MLA_HEADS, tm, MLA_QK_PAD), lambda i, j: (i, 0, j, 0)),
                   pl.BlockSpec((1, MLA_HEADS, MLA_V_DIM, tm), lambda i, j: (i, 0, 0, j))],
        out_shape=[jax.ShapeDtypeStruct((b, s // tm, MLA_HEADS, MLA_QK_PAD, tm), BF16),
                   jax.ShapeDtypeStruct((b, MLA_HEADS, s, MLA_QK_PAD), BF16),
                   jax.ShapeDtypeStruct((b, MLA_HEADS, MLA_V_DIM, s), BF16)],
        compiler_params=_cparams(("parallel", "parallel")),
        name="mla_prep",
    )(cq, ckv, kr, cos, sin, cos.T, sin.T, *weights, *q_gains, *k_gains)


def _mla_attn_kernel(q_t_ref, k_ref, v_t_ref, o_ref, *, tq, hb, n_q):
    qi = pl.program_id(2)
    d_v = v_t_ref.shape[2]
    tri = lax.broadcasted_iota(jnp.int32, (tq, tq), 0) <= lax.broadcasted_iota(jnp.int32, (tq, tq), 1)

    def attend(n_before):
        nk = n_before + tq
        ss = [_dot(k_ref[0, h, 0:nk, :], q_t_ref[0, 0, h]) for h in range(hb)]
        for h in range(hb):
            s_diag = jnp.where(tri, ss[h][n_before:nk], NEG)
            m = jnp.max(s_diag, axis=0, keepdims=True)
            if n_before:
                m = jnp.maximum(m, jnp.max(ss[h][0:n_before], axis=0, keepdims=True))
            p_diag = jnp.exp(s_diag - m)
            l = jnp.sum(p_diag, axis=0, keepdims=True)
            p = p_diag.astype(BF16)
            if n_before:
                p_before = jnp.exp(ss[h][0:n_before] - m)
                l = l + jnp.sum(p_before, axis=0, keepdims=True)
                p = jnp.concatenate([p_before.astype(BF16), p], axis=0)
            o_ref[0, :, h * d_v:(h + 1) * d_v] = (_dot(v_t_ref[0, h, :, 0:nk], p) / l).T

    for v in range(n_q):
        pl.when(qi == v)(functools.partial(attend, v * tq))


def _mla_attn(q_t, k, v_t, *, tq=256, hb=8):
    b, h, s, dq = k.shape
    dv = v_t.shape[2]
    assert q_t.shape == (b, s // tq, h, dq, tq)
    return pl.pallas_call(
        functools.partial(_mla_attn_kernel, tq=tq, hb=hb, n_q=s // tq),
        grid=(b, h // hb, s // tq),
        in_specs=[pl.BlockSpec((1, 1, hb, dq, tq), lambda i, j, l: (i, l, j, 0, 0)),
                  pl.BlockSpec((1, hb, s, dq), lambda i, j, l: (i, j, 0, 0)),
                  pl.BlockSpec((1, hb, dv, s), lambda i, j, l: (i, j, 0, 0))],
        out_specs=pl.BlockSpec((1, tq, hb * dv), lambda i, j, l: (i, l, j)),
        out_shape=jax.ShapeDtypeStruct((b, s, h * dv), F32),
        compiler_params=_cparams(("parallel", "parallel", "arbitrary")),
        name="mla_attn",
    )(q_t, k, v_t)


def _out_kernel(xa_ref, xb_ref, oa_ref, ob_ref, ga_ref, gb_ref, w_ref, o_ref, *, nb_first):
    oa = _rms(oa_ref[...], ga_ref[...]).astype(BF16)
    ob = _rms(ob_ref[...], gb_ref[...]).astype(BF16)
    o_ref[...] = (_two_part_rows(xa_ref, xb_ref, nb_first)
                  + _dot(jnp.concatenate([oa, ob], axis=1), w_ref[...]))


def _out_proj(x_parts, oa, ob, gain_a, gain_b, w, *, tm=512):
    t, d = oa.shape[0], x_parts[0].shape[1]
    nb_first, x_specs = _two_part_specs(x_parts, tm)

    def rows(width):
        return pl.BlockSpec((tm, width), lambda i: (i, 0))

    def full(a):
        return pl.BlockSpec(a.shape, lambda i: (0,) * a.ndim)

    return pl.pallas_call(
        functools.partial(_out_kernel, nb_first=nb_first),
        grid=(t // tm,),
        in_specs=x_specs + [rows(oa.shape[1]), rows(ob.shape[1]), full(gain_a), full(gain_b), full(w)],
        out_specs=rows(d),
        out_shape=jax.ShapeDtypeStruct((t, d), F32),
        compiler_params=_cparams(("parallel",)),
        name="out_proj",
    )(*x_parts, oa, ob, gain_a, gain_b, w)


def _rot_cols(w):
    half = w.shape[1] // 2
    return jnp.concatenate([-w[:, half:], w[:, :half]], axis=1)


def _layout_w_in(w_in):
    d = w_in.shape[0]
    off = np.cumsum([0, NSA_WIDTH] + [NSA_KV_WIDTH] * 6 + [NSA_HEADS * 3, MLA_Q_RANK, MLA_KV_RANK, MLA_ROPE_DIM])
    q, kc, vc, ks, vs, kw, vw, gt, cq, ckv, kr = [w_in[:, off[i]:off[i + 1]] for i in range(11)]
    zr = jnp.zeros((d, LANES - MLA_ROPE_DIM), w_in.dtype)
    zg = jnp.zeros((d, LANES - NSA_HEADS * 3), w_in.dtype)
    w_n = jnp.concatenate([kc, vc, ks, kw, cq, ckv, kr, zr, _rot_cols(kr), zr], axis=1).astype(BF16)
    w_t = jnp.concatenate([q, vs, vw, gt, zg], axis=1).astype(BF16)
    return w_n, w_t


def _layout_mla_weights(w_uq, w_ukv):
    r = w_uq.shape[0]
    zr = jnp.zeros((r, LANES - MLA_ROPE_DIM), w_uq.dtype)
    main, rot, k_cols, v_cols = [], [], [], []
    for hd in range(MLA_HEADS):
        base = hd * MLA_QK_DIM
        rope = w_uq[:, base + MLA_NOPE_DIM:base + MLA_QK_DIM]
        main += [w_uq[:, base:base + MLA_NOPE_DIM], rope, zr]
        rot += [_rot_cols(rope), zr]
        kv = hd * (MLA_NOPE_DIM + MLA_V_DIM)
        k_cols.append(w_ukv[:, kv:kv + MLA_NOPE_DIM])
        v_cols.append(w_ukv[:, kv + MLA_NOPE_DIM:kv + MLA_NOPE_DIM + MLA_V_DIM])
    cat = lambda parts: jnp.concatenate(parts, axis=1).astype(BF16)
    return cat(main).T, cat(rot).T, cat(k_cols), cat(v_cols).T


def _rope_gains(gain):
    half = MLA_ROPE_DIM // 2
    zr = jnp.zeros((LANES - MLA_ROPE_DIM,), gain.dtype)
    g_rope = gain[MLA_NOPE_DIM:]
    return (gain[None, :MLA_NOPE_DIM], jnp.concatenate([g_rope, zr])[None, :],
            jnp.concatenate([g_rope[half:], g_rope[:half], zr])[None, :])


def _rope_tables(s):
    half = MLA_ROPE_DIM // 2
    inv = ROPE_THETA ** (-jnp.arange(half, dtype=F32) / half)
    ang = jnp.arange(s).astype(F32)[:, None] * inv[None, :]
    zr = jnp.zeros((s, LANES - MLA_ROPE_DIM), F32)
    cos, sin = jnp.cos(ang), jnp.sin(ang)
    return jnp.concatenate([cos, cos, zr], axis=1), jnp.concatenate([sin, sin, zr], axis=1)


def kernel(x, ffn1_norm, ffn1_w_gate, ffn1_w_up, ffn1_w_down, mix_norm, w_in, nsa_q_norm, nsa_k_norm, nsa_cmp_pos_k, nsa_cmp_w1_k, nsa_cmp_w2_k, nsa_cmp_pos_v, nsa_cmp_w1_v, nsa_cmp_w2_v, mla_q_a_norm, mla_w_uq, mla_kv_a_norm, mla_w_ukv, mla_q_norm, mla_k_norm, out_norm_nsa, out_norm_mla, w_out, ffn2_norm, ffn2_w_gate, ffn2_w_up, ffn2_w_down):
    b, s, d = x.shape
    depth = ffn1_norm.shape[0]
    tables = _rope_tables(s)
    xt = x.reshape(b * s, d)
    for l in range(depth):
        x_parts = _ffn(xt, ffn1_norm[l][None, :], ffn1_w_gate[l], ffn1_w_up[l], ffn1_w_down[l], in_place=False)

        w_n, w_t = _layout_w_in(w_in[l])
        q_t, kv_cmp, ks, vs_t, kw, vw_t, gate_t, cq, ckv, kr = _proj(
            x_parts, mix_norm[l][None, :], w_n, w_t, nsa_q_norm[l], nsa_k_norm[l],
            mla_q_a_norm[l][None, :], mla_kv_a_norm[l][None, :])

        kc, vc_t = _compress(
            kv_cmp, s,
            nsa_cmp_w1_k[l].astype(BF16), nsa_cmp_w2_k[l].astype(BF16), nsa_cmp_pos_k[l].reshape(1, -1),
            nsa_cmp_w1_v[l].astype(BF16), nsa_cmp_w2_v[l].astype(BF16).T, nsa_cmp_pos_v[l].reshape(1, -1),
            nsa_k_norm[l])

        def bs(a):
            return a.reshape(b, s, a.shape[-1])

        o_a = _nsa(q_t, kc, vc_t, bs(ks), vs_t, bs(kw), vw_t, gate_t)

        q_m, k_m, v_m = _mla_prep(
            bs(cq), bs(ckv), bs(kr), tables, _layout_mla_weights(mla_w_uq[l], mla_w_ukv[l]),
            _rope_gains(mla_q_norm[l]), _rope_gains(mla_k_norm[l]))
        o_b = _mla_attn(q_m, k_m, v_m)

        xt = _out_proj(x_parts, o_a.reshape(b * s, NSA_WIDTH), o_b.reshape(b * s, MLA_WIDTH),
                       out_norm_nsa[l][None, :], out_norm_mla[l][None, :], w_out[l].astype(BF16))

        xt = _ffn(xt, ffn2_norm[l][None, :], ffn2_w_gate[l], ffn2_w_up[l], ffn2_w_down[l], in_place=True)
    return xt.reshape(b, s, d)
```

```python
import functools

import numpy as np
import jax
import jax.numpy as jnp
from jax import lax
from jax.experimental import pallas as pl
from jax.experimental.pallas import tpu as pltpu

F32 = jnp.float32
BF16 = jnp.bfloat16

HEAD_DIM = 128
NSA_HEADS = 8
NSA_KV_HEADS = 2
NSA_GROUP = NSA_HEADS // NSA_KV_HEADS
CMP_BLOCK = 32
CMP_STRIDE = 16
CMP_HIDDEN = 256
SEL_BLOCK = 64
SEL_SHIFT = 6
SEL_TOP = 8
WINDOW = 512
MLA_HEADS = 8
MLA_Q_RANK = 384
MLA_KV_RANK = 256
MLA_NOPE_DIM = 128
MLA_ROPE_DIM = 64
MLA_V_DIM = 128
MLA_QK_DIM = MLA_NOPE_DIM + MLA_ROPE_DIM
ROPE_THETA = 10000.0
EPS = 1e-6
NEG = -1e30
FORCE_SCORE = 1e4
NSA_WIDTH = NSA_HEADS * HEAD_DIM
MLA_WIDTH = MLA_HEADS * MLA_V_DIM
NSA_KV_WIDTH = NSA_KV_HEADS * HEAD_DIM

LANES = 128
MLA_QK_PAD = 2 * LANES
MLA_HEAD_GROUP = 2
VMEM_LIMIT = 58 * 1024 * 1024
POS_SPLIT = 16
FEAT_ROWS = 16


def _cparams(sem):
    return pltpu.CompilerParams(dimension_semantics=sem, vmem_limit_bytes=VMEM_LIMIT)


def _rms(x, gain):
    return x * lax.rsqrt(jnp.mean(x * x, axis=-1, keepdims=True) + EPS) * gain


def _dot(a, b):
    return jnp.dot(a, b, preferred_element_type=F32)


def _dot_nt(a, b):
    return lax.dot_general(a, b, (((1,), (1,)), ((), ())), preferred_element_type=F32)


def _sigmoid(x):
    return 1.0 / (1.0 + jnp.exp(-x))


def _ffn_accumulate(x_ref, g_ref, weights, o_ref, h_ref):
    @pl.when(pl.program_id(1) == 0)
    def _():
        x = x_ref[...]
        h_ref[...] = _rms(x, g_ref[...]).astype(BF16)
        o_ref[...] = x

    wg, wu, wd = weights()
    h = h_ref[...]
    gate = _dot(h, wg)
    up = _dot(h, wu)
    act = gate * _sigmoid(gate) * up * 0.5
    o_ref[...] += _dot(act.astype(BF16), wd)


def _ffn_first_kernel(x_ref, g_ref, wg_ref, wu_ref, wd_ref, o_ref, wg_out, wu_out, wd_out, h_ref):
    def weights():
        wg, wu, wd = wg_ref[...].astype(BF16), wu_ref[...].astype(BF16), wd_ref[...].astype(BF16)
        wg_out[...], wu_out[...], wd_out[...] = wg, wu, wd
        return wg, wu, wd

    _ffn_accumulate(x_ref, g_ref, weights, o_ref, h_ref)


def _ffn_rest_kernel(x_ref, g_ref, wg_ref, wu_ref, wd_ref, o_ref, h_ref):
    _ffn_accumulate(x_ref, g_ref, lambda: (wg_ref[...], wu_ref[...], wd_ref[...]), o_ref, h_ref)


def _ffn(x, gain, wg, wu, wd, *, in_place, tm=1024, tf_first=256, tf=512):
    t, d = x.shape
    f = wg.shape[1]
    alias = {0: 0} if in_place else {}

    def w_specs(tfx):
        return [pl.BlockSpec((d, tfx), lambda i, j: (0, j)),
                pl.BlockSpec((d, tfx), lambda i, j: (0, j)),
                pl.BlockSpec((tfx, d), lambda i, j: (j, 0))]

    y, wg_b, wu_b, wd_b = pl.pallas_call(
        _ffn_first_kernel,
        grid=(1, f // tf_first),
        in_specs=[pl.BlockSpec((tm, d), lambda i, j: (0, 0), pipeline_mode=pl.Buffered(1)),
                  pl.BlockSpec((1, d), lambda i, j: (0, 0))] + w_specs(tf_first),
        out_specs=[pl.BlockSpec((tm, d), lambda i, j: (0, 0))] + w_specs(tf_first),
        out_shape=[jax.ShapeDtypeStruct((t if in_place else tm, d), F32), jax.ShapeDtypeStruct((d, f), BF16),
                   jax.ShapeDtypeStruct((d, f), BF16), jax.ShapeDtypeStruct((f, d), BF16)],
        scratch_shapes=[pltpu.VMEM((tm, d), BF16)],
        input_output_aliases=alias,
        compiler_params=_cparams(("arbitrary", "arbitrary")),
        name="ffn_first",
    )(x, gain, wg, wu, wd)

    out_row0 = 1 if in_place else 0
    rest = pl.pallas_call(
        _ffn_rest_kernel,
        grid=(t // tm - 1, f // tf),
        in_specs=[pl.BlockSpec((tm, d), lambda i, j: (i + 1, 0)),
                  pl.BlockSpec((1, d), lambda i, j: (0, 0))] + w_specs(tf),
        out_specs=pl.BlockSpec((tm, d), lambda i, j: (i + out_row0, 0)),
        out_shape=jax.ShapeDtypeStruct((t if in_place else t - tm, d), F32),
        scratch_shapes=[pltpu.VMEM((tm, d), BF16)],
        input_output_aliases=alias,
        compiler_params=_cparams(("parallel", "arbitrary")),
        name="ffn_rest",
    )(y if in_place else x, gain, wg_b, wu_b, wd_b)
    return rest if in_place else (y, rest)


_N_CMP = 0
_N_KS = _N_CMP + 2 * NSA_KV_WIDTH
_N_KW = _N_KS + NSA_KV_WIDTH
_N_CQ = _N_KW + NSA_KV_WIDTH
_N_CKV = _N_CQ + MLA_Q_RANK
_N_KR = _N_CKV + MLA_KV_RANK
_N_END = _N_KR + 2 * LANES
_T_Q = 0
_T_VS = _T_Q + NSA_WIDTH
_T_VW = _T_VS + NSA_KV_WIDTH
_T_GATE = _T_VW + NSA_KV_WIDTH
_T_END = _T_GATE + LANES


def _two_part_specs(parts, tm):
    first, rest = parts
    nb_first = first.shape[0] // tm
    d = first.shape[1]
    return nb_first, [pl.BlockSpec((tm, d), lambda i: (jnp.minimum(i, nb_first - 1), 0)),
                      pl.BlockSpec((tm, d), lambda i: (jnp.maximum(i - nb_first, 0), 0))]


def _two_part_rows(first_ref, rest_ref, nb_first):
    return jnp.where(pl.program_id(0) < nb_first, first_ref[...], rest_ref[...])


def _proj_kernel(xa_ref, xb_ref, g_ref, wn_ref, wt_ref, qg_ref, kg_ref, qag_ref, kvag_ref,
                 q_t_ref, cmp_ref, ks_ref, vs_t_ref, kw_ref, vw_t_ref, gate_t_ref, cq_ref, ckv_ref, kr_ref, *,
                 nb_first):
    tq = q_t_ref.shape[2]
    h = _rms(_two_part_rows(xa_ref, xb_ref, nb_first), g_ref[...]).astype(BF16)
    n_cuts = (_N_CMP, _N_CQ, _N_END)
    t_cuts = (_T_Q, _T_VS, _T_END)
    p_n = [_dot(h, wn_ref[:, a:b]) for a, b in zip(n_cuts[:-1], n_cuts[1:])]
    p_t = [_dot_nt(wt_ref[a:b, :], h) for a, b in zip(t_cuts[:-1], t_cuts[1:])]

    def proj(lo, hi):
        c = max(i for i, a in enumerate(n_cuts[:-1]) if a <= lo)
        return p_n[c][:, lo - n_cuts[c]:hi - n_cuts[c]]

    def proj_t(lo, hi):
        c = max(i for i, a in enumerate(t_cuts[:-1]) if a <= lo)
        return p_t[c][lo - t_cuts[c]:hi - t_cuts[c], :]

    qscale = HEAD_DIM ** -0.5
    for hd in range(NSA_HEADS):
        q = proj_t(_T_Q + hd * HEAD_DIM, _T_Q + (hd + 1) * HEAD_DIM)
        inv = lax.rsqrt(jnp.mean(q * q, axis=0, keepdims=True) + EPS) * qscale
        qn = (q * inv * qg_ref[...]).astype(BF16)
        for c in range(q_t_ref.shape[0]):
            q_t_ref[c, hd * HEAD_DIM:(hd + 1) * HEAD_DIM, :] = qn[:, c * tq:(c + 1) * tq]
    for c in range(cmp_ref.shape[0]):
        cmp_ref[c] = proj(_N_CMP + c * HEAD_DIM, _N_CMP + (c + 1) * HEAD_DIM)
    for g in range(NSA_KV_HEADS):
        sl = slice(g * HEAD_DIM, (g + 1) * HEAD_DIM)
        ks_ref[:, sl] = _rms(proj(_N_KS + g * HEAD_DIM, _N_KS + (g + 1) * HEAD_DIM), kg_ref[1:2, :]).astype(BF16)
        kw_ref[:, sl] = _rms(proj(_N_KW + g * HEAD_DIM, _N_KW + (g + 1) * HEAD_DIM), kg_ref[2:3, :]).astype(BF16)
    vs_t_ref[...] = proj_t(_T_VS, _T_VW).astype(BF16)
    vw_t_ref[...] = proj_t(_T_VW, _T_GATE).astype(BF16)
    cq_ref[...] = _rms(proj(_N_CQ, _N_CKV), qag_ref[...]).astype(BF16)
    ckv_ref[...] = _rms(proj(_N_CKV, _N_KR), kvag_ref[...]).astype(BF16)
    kr_ref[...] = proj(_N_KR, _N_END)
    gates = _sigmoid(proj_t(_T_GATE, _T_END))
    for c in range(gate_t_ref.shape[0]):
        gate_t_ref[c] = gates[:, c * tq:(c + 1) * tq]


def _proj(x_parts, gain, w_n, w_t, q_gain, k_gains, qa_gain, kva_gain, *, tm=512, tq=256):
    t = x_parts[0].shape[0] + x_parts[1].shape[0]
    nb_first, x_specs = _two_part_specs(x_parts, tm)
    q_gain = jnp.broadcast_to(q_gain.reshape(HEAD_DIM, 1), (HEAD_DIM, tm))

    def full(a):
        return pl.BlockSpec(a.shape, lambda i: (0,) * a.ndim)

    def rows(width):
        return (t, width), pl.BlockSpec((tm, width), lambda i: (i, 0))

    def cols(height):
        return (height, t), pl.BlockSpec((height, tm), lambda i: (0, i))

    def q_tiles(height):
        return (t // tq, height, tq), pl.BlockSpec((tm // tq, height, tq), lambda i: (i, 0, 0))

    n_cmp_cols = 2 * NSA_KV_WIDTH // HEAD_DIM
    cmp_out = (n_cmp_cols, t, HEAD_DIM), pl.BlockSpec((n_cmp_cols, tm, HEAD_DIM), lambda i: (0, i, 0))
    outs = [(q_tiles(NSA_WIDTH), BF16), (cmp_out, F32), (rows(NSA_KV_WIDTH), BF16),
            (cols(NSA_KV_WIDTH), BF16), (rows(NSA_KV_WIDTH), BF16), (cols(NSA_KV_WIDTH), BF16),
            (q_tiles(LANES), F32), (rows(MLA_Q_RANK), BF16), (rows(MLA_KV_RANK), BF16), (rows(2 * LANES), F32)]
    return pl.pallas_call(
        functools.partial(_proj_kernel, nb_first=nb_first),
        grid=(t // tm,),
        in_specs=x_specs + [full(gain), full(w_n), full(w_t), full(q_gain), full(k_gains), full(qa_gain),
                            full(kva_gain)],
        out_specs=[spec for (_, spec), _ in outs],
        out_shape=[jax.ShapeDtypeStruct(shape, dt) for (shape, _), dt in outs],
        compiler_params=_cparams(("parallel",)),
        name="proj",
    )(*x_parts, gain, w_n, w_t, q_gain, k_gains, qa_gain, kva_gain)


def _gelu_tanh(x):
    return 0.5 * x * (1.0 + jnp.tanh(np.sqrt(2.0 / np.pi) * (x + 0.044715 * (x * x * x))))


def _compress_kernel(x_ref, w1k_ref, w2k_ref, pk_ref, w1v_ref, w2v_ref, pv_ref, kg_ref, kc_ref, vc_t_ref):
    half = CMP_STRIDE * HEAD_DIM
    nrow = x_ref.shape[1] // CMP_STRIDE
    for which, (w1_ref, w2_ref, p_ref) in enumerate(((w1k_ref, w2k_ref, pk_ref), (w1v_ref, w2v_ref, pv_ref))):
        pos_bias = _dot(jnp.broadcast_to(p_ref[...], (8, 2 * half)).astype(BF16), w1_ref[...])[0:1, :]
        for g in range(NSA_KV_HEADS):
            col = which * NSA_KV_HEADS + g
            xg = jnp.concatenate(
                [x_ref[col, pl.ds(l, nrow, stride=CMP_STRIDE), :].astype(BF16) for l in range(CMP_STRIDE)], axis=1)
            first = _dot(xg, w1_ref[0:half, :])
            second = _dot(xg, w1_ref[half:2 * half, :])
            pre = first + pltpu.roll(second, nrow - 1, 0) + pos_bias
            hidden = _gelu_tanh(pre).astype(BF16)
            if which == 0:
                kc_ref[0, g] = _rms(_dot(hidden, w2_ref[...]), kg_ref[0:1, :]).astype(BF16)
            else:
                vc_t_ref[0, g] = _dot_nt(w2_ref[...], hidden).astype(BF16)


def _compress(xc, seq, w1k, w2k, pk, w1v, w2v, pv, k_gains):
    ncol, t, width = xc.shape
    b, nrow = t // seq, seq // CMP_STRIDE

    def full(a):
        return pl.BlockSpec(a.shape, lambda i: (0,) * a.ndim)

    out_sds = jax.ShapeDtypeStruct((b, NSA_KV_HEADS, nrow, HEAD_DIM), BF16)
    out_spec = pl.BlockSpec((1, NSA_KV_HEADS, nrow, HEAD_DIM), lambda i: (i, 0, 0, 0))
    return pl.pallas_call(
        _compress_kernel,
        grid=(b,),
        in_specs=[pl.BlockSpec((ncol, seq, width), lambda i: (0, i, 0)),
                  full(w1k), full(w2k), full(pk), full(w1v), full(w2v), full(pv), full(k_gains)],
        out_specs=[out_spec, out_spec],
        out_shape=[out_sds, out_sds],
        compiler_params=_cparams(("parallel",)),
        name="compress",
    )(xc, w1k, w2k, pk, w1v, w2v, pv, k_gains)


def _softmax_cols(s):
    m = jnp.max(s, axis=0, keepdims=True)
    p = jnp.exp(s - m)
    return m, jnp.sum(p, axis=0, keepdims=True), p


def _nsa_kernel(q_t_ref, kc_ref, vc_t_ref, ks_ref, vs_t_ref, kw_ref, vw_t_ref, gate_t_ref, kfeat_ref, cfeat_ref,
                slope_ref, o_ref, *, tq, ck):
    qi = pl.program_id(1)
    q0 = qi * tq
    ng = NSA_KV_HEADS
    wide = NSA_GROUP * tq
    n_sel = ks_ref.shape[1] // SEL_BLOCK
    ncmp = kc_ref.shape[2]
    n_win = WINDOW + tq

    def tile4(a):
        return jnp.concatenate([a] * NSA_GROUP, axis=1)

    def masked(s, mask, fill=NEG):
        return jnp.concatenate([jnp.where(mask, s[:, r * tq:(r + 1) * tq], fill) for r in range(NSA_GROUP)], axis=1)

    def q_aug(g, extra):
        q = jnp.concatenate([q_t_ref[0, (g * NSA_GROUP + r) * HEAD_DIM:(g * NSA_GROUP + r + 1) * HEAD_DIM, :]
                             for r in range(NSA_GROUP)], axis=1)
        rows = [q, slope_ref[g]]
        used = slope_ref.shape[1]
        if extra is not None:
            rows.append(extra)
            used += extra.shape[0]
        rows.append(jnp.zeros((HEAD_DIM - used, wide), BF16))
        return jnp.concatenate(rows, axis=0)

    def gsl(g):
        return slice(g * HEAD_DIM, (g + 1) * HEAD_DIM)

    lane_q = q0 + lax.broadcasted_iota(jnp.int32, (1, tq), 1)

    qa_plain = [q_aug(g, None) for g in range(ng)]
    w0 = pl.multiple_of(jnp.maximum(q0 - WINDOW, 0), tq)
    s_cmp = [_dot(jnp.concatenate([kc_ref[0, g], cfeat_ref[...]], axis=1), qa_plain[g]) for g in range(ng)]
    s_win = [_dot(jnp.concatenate([kw_ref[0, pl.ds(w0, n_win), gsl(g)], kfeat_ref[pl.ds(w0, n_win), :]], axis=1),
                  qa_plain[g]) for g in range(ng)]

    n_row = lax.broadcasted_iota(jnp.int32, (ncmp, tq), 0)
    mask_c = n_row * CMP_STRIDE + (CMP_BLOCK - 1) <= lane_q
    s_row = lax.broadcasted_iota(jnp.int32, (n_sel, ncmp), 0)
    n_col = lax.broadcasted_iota(jnp.int32, (n_sel, ncmp), 1)
    overlap_t = jnp.where((n_col * CMP_STRIDE < (s_row + 1) * SEL_BLOCK)
                          & (n_col * CMP_STRIDE + (CMP_BLOCK - 1) >= s_row * SEL_BLOCK), 1.0, 0.0).astype(BF16)
    p_cmp, imp = [], []
    for g in range(ng):
        s = masked(s_cmp[g], mask_c)
        m = jnp.max(s, axis=0, keepdims=True)
        p = masked(jnp.exp(s - m), mask_c, 0.0)
        l = jnp.sum(p, axis=0, keepdims=True)
        p = p / jnp.where(l > 0.0, l, 1.0)
        p_cmp.append(p.astype(BF16))
        p_sum = p[:, 0:tq]
        for r in range(1, NSA_GROUP):
            p_sum = p_sum + p[:, r * tq:(r + 1) * tq]
        p_hi = p_sum.astype(BF16)
        p_lo = (p_sum - p_hi.astype(F32)).astype(BF16)
        imp.append(_dot(overlap_t, p_hi) + _dot(overlap_t, p_lo))
    o_cmp = [_dot(vc_t_ref[0, g], p_cmp[g]) for g in range(ng)]

    blk = lax.broadcasted_iota(jnp.int32, (n_sel, tq), 0)
    blk_f = blk.astype(F32)
    forced = (blk == 0) | (blk == jnp.right_shift(lane_q, SEL_SHIFT))
    future = blk * SEL_BLOCK > lane_q
    penalty = []
    for g in range(ng):
        v = jnp.where(future, NEG, jnp.where(forced, FORCE_SCORE, imp[g]))
        unselected = jnp.full((n_sel, tq), NEG, F32)
        for _ in range(min(SEL_TOP, n_sel)):
            mx = jnp.max(v, axis=0, keepdims=True)
            first = jnp.min(jnp.where(v == mx, blk_f, float(n_sel)), axis=0, keepdims=True)
            pick = blk_f == first
            unselected = jnp.where(pick, 0.0, unselected)
            v = jnp.where(pick, -jnp.inf, v)
        penalty.append(tile4(unselected.astype(BF16)))

    rel = lax.broadcasted_iota(jnp.int32, (n_win, tq), 0) - lax.broadcasted_iota(jnp.int32, (n_win, tq), 1)
    off = q0 - w0
    mask_w = (rel <= off) & (rel > off - WINDOW)
    o_win = []
    p_win = []
    for g in range(ng):
        _, l, p = _softmax_cols(masked(s_win[g], mask_w))
        p_win.append((l, p.astype(BF16)))
    for g in range(ng):
        l, p = p_win[g]
        o_win.append(_dot(vw_t_ref[gsl(g), pl.ds(w0, n_win)], p) / l)

    partial = []
    for g in range(ng):
        for r in range(NSA_GROUP):
            hd = g * NSA_GROUP + r
            cs = slice(r * tq, (r + 1) * tq)
            partial.append(gate_t_ref[0, 3 * hd:3 * hd + 1, :] * o_cmp[g][:, cs]
                           + gate_t_ref[0, 3 * hd + 2:3 * hd + 3, :] * o_win[g][:, cs])

    qa_sel = [q_aug(g, penalty[g]) for g in range(ng)]
    rel_d = lax.broadcasted_iota(jnp.int32, (ck, tq), 0) - lax.broadcasted_iota(jnp.int32, (ck, tq), 1)

    def select_and_store(n_before):
        nk = n_before + ck
        mask_d = rel_d <= q0 - n_before
        ss = [_dot(jnp.concatenate([ks_ref[0, 0:nk, gsl(g)], kfeat_ref[0:nk, :]], axis=1), qa_sel[g])
              for g in range(ng)]
        for g in range(ng):
            s_diag = masked(ss[g][n_before:nk], mask_d)
            m = jnp.max(s_diag, axis=0, keepdims=True)
            if n_before:
                m = jnp.maximum(m, jnp.max(ss[g][0:n_before], axis=0, keepdims=True))
            p_diag = jnp.exp(s_diag - m)
            l = jnp.sum(p_diag, axis=0, keepdims=True)
            p = p_diag.astype(BF16)
            if n_before:
                p_before = jnp.exp(ss[g][0:n_before] - m)
                l = l + jnp.sum(p_before, axis=0, keepdims=True)
                p = jnp.concatenate([p_before.astype(BF16), p], axis=0)
            o_sel = _dot(vs_t_ref[gsl(g), 0:nk], p) / l
            for r in range(NSA_GROUP):
                hd = g * NSA_GROUP + r
                mixed = partial[hd] + gate_t_ref[0, 3 * hd + 1:3 * hd + 2, :] * o_sel[:, r * tq:(r + 1) * tq]
                o_ref[0, :, hd * HEAD_DIM:(hd + 1) * HEAD_DIM] = mixed.T

    per = ck // tq
    for v in range(ks_ref.shape[1] // ck):
        pl.when(qi // per == v)(functools.partial(select_and_store, v * ck))


def _nsa(q_t, kc, vc_t, ks, vs_t, kw, vw_t, gate_t, *, tq=256, ck=256):
    b, s, _ = ks.shape
    nq = s // tq
    pos = np.arange(s)
    kfeat = np.zeros((s, LANES), np.float32)
    kfeat[:, 0] = POS_SPLIT * (pos // POS_SPLIT)
    kfeat[:, 1] = pos % POS_SPLIT
    kfeat[pos, FEAT_ROWS + pos // SEL_BLOCK] = 1.0
    ncmp = kc.shape[2]
    cfeat = np.zeros((ncmp, LANES), np.float32)
    cfeat[:, 0] = CMP_STRIDE * np.arange(ncmp)
    cfeat[:, 1] = 0.5 * (CMP_BLOCK - 1)
    slope = np.zeros((NSA_KV_HEADS, FEAT_ROWS, NSA_GROUP * tq), np.float32)
    for hd in range(NSA_HEADS):
        g, r = divmod(hd, NSA_GROUP)
        slope[g, 0:2, r * tq:(r + 1) * tq] = 2.0 ** (-8.0 * (hd + 1) / NSA_HEADS)
    assert FEAT_ROWS + s // SEL_BLOCK <= LANES and s // SEL_BLOCK % FEAT_ROWS == 0 and s % ck == 0 and ck % tq == 0

    def full(a):
        return pl.BlockSpec(a.shape, lambda i, j: (0,) * a.ndim)

    def per_b(a):
        return pl.BlockSpec((1,) + a.shape[1:], lambda i, j: (i,) + (0,) * (a.ndim - 1))

    def per_b_cols(a):
        return pl.BlockSpec((a.shape[0], s), lambda i, j: (0, i))

    def q_cols(a):
        assert a.shape[0] == b * nq and a.shape[2] == tq
        return pl.BlockSpec((1, a.shape[1], tq), lambda i, j: (i * nq + j, 0, 0))

    consts = [jnp.asarray(kfeat, BF16), jnp.asarray(cfeat, BF16), jnp.asarray(slope, BF16)]
    return pl.pallas_call(
        functools.partial(_nsa_kernel, tq=tq, ck=ck),
        grid=(b, nq),
        in_specs=[q_cols(q_t), per_b(kc), per_b(vc_t), per_b(ks), per_b_cols(vs_t), per_b(kw), per_b_cols(vw_t),
                  q_cols(gate_t)] + [full(c) for c in consts],
        out_specs=pl.BlockSpec((1, tq, NSA_WIDTH), lambda i, j: (i, j, 0)),
        out_shape=jax.ShapeDtypeStruct((b, s, NSA_WIDTH), F32),
        compiler_params=_cparams(("parallel", "arbitrary")),
        name="nsa",
    )(q_t, kc, vc_t, ks, vs_t, kw, vw_t, gate_t, *consts)


def _mla_prep_kernel(cq_ref, ckv_ref, kr_ref, cos_ref, sin_ref, cos_t_ref, sin_t_ref, wq_t_ref, wqr_t_ref, wk_ref,
                     wv_t_ref, qgn_ref, qgr_ref, qgt_ref, kgn_ref, kgr_ref, kgt_ref, q_t_ref, k_ref, v_t_ref):
    cq = cq_ref[0]
    ckv = ckv_ref[0]
    k_rope = kr_ref[0, :, 0:LANES]
    k_rot = kr_ref[0, :, LANES:2 * LANES]
    k_rope_ssq = jnp.sum(k_rope * k_rope, axis=-1, keepdims=True)
    k_rope_emb = k_rope * kgr_ref[...] * cos_ref[...] + k_rot * kgt_ref[...] * sin_ref[...]
    q_cos = qgr_ref[...] * cos_t_ref[...]
    q_sin = qgt_ref[...] * sin_t_ref[...]
    qscale = MLA_QK_DIM ** -0.5
    hg = MLA_HEAD_GROUP
    for hd in range(MLA_HEADS):
        if hd % hg == 0:
            qx_all = _dot_nt(wq_t_ref[hd * MLA_QK_PAD:(hd + hg) * MLA_QK_PAD, :], cq)
            q_rot_all = _dot_nt(wqr_t_ref[hd * LANES:(hd + hg) * LANES, :], cq)
            k_nope_all = _dot(ckv, wk_ref[:, hd * LANES:(hd + hg) * LANES])
            v_all = _dot_nt(wv_t_ref[hd * LANES:(hd + hg) * LANES, :], ckv)
        lo = hd % hg
        q_nope = qx_all[lo * MLA_QK_PAD:lo * MLA_QK_PAD + LANES]
        q_rope = qx_all[lo * MLA_QK_PAD + LANES:(lo + 1) * MLA_QK_PAD]
        q_rot = q_rot_all[lo * LANES:(lo + 1) * LANES]
        ssq = jnp.sum(q_nope * q_nope, axis=0, keepdims=True) + jnp.sum(q_rope * q_rope, axis=0, keepdims=True)
        inv = lax.rsqrt(ssq / MLA_QK_DIM + EPS) * qscale
        q_t_ref[0, 0, hd, 0:LANES, :] = (q_nope * inv * qgn_ref[...]).astype(BF16)
        q_t_ref[0, 0, hd, LANES:2 * LANES, :] = ((q_rope * q_cos + q_rot * q_sin) * inv).astype(BF16)

        k_nope = k_nope_all[:, lo * LANES:(lo + 1) * LANES]
        ssq = jnp.sum(k_nope * k_nope, axis=-1, keepdims=True) + k_rope_ssq
        inv = lax.rsqrt(ssq / MLA_QK_DIM + EPS)
        k_ref[0, hd, :, 0:LANES] = (k_nope * inv * kgn_ref[...]).astype(BF16)
        k_ref[0, hd, :, LANES:2 * LANES] = (k_rope_emb * inv).astype(BF16)
        v_t_ref[0, hd] = v_all[lo * LANES:(lo + 1) * LANES].astype(BF16)


def _mla_prep(cq, ckv, kr, tables, weights, q_gains, k_gains, *, tm=256):
    b, s, _ = cq.shape
    cos, sin = tables
    q_gains = tuple(jnp.broadcast_to(g.reshape(LANES, 1), (LANES, tm)) for g in q_gains)

    def full(a):
        return pl.BlockSpec(a.shape, lambda i, j: (0,) * a.ndim)

    def rows(width):
        return pl.BlockSpec((1, tm, width), lambda i, j: (i, j, 0))

    pos_rows = pl.BlockSpec((tm, LANES), lambda i, j: (j, 0))
    pos_cols = pl.BlockSpec((LANES, tm), lambda i, j: (0, j))
    return pl.pallas_call(
        _mla_prep_kernel,
        grid=(b, s // tm),
        in_specs=[rows(MLA_Q_RANK), rows(MLA_KV_RANK), rows(2 * LANES), pos_rows, pos_rows, pos_cols, pos_cols]
                 + [full(w) for w in weights] + [full(g) for g in q_gains + k_gains],
        out_specs=[pl.BlockSpec((1, 1, MLA_HEADS, MLA_QK_PAD, tm), lambda i, j: (i, j, 0, 0, 0)),
                   pl.BlockSpec((1, MLA_HEADS, tm, MLA_QK_PAD), lambda i, j: (i, 0, j, 0)),
                   pl.BlockSpec((1, MLA_HEADS, MLA_V_DIM, tm), lambda i, j: (i, 0, 0, j))],
        out_shape=[jax.ShapeDtypeStruct((b, s // tm, MLA_HEADS, MLA_QK_PAD, tm), BF16),
                   jax.ShapeDtypeStruct((b, MLA_HEADS, s, MLA_QK_PAD), BF16),
                   jax.ShapeDtypeStruct((b, MLA_HEADS, MLA_V_DIM, s), BF16)],
        compiler_params=_cparams(("parallel", "parallel")),
        name="mla_prep",
    )(cq, ckv, kr, cos, sin, cos.T, sin.T, *weights, *q_gains, *k_gains)


def _mla_attn_kernel(q_t_ref, k_ref, v_t_ref, o_ref, *, tq, hb, n_q):
    qi = pl.program_id(2)
    d_v = v_t_ref.shape[2]
    tri = lax.broadcasted_iota(jnp.int32, (tq, tq), 0) <= lax.broadcasted_iota(jnp.int32, (tq, tq), 1)

    def attend(n_before):
        nk = n_before + tq
        ss = [_dot(k_ref[0, h, 0:nk, :], q_t_ref[0, 0, h]) for h in range(hb)]
        for h in range(hb):
            s_diag = jnp.where(tri, ss[h][n_before:nk], NEG)
            m = jnp.max(s_diag, axis=0, keepdims=True)
            if n_before:
                m = jnp.maximum(m, jnp.max(ss[h][0:n_before], axis=0, keepdims=True))
            p_diag = jnp.exp(s_diag - m)
            l = jnp.sum(p_diag, axis=0, keepdims=True)
            p = p_diag.astype(BF16)
            if n_before:
                p_before = jnp.exp(ss[h][0:n_before] - m)
                l = l + jnp.sum(p_before, axis=0, keepdims=True)
                p = jnp.concatenate([p_before.astype(BF16), p], axis=0)
            o_ref[0, :, h * d_v:(h + 1) * d_v] = (_dot(v_t_ref[0, h, :, 0:nk], p) / l).T

    for v in range(n_q):
        pl.when(qi == v)(functools.partial(attend, v * tq))


def _mla_attn(q_t, k, v_t, *, tq=256, hb=8):
    b, h, s, dq = k.shape
    dv = v_t.shape[2]
    assert q_t.shape == (b, s // tq, h, dq, tq)
    return pl.pallas_call(
        functools.partial(_mla_attn_kernel, tq=tq, hb=hb, n_q=s // tq),
        grid=(b, h // hb, s // tq),
        in_specs=[pl.BlockSpec((1, 1, hb, dq, tq), lambda i, j, l: (i, l, j, 0, 0)),
                  pl.BlockSpec((1, hb, s, dq), lambda i, j, l: (i, j, 0, 0)),
                  pl.BlockSpec((1, hb, dv, s), lambda i, j, l: (i, j, 0, 0))],
        out_specs=pl.BlockSpec((1, tq, hb * dv), lambda i, j, l: (i, l, j)),
        out_shape=jax.ShapeDtypeStruct((b, s, h * dv), F32),
        compiler_params=_cparams(("parallel", "parallel", "arbitrary")),
        name="mla_attn",
    )(q_t, k, v_t)


def _out_kernel(xa_ref, xb_ref, oa_ref, ob_ref, ga_ref, gb_ref, w_ref, o_ref, *, nb_first):
    oa = _rms(oa_ref[...], ga_ref[...]).astype(BF16)
    ob = _rms(ob_ref[...], gb_ref[...]).astype(BF16)
    o_ref[...] = (_two_part_rows(xa_ref, xb_ref, nb_first)
                  + _dot(jnp.concatenate([oa, ob], axis=1), w_ref[...]))


def _out_proj(x_parts, oa, ob, gain_a, gain_b, w, *, tm=512):
    t, d = oa.shape[0], x_parts[0].shape[1]
    nb_first, x_specs = _two_part_specs(x_parts, tm)

    def rows(width):
        return pl.BlockSpec((tm, width), lambda i: (i, 0))

    def full(a):
        return pl.BlockSpec(a.shape, lambda i: (0,) * a.ndim)

    return pl.pallas_call(
        functools.partial(_out_kernel, nb_first=nb_first),
        grid=(t // tm,),
        in_specs=x_specs + [rows(oa.shape[1]), rows(ob.shape[1]), full(gain_a), full(gain_b), full(w)],
        out_specs=rows(d),
        out_shape=jax.ShapeDtypeStruct((t, d), F32),
        compiler_params=_cparams(("parallel",)),
        name="out_proj",
    )(*x_parts, oa, ob, gain_a, gain_b, w)


def _rot_cols(w):
    half = w.shape[1] // 2
    return jnp.concatenate([-w[:, half:], w[:, :half]], axis=1)


def _layout_w_in(w_in):
    d = w_in.shape[0]
    off = np.cumsum([0, NSA_WIDTH] + [NSA_KV_WIDTH] * 6 + [NSA_HEADS * 3, MLA_Q_RANK, MLA_KV_RANK, MLA_ROPE_DIM])
    q, kc, vc, ks, vs, kw, vw, gt, cq, ckv, kr = [w_in[:, off[i]:off[i + 1]] for i in range(11)]
    zr = jnp.zeros((d, LANES - MLA_ROPE_DIM), w_in.dtype)
    zg = jnp.zeros((d, LANES - NSA_HEADS * 3), w_in.dtype)
    w_n = jnp.concatenate([kc, vc, ks, kw, cq, ckv, kr, zr, _rot_cols(kr), zr], axis=1).astype(BF16)
    w_t = jnp.concatenate([q, vs, vw, gt, zg], axis=1).astype(BF16).T
    return w_n, w_t


def _layout_mla_weights(w_uq, w_ukv):
    r = w_uq.shape[0]
    zr = jnp.zeros((r, LANES - MLA_ROPE_DIM), w_uq.dtype)
    main, rot, k_cols, v_cols = [], [], [], []
    for hd in range(MLA_HEADS):
        base = hd * MLA_QK_DIM
        rope = w_uq[:, base + MLA_NOPE_DIM:base + MLA_QK_DIM]
        main += [w_uq[:, base:base + MLA_NOPE_DIM], rope, zr]
        rot += [_rot_cols(rope), zr]
        kv = hd * (MLA_NOPE_DIM + MLA_V_DIM)
        k_cols.append(w_ukv[:, kv:kv + MLA_NOPE_DIM])
        v_cols.append(w_ukv[:, kv + MLA_NOPE_DIM:kv + MLA_NOPE_DIM + MLA_V_DIM])
    cat = lambda parts: jnp.concatenate(parts, axis=1).astype(BF16)
    return cat(main).T, cat(rot).T, cat(k_cols), cat(v_cols).T


def _rope_gains(gain):
    half = MLA_ROPE_DIM // 2
    zr = jnp.zeros((LANES - MLA_ROPE_DIM,), gain.dtype)
    g_rope = gain[MLA_NOPE_DIM:]
    return (gain[None, :MLA_NOPE_DIM], jnp.concatenate([g_rope, zr])[None, :],
            jnp.concatenate([g_rope[half:], g_rope[:half], zr])[None, :])


def _rope_tables(s):
    half = MLA_ROPE_DIM // 2
    inv = ROPE_THETA ** (-jnp.arange(half, dtype=F32) / half)
    ang = jnp.arange(s).astype(F32)[:, None] * inv[None, :]
    zr = jnp.zeros((s, LANES - MLA_ROPE_DIM), F32)
    cos, sin = jnp.cos(ang), jnp.sin(ang)
    return jnp.concatenate([cos, cos, zr], axis=1), jnp.concatenate([sin, sin, zr], axis=1)


def kernel(x, ffn1_norm, ffn1_w_gate, ffn1_w_up, ffn1_w_down, mix_norm, w_in, nsa_q_norm, nsa_k_norm, nsa_cmp_pos_k, nsa_cmp_w1_k, nsa_cmp_w2_k, nsa_cmp_pos_v, nsa_cmp_w1_v, nsa_cmp_w2_v, mla_q_a_norm, mla_w_uq, mla_kv_a_norm, mla_w_ukv, mla_q_norm, mla_k_norm, out_norm_nsa, out_norm_mla, w_out, ffn2_norm, ffn2_w_gate, ffn2_w_up, ffn2_w_down):
    b, s, d = x.shape
    depth = ffn1_norm.shape[0]
    tables = _rope_tables(s)
    xt = x.reshape(b * s, d)
    for l in range(depth):
        x_parts = _ffn(xt, ffn1_norm[l][None, :], ffn1_w_gate[l], ffn1_w_up[l], ffn1_w_down[l], in_place=False)

        w_n, w_t = _layout_w_in(w_in[l])
        q_t, kv_cmp, ks, vs_t, kw, vw_t, gate_t, cq, ckv, kr = _proj(
            x_parts, mix_norm[l][None, :], w_n, w_t, nsa_q_norm[l], nsa_k_norm[l],
            mla_q_a_norm[l][None, :], mla_kv_a_norm[l][None, :])

        kc, vc_t = _compress(
            kv_cmp, s,
            nsa_cmp_w1_k[l].astype(BF16), nsa_cmp_w2_k[l].astype(BF16), nsa_cmp_pos_k[l].reshape(1, -1),
            nsa_cmp_w1_v[l].astype(BF16), nsa_cmp_w2_v[l].astype(BF16).T, nsa_cmp_pos_v[l].reshape(1, -1),
            nsa_k_norm[l])

        def bs(a):
            return a.reshape(b, s, a.shape[-1])

        o_a = _nsa(q_t, kc, vc_t, bs(ks), vs_t, bs(kw), vw_t, gate_t)

        q_m, k_m, v_m = _mla_prep(
            bs(cq), bs(ckv), bs(kr), tables, _layout_mla_weights(mla_w_uq[l], mla_w_ukv[l]),
            _rope_gains(mla_q_norm[l]), _rope_gains(mla_k_norm[l]))
        o_b = _mla_attn(q_m, k_m, v_m)

        xt = _out_proj(x_parts, o_a.reshape(b * s, NSA_WIDTH), o_b.reshape(b * s, MLA_WIDTH),
                       out_norm_nsa[l][None, :], out_norm_mla[l][None, :], w_out[l].astype(BF16))

        xt = _ffn(xt, ffn2_norm[l][None, :], ffn2_w_gate[l], ffn2_w_up[l], ffn2_w_down[l], in_place=True)
    return xt.reshape(b, s, d)
```

```python
import functools

import numpy as np
import jax
import jax.numpy as jnp
from jax import lax
from jax.experimental import pallas as pl
from jax.experimental.pallas import tpu as pltpu

F32 = jnp.float32
BF16 = jnp.bfloat16

HEAD_DIM = 128
NSA_HEADS = 8
NSA_KV_HEADS = 2
NSA_GROUP = NSA_HEADS // NSA_KV_HEADS
CMP_BLOCK = 32
CMP_STRIDE = 16
CMP_HIDDEN = 256
SEL_BLOCK = 64
SEL_SHIFT = 6
SEL_TOP = 8
WINDOW = 512
MLA_HEADS = 8
MLA_Q_RANK = 384
MLA_KV_RANK = 256
MLA_NOPE_DIM = 128
MLA_ROPE_DIM = 64
MLA_V_DIM = 128
MLA_QK_DIM = MLA_NOPE_DIM + MLA_ROPE_DIM
ROPE_THETA = 10000.0
EPS = 1e-6
NEG = -1e30
FORCE_SCORE = 1e4
NSA_WIDTH = NSA_HEADS * HEAD_DIM
MLA_WIDTH = MLA_HEADS * MLA_V_DIM
NSA_KV_WIDTH = NSA_KV_HEADS * HEAD_DIM

LANES = 128
MLA_QK_PAD = 2 * LANES
VMEM_LIMIT = 58 * 1024 * 1024
POS_SPLIT = 16
FEAT_ROWS = 16
NSA_Q_TILE = 256
MLA_Q_TILE = 256


def _cparams(sem):
    return pltpu.CompilerParams(dimension_semantics=sem, vmem_limit_bytes=VMEM_LIMIT)


def _rms(x, gain):
    return x * lax.rsqrt(jnp.mean(x * x, axis=-1, keepdims=True) + EPS) * gain


def _dot(a, b):
    return jnp.dot(a, b, preferred_element_type=F32)


def _dot_nt(a, b):
    return lax.dot_general(a, b, (((1,), (1,)), ((), ())), preferred_element_type=F32)


def _sigmoid(x):
    return 1.0 / (1.0 + jnp.exp(-x))


def _ffn_accumulate(x_ref, g_ref, weights, o_ref, h_ref):
    @pl.when(pl.program_id(1) == 0)
    def _():
        x = x_ref[...]
        h_ref[...] = _rms(x, g_ref[...]).astype(BF16)
        o_ref[...] = x

    wg, wu, wd = weights()
    h = h_ref[...]
    gate = _dot(h, wg)
    up = _dot(h, wu)
    act = gate * _sigmoid(gate) * up * 0.5
    o_ref[...] += _dot(act.astype(BF16), wd)


def _ffn_first_kernel(x_ref, g_ref, wg_ref, wu_ref, wd_ref, o_ref, wg_out, wu_out, wd_out, h_ref):
    def weights():
        wg, wu, wd = wg_ref[...].astype(BF16), wu_ref[...].astype(BF16), wd_ref[...].astype(BF16)
        wg_out[...], wu_out[...], wd_out[...] = wg, wu, wd
        return wg, wu, wd

    _ffn_accumulate(x_ref, g_ref, weights, o_ref, h_ref)


def _ffn_rest_kernel(x_ref, g_ref, wg_ref, wu_ref, wd_ref, o_ref, h_ref):
    _ffn_accumulate(x_ref, g_ref, lambda: (wg_ref[...], wu_ref[...], wd_ref[...]), o_ref, h_ref)


def _ffn(x, gain, wg, wu, wd, *, in_place, tm=1024, tf_first=256, tf=512):
    t, d = x.shape
    f = wg.shape[1]
    alias = {0: 0} if in_place else {}

    def w_specs(tfx):
        return [pl.BlockSpec((d, tfx), lambda i, j: (0, j)),
                pl.BlockSpec((d, tfx), lambda i, j: (0, j)),
                pl.BlockSpec((tfx, d), lambda i, j: (j, 0))]

    y, wg_b, wu_b, wd_b = pl.pallas_call(
        _ffn_first_kernel,
        grid=(1, f // tf_first),
        in_specs=[pl.BlockSpec((tm, d), lambda i, j: (0, 0), pipeline_mode=pl.Buffered(1)),
                  pl.BlockSpec((1, d), lambda i, j: (0, 0))] + w_specs(tf_first),
        out_specs=[pl.BlockSpec((tm, d), lambda i, j: (0, 0))] + w_specs(tf_first),
        out_shape=[jax.ShapeDtypeStruct((t if in_place else tm, d), F32), jax.ShapeDtypeStruct((d, f), BF16),
                   jax.ShapeDtypeStruct((d, f), BF16), jax.ShapeDtypeStruct((f, d), BF16)],
        scratch_shapes=[pltpu.VMEM((tm, d), BF16)],
        input_output_aliases=alias,
        compiler_params=_cparams(("arbitrary", "arbitrary")),
        name="ffn_first",
    )(x, gain, wg, wu, wd)

    out_row0 = 1 if in_place else 0
    rest = pl.pallas_call(
        _ffn_rest_kernel,
        grid=(t // tm - 1, f // tf),
        in_specs=[pl.BlockSpec((tm, d), lambda i, j: (i + 1, 0)),
                  pl.BlockSpec((1, d), lambda i, j: (0, 0))] + w_specs(tf),
        out_specs=pl.BlockSpec((tm, d), lambda i, j: (i + out_row0, 0)),
        out_shape=jax.ShapeDtypeStruct((t if in_place else t - tm, d), F32),
        scratch_shapes=[pltpu.VMEM((tm, d), BF16)],
        input_output_aliases=alias,
        compiler_params=_cparams(("parallel", "arbitrary")),
        name="ffn_rest",
    )(y if in_place else x, gain, wg_b, wu_b, wd_b)
    return rest if in_place else (y, rest)


_N_CMP = 0
_N_KS = _N_CMP + 2 * NSA_KV_WIDTH
_N_KW = _N_KS + NSA_KV_WIDTH
_N_CQ = _N_KW + NSA_KV_WIDTH
_N_CKV = _N_CQ + MLA_Q_RANK
_N_KR = _N_CKV + MLA_KV_RANK
_N_END = _N_KR + 2 * LANES
_T_Q = 0
_T_VS = _T_Q + NSA_WIDTH
_T_VW = _T_VS + NSA_KV_WIDTH
_T_GATE = _T_VW + NSA_KV_WIDTH
_T_END = _T_GATE + LANES


def _two_part_specs(parts, tm):
    first, rest = parts
    nb_first = first.shape[0] // tm
    d = first.shape[1]
    return nb_first, [pl.BlockSpec((tm, d), lambda i: (jnp.minimum(i, nb_first - 1), 0)),
                      pl.BlockSpec((tm, d), lambda i: (jnp.maximum(i - nb_first, 0), 0))]


def _two_part_rows(first_ref, rest_ref, nb_first):
    return jnp.where(pl.program_id(0) < nb_first, first_ref[...], rest_ref[...])


def _proj_kernel(xa_ref, xb_ref, g_ref, wn_ref, wt_ref, qg_ref, kg_ref, qag_ref, kvag_ref,
                 q_t_ref, cmp_ref, ks_ref, vs_t_ref, kw_ref, vw_t_ref, gate_t_ref, cq_ref, ckv_ref, kr_ref, *,
                 nb_first):
    tq = q_t_ref.shape[2]
    h = _rms(_two_part_rows(xa_ref, xb_ref, nb_first), g_ref[...]).astype(BF16)
    p_n = _dot(h, wn_ref[...])
    p_t = _dot_nt(wt_ref[...], h)

    def proj(lo, hi):
        return p_n[:, lo:hi]

    def proj_t(lo, hi):
        return p_t[lo:hi, :]

    qscale = HEAD_DIM ** -0.5
    for hd in range(NSA_HEADS):
        q = proj_t(_T_Q + hd * HEAD_DIM, _T_Q + (hd + 1) * HEAD_DIM)
        inv = lax.rsqrt(jnp.mean(q * q, axis=0, keepdims=True) + EPS) * qscale
        qn = (q * inv * qg_ref[...]).astype(BF16)
        for c in range(q_t_ref.shape[0]):
            q_t_ref[c, hd * HEAD_DIM:(hd + 1) * HEAD_DIM, :] = qn[:, c * tq:(c + 1) * tq]
    for c in range(cmp_ref.shape[0]):
        cmp_ref[c] = proj(_N_CMP + c * HEAD_DIM, _N_CMP + (c + 1) * HEAD_DIM)
    for g in range(NSA_KV_HEADS):
        sl = slice(g * HEAD_DIM, (g + 1) * HEAD_DIM)
        ks_ref[:, sl] = _rms(proj(_N_KS + g * HEAD_DIM, _N_KS + (g + 1) * HEAD_DIM), kg_ref[1:2, :]).astype(BF16)
        kw_ref[:, sl] = _rms(proj(_N_KW + g * HEAD_DIM, _N_KW + (g + 1) * HEAD_DIM), kg_ref[2:3, :]).astype(BF16)
    vs_t_ref[...] = proj_t(_T_VS, _T_VW).astype(BF16)
    vw_t_ref[...] = proj_t(_T_VW, _T_GATE).astype(BF16)
    cq_ref[...] = _rms(proj(_N_CQ, _N_CKV), qag_ref[...]).astype(BF16)
    ckv_ref[...] = _rms(proj(_N_CKV, _N_KR), kvag_ref[...]).astype(BF16)
    kr_ref[...] = proj(_N_KR, _N_END)
    gates = _sigmoid(proj_t(_T_GATE, _T_END))
    for c in range(gate_t_ref.shape[0]):
        gate_t_ref[c] = gates[:, c * tq:(c + 1) * tq]


def _proj(x_parts, gain, w_n, w_t, q_gain, k_gains, qa_gain, kva_gain, *, tm=512, tq=NSA_Q_TILE):
    t = x_parts[0].shape[0] + x_parts[1].shape[0]
    nb_first, x_specs = _two_part_specs(x_parts, tm)
    q_gain = jnp.broadcast_to(q_gain.reshape(HEAD_DIM, 1), (HEAD_DIM, tm))

    def full(a):
        return pl.BlockSpec(a.shape, lambda i: (0,) * a.ndim)

    def rows(width):
        return (t, width), pl.BlockSpec((tm, width), lambda i: (i, 0))

    def cols(height):
        return (height, t), pl.BlockSpec((height, tm), lambda i: (0, i))

    def q_tiles(height):
        return (t // tq, height, tq), pl.BlockSpec((tm // tq, height, tq), lambda i: (i, 0, 0))

    n_cmp_cols = 2 * NSA_KV_WIDTH // HEAD_DIM
    cmp_out = (n_cmp_cols, t, HEAD_DIM), pl.BlockSpec((n_cmp_cols, tm, HEAD_DIM), lambda i: (0, i, 0))
    outs = [(q_tiles(NSA_WIDTH), BF16), (cmp_out, F32), (rows(NSA_KV_WIDTH), BF16),
            (cols(NSA_KV_WIDTH), BF16), (rows(NSA_KV_WIDTH), BF16), (cols(NSA_KV_WIDTH), BF16),
            (q_tiles(LANES), F32), (rows(MLA_Q_RANK), BF16), (rows(MLA_KV_RANK), BF16), (rows(2 * LANES), F32)]
    return pl.pallas_call(
        functools.partial(_proj_kernel, nb_first=nb_first),
        grid=(t // tm,),
        in_specs=x_specs + [full(gain), full(w_n), full(w_t), full(q_gain), full(k_gains), full(qa_gain),
                            full(kva_gain)],
        out_specs=[spec for (_, spec), _ in outs],
        out_shape=[jax.ShapeDtypeStruct(shape, dt) for (shape, _), dt in outs],
        compiler_params=_cparams(("parallel",)),
        name="proj",
    )(*x_parts, gain, w_n, w_t, q_gain, k_gains, qa_gain, kva_gain)


def _gelu_tanh(x):
    return 0.5 * x * (1.0 + jnp.tanh(np.sqrt(2.0 / np.pi) * (x + 0.044715 * (x * x * x))))


def _compress_kernel(x_ref, w1k_ref, w2k_ref, pk_ref, w1v_ref, w2v_ref, pv_ref, kg_ref, kc_ref, vc_t_ref):
    half = CMP_STRIDE * HEAD_DIM
    nrow = x_ref.shape[1] // CMP_STRIDE
    for which, (w1_ref, w2_ref, p_ref) in enumerate(((w1k_ref, w2k_ref, pk_ref), (w1v_ref, w2v_ref, pv_ref))):
        w1 = w1_ref[...].astype(BF16)
        w2 = w2_ref[...].astype(BF16)
        pos_bias = _dot(jnp.broadcast_to(p_ref[...], (8, 2 * half)).astype(BF16), w1)[0:1, :]
        for g in range(NSA_KV_HEADS):
            col = which * NSA_KV_HEADS + g
            xg = jnp.concatenate(
                [x_ref[col, pl.ds(l, nrow, stride=CMP_STRIDE), :].astype(BF16) for l in range(CMP_STRIDE)], axis=1)
            first = _dot(xg, w1[0:half, :])
            second = _dot(xg, w1[half:2 * half, :])
            pre = first + pltpu.roll(second, nrow - 1, 0) + pos_bias
            hidden = _gelu_tanh(pre).astype(BF16)
            if which == 0:
                kc_ref[0, g] = _rms(_dot(hidden, w2), kg_ref[0:1, :]).astype(BF16)
            else:
                vc_t_ref[0, g] = _dot_nt(w2, hidden).astype(BF16)


def _compress(xc, seq, w1k, w2k, pk, w1v, w2v, pv, k_gains):
    ncol, t, width = xc.shape
    b, nrow = t // seq, seq // CMP_STRIDE

    def full(a):
        return pl.BlockSpec(a.shape, lambda i: (0,) * a.ndim)

    out_sds = jax.ShapeDtypeStruct((b, NSA_KV_HEADS, nrow, HEAD_DIM), BF16)
    out_spec = pl.BlockSpec((1, NSA_KV_HEADS, nrow, HEAD_DIM), lambda i: (i, 0, 0, 0))
    return pl.pallas_call(
        _compress_kernel,
        grid=(b,),
        in_specs=[pl.BlockSpec((ncol, seq, width), lambda i: (0, i, 0)),
                  full(w1k), full(w2k), full(pk), full(w1v), full(w2v), full(pv), full(k_gains)],
        out_specs=[out_spec, out_spec],
        out_shape=[out_sds, out_sds],
        compiler_params=_cparams(("parallel",)),
        name="compress",
    )(xc, w1k, w2k, pk, w1v, w2v, pv, k_gains)


def _softmax_cols(s):
    m = jnp.max(s, axis=0, keepdims=True)
    p = jnp.exp(s - m)
    return m, jnp.sum(p, axis=0, keepdims=True), p


def _nsa_kernel(q_t_ref, kc_ref, vc_t_ref, ks_ref, vs_t_ref, kw_ref, vw_t_ref, gate_t_ref, kfeat_ref, cfeat_ref,
                slope_ref, o_ref, *, tq, ck):
    qi = pl.program_id(1)
    q0 = qi * tq
    ng = NSA_KV_HEADS
    wide = NSA_GROUP * tq
    n_sel = ks_ref.shape[1] // SEL_BLOCK
    ncmp = kc_ref.shape[2]
    n_win = WINDOW + tq

    def tile4(a):
        return jnp.concatenate([a] * NSA_GROUP, axis=1)

    def masked(s, mask, fill=NEG):
        return jnp.concatenate([jnp.where(mask, s[:, r * tq:(r + 1) * tq], fill) for r in range(NSA_GROUP)], axis=1)

    def q_aug(g, extra):
        q = jnp.concatenate([q_t_ref[0, (g * NSA_GROUP + r) * HEAD_DIM:(g * NSA_GROUP + r + 1) * HEAD_DIM, :]
                             for r in range(NSA_GROUP)], axis=1)
        rows = [q, slope_ref[g]]
        used = slope_ref.shape[1]
        if extra is not None:
            rows.append(extra)
            used += extra.shape[0]
        rows.append(jnp.zeros((HEAD_DIM - used, wide), BF16))
        return jnp.concatenate(rows, axis=0)

    def gsl(g):
        return slice(g * HEAD_DIM, (g + 1) * HEAD_DIM)

    lane_q = q0 + lax.broadcasted_iota(jnp.int32, (1, tq), 1)

    qa_plain = [q_aug(g, None) for g in range(ng)]
    w0 = pl.multiple_of(jnp.maximum(q0 - WINDOW, 0), tq)
    s_cmp = [_dot(jnp.concatenate([kc_ref[0, g], cfeat_ref[...]], axis=1), qa_plain[g]) for g in range(ng)]
    s_win = [_dot(jnp.concatenate([kw_ref[0, pl.ds(w0, n_win), gsl(g)], kfeat_ref[pl.ds(w0, n_win), :]], axis=1),
                  qa_plain[g]) for g in range(ng)]

    n_row = lax.broadcasted_iota(jnp.int32, (ncmp, tq), 0)
    mask_c = n_row * CMP_STRIDE + (CMP_BLOCK - 1) <= lane_q
    s_row = lax.broadcasted_iota(jnp.int32, (n_sel, ncmp), 0)
    n_col = lax.broadcasted_iota(jnp.int32, (n_sel, ncmp), 1)
    overlap_t = jnp.where((n_col * CMP_STRIDE < (s_row + 1) * SEL_BLOCK)
                          & (n_col * CMP_STRIDE + (CMP_BLOCK - 1) >= s_row * SEL_BLOCK), 1.0, 0.0).astype(BF16)
    p_cmp, imp = [], []
    for g in range(ng):
        s = masked(s_cmp[g], mask_c)
        m = jnp.max(s, axis=0, keepdims=True)
        p = masked(jnp.exp(s - m), mask_c, 0.0)
        l = jnp.sum(p, axis=0, keepdims=True)
        p = p / jnp.where(l > 0.0, l, 1.0)
        p_cmp.append(p.astype(BF16))
        p_sum = p[:, 0:tq]
        for r in range(1, NSA_GROUP):
            p_sum = p_sum + p[:, r * tq:(r + 1) * tq]
        p_hi = p_sum.astype(BF16)
        p_lo = (p_sum - p_hi.astype(F32)).astype(BF16)
        imp.append(_dot(overlap_t, p_hi) + _dot(overlap_t, p_lo))
    o_cmp = [_dot(vc_t_ref[0, g], p_cmp[g]) for g in range(ng)]

    blk = lax.broadcasted_iota(jnp.int32, (n_sel, tq), 0)
    blk_f = blk.astype(F32)
    forced = (blk == 0) | (blk == jnp.right_shift(lane_q, SEL_SHIFT))
    future = blk * SEL_BLOCK > lane_q
    penalty = []
    for g in range(ng):
        v = jnp.where(future, NEG, jnp.where(forced, FORCE_SCORE, imp[g]))
        unselected = jnp.full((n_sel, tq), NEG, F32)
        for _ in range(min(SEL_TOP, n_sel)):
            mx = jnp.max(v, axis=0, keepdims=True)
            first = jnp.min(jnp.where(v == mx, blk_f, float(n_sel)), axis=0, keepdims=True)
            pick = blk_f == first
            unselected = jnp.where(pick, 0.0, unselected)
            v = jnp.where(pick, -jnp.inf, v)
        penalty.append(tile4(unselected.astype(BF16)))

    rel = lax.broadcasted_iota(jnp.int32, (n_win, tq), 0) - lax.broadcasted_iota(jnp.int32, (n_win, tq), 1)
    off = q0 - w0
    mask_w = (rel <= off) & (rel > off - WINDOW)
    o_win = []
    p_win = []
    for g in range(ng):
        _, l, p = _softmax_cols(masked(s_win[g], mask_w))
        p_win.append((l, p.astype(BF16)))
    for g in range(ng):
        l, p = p_win[g]
        o_win.append(_dot(vw_t_ref[gsl(g), pl.ds(w0, n_win)], p) / l)

    partial = []
    for g in range(ng):
        for r in range(NSA_GROUP):
            hd = g * NSA_GROUP + r
            cs = slice(r * tq, (r + 1) * tq)
            partial.append(gate_t_ref[0, 3 * hd:3 * hd + 1, :] * o_cmp[g][:, cs]
                           + gate_t_ref[0, 3 * hd + 2:3 * hd + 3, :] * o_win[g][:, cs])

    qa_sel = [q_aug(g, penalty[g]) for g in range(ng)]
    rel_d = lax.broadcasted_iota(jnp.int32, (ck, tq), 0) - lax.broadcasted_iota(jnp.int32, (ck, tq), 1)

    def select_and_store(n_before):
        nk = n_before + ck
        mask_d = rel_d <= q0 - n_before
        ss = [_dot(jnp.concatenate([ks_ref[0, 0:nk, gsl(g)], kfeat_ref[0:nk, :]], axis=1), qa_sel[g])
              for g in range(ng)]
        for g in range(ng):
            s_diag = masked(ss[g][n_before:nk], mask_d)
            m = jnp.max(s_diag, axis=0, keepdims=True)
            if n_before:
                m = jnp.maximum(m, jnp.max(ss[g][0:n_before], axis=0, keepdims=True))
            p_diag = jnp.exp(s_diag - m)
            l = jnp.sum(p_diag, axis=0, keepdims=True)
            p = p_diag.astype(BF16)
            if n_before:
                p_before = jnp.exp(ss[g][0:n_before] - m)
                l = l + jnp.sum(p_before, axis=0, keepdims=True)
                p = jnp.concatenate([p_before.astype(BF16), p], axis=0)
            o_sel = _dot(vs_t_ref[gsl(g), 0:nk], p) / l
            for r in range(NSA_GROUP):
                hd = g * NSA_GROUP + r
                mixed = partial[hd] + gate_t_ref[0, 3 * hd + 1:3 * hd + 2, :] * o_sel[:, r * tq:(r + 1) * tq]
                o_ref[0, :, hd * HEAD_DIM:(hd + 1) * HEAD_DIM] = mixed.T

    per = ck // tq
    for v in range(ks_ref.shape[1] // ck):
        pl.when(qi // per == v)(functools.partial(select_and_store, v * ck))


def _nsa(q_t, kc, vc_t, ks, vs_t, kw, vw_t, gate_t, *, tq=NSA_Q_TILE, ck=256):
    b, s, _ = ks.shape
    nq = s // tq
    pos = np.arange(s)
    kfeat = np.zeros((s, LANES), np.float32)
    kfeat[:, 0] = POS_SPLIT * (pos // POS_SPLIT)
    kfeat[:, 1] = pos % POS_SPLIT
    kfeat[pos, FEAT_ROWS + pos // SEL_BLOCK] = 1.0
    ncmp = kc.shape[2]
    cfeat = np.zeros((ncmp, LANES), np.float32)
    cfeat[:, 0] = CMP_STRIDE * np.arange(ncmp)
    cfeat[:, 1] = 0.5 * (CMP_BLOCK - 1)
    slope = np.zeros((NSA_KV_HEADS, FEAT_ROWS, NSA_GROUP * tq), np.float32)
    for hd in range(NSA_HEADS):
        g, r = divmod(hd, NSA_GROUP)
        slope[g, 0:2, r * tq:(r + 1) * tq] = 2.0 ** (-8.0 * (hd + 1) / NSA_HEADS)
    assert FEAT_ROWS + s // SEL_BLOCK <= LANES and s // SEL_BLOCK % FEAT_ROWS == 0 and s % ck == 0 and ck % tq == 0

    def full(a):
        return pl.BlockSpec(a.shape, lambda i, j: (0,) * a.ndim)

    def per_b(a):
        return pl.BlockSpec((1,) + a.shape[1:], lambda i, j: (i,) + (0,) * (a.ndim - 1))

    def per_b_cols(a):
        return pl.BlockSpec((a.shape[0], s), lambda i, j: (0, i))

    def q_cols(a):
        assert a.shape[0] == b * nq and a.shape[2] == tq
        return pl.BlockSpec((1, a.shape[1], tq), lambda i, j: (i * nq + j, 0, 0))

    consts = [jnp.asarray(kfeat, BF16), jnp.asarray(cfeat, BF16), jnp.asarray(slope, BF16)]
    return pl.pallas_call(
        functools.partial(_nsa_kernel, tq=tq, ck=ck),
        grid=(b, nq),
        in_specs=[q_cols(q_t), per_b(kc), per_b(vc_t), per_b(ks), per_b_cols(vs_t), per_b(kw), per_b_cols(vw_t),
                  q_cols(gate_t)] + [full(c) for c in consts],
        out_specs=pl.BlockSpec((1, tq, NSA_WIDTH), lambda i, j: (i, j, 0)),
        out_shape=jax.ShapeDtypeStruct((b, s, NSA_WIDTH), F32),
        compiler_params=_cparams(("parallel", "arbitrary")),
        name="nsa",
    )(q_t, kc, vc_t, ks, vs_t, kw, vw_t, gate_t, *consts)


def _mla_prep_kernel(cq_ref, ckv_ref, kr_ref, cos_ref, sin_ref, cos_t_ref, sin_t_ref, wq_t_ref, wqr_t_ref, wk_ref,
                     wv_t_ref, qgn_ref, qgr_ref, qgt_ref, kgn_ref, kgr_ref, kgt_ref, q_t_ref, k_ref, v_t_ref):
    cq = cq_ref[0]
    ckv = ckv_ref[0]
    k_rope = kr_ref[0, :, 0:LANES]
    k_rot = kr_ref[0, :, LANES:2 * LANES]
    k_rope_ssq = jnp.sum(k_rope * k_rope, axis=-1, keepdims=True)
    k_rope_emb = k_rope * kgr_ref[...] * cos_ref[...] + k_rot * kgt_ref[...] * sin_ref[...]
    q_cos = qgr_ref[...] * cos_t_ref[...]
    q_sin = qgt_ref[...] * sin_t_ref[...]
    qscale = MLA_QK_DIM ** -0.5
    qx_all = _dot_nt(wq_t_ref[...], cq)
    q_rot_all = _dot_nt(wqr_t_ref[...], cq)
    k_nope_all = _dot(ckv, wk_ref[...])
    v_all = _dot_nt(wv_t_ref[...], ckv)
    for hd in range(MLA_HEADS):
        q_nope = qx_all[hd * MLA_QK_PAD:hd * MLA_QK_PAD + LANES]
        q_rope = qx_all[hd * MLA_QK_PAD + LANES:(hd + 1) * MLA_QK_PAD]
        q_rot = q_rot_all[hd * LANES:(hd + 1) * LANES]
        ssq = jnp.sum(q_nope * q_nope, axis=0, keepdims=True) + jnp.sum(q_rope * q_rope, axis=0, keepdims=True)
        inv = lax.rsqrt(ssq / MLA_QK_DIM + EPS) * qscale
        q_t_ref[0, 0, hd, 0:LANES, :] = (q_nope * inv * qgn_ref[...]).astype(BF16)
        q_t_ref[0, 0, hd, LANES:2 * LANES, :] = ((q_rope * q_cos + q_rot * q_sin) * inv).astype(BF16)

        k_nope = k_nope_all[:, hd * LANES:(hd + 1) * LANES]
        ssq = jnp.sum(k_nope * k_nope, axis=-1, keepdims=True) + k_rope_ssq
        inv = lax.rsqrt(ssq / MLA_QK_DIM + EPS)
        k_ref[0, hd, :, 0:LANES] = (k_nope * inv * kgn_ref[...]).astype(BF16)
        k_ref[0, hd, :, LANES:2 * LANES] = (k_rope_emb * inv).astype(BF16)
        v_t_ref[0, hd] = v_all[hd * LANES:(hd + 1) * LANES].astype(BF16)


def _mla_prep(cq, ckv, kr, tables, weights, q_gains, k_gains, *, tm=MLA_Q_TILE):
    b, s, _ = cq.shape
    cos, sin = tables
    q_gains = tuple(jnp.broadcast_to(g.reshape(LANES, 1), (LANES, tm)) for g in q_gains)

    def full(a):
        return pl.BlockSpec(a.shape, lambda i, j: (0,) * a.ndim)

    def rows(width):
        return pl.BlockSpec((1, tm, width), lambda i, j: (i, j, 0))

    pos_rows = pl.BlockSpec((tm, LANES), lambda i, j: (j, 0))
    pos_cols = pl.BlockSpec((LANES, tm), lambda i, j: (0, j))
    return pl.pallas_call(
        _mla_prep_kernel,
        grid=(b, s // tm),
        in_specs=[rows(MLA_Q_RANK), rows(MLA_KV_RANK), rows(2 * LANES), pos_rows, pos_rows, pos_cols, pos_cols]
                 + [full(w) for w in weights] + [full(g) for g in q_gains + k_gains],
        out_specs=[pl.BlockSpec((1, 1, MLA_HEADS, MLA_QK_PAD, tm), lambda i, j: (i, j, 0, 0, 0)),
                   pl.BlockSpec((1, MLA_HEADS, tm, MLA_QK_PAD), lambda i, j: (i, 0, j, 0)),
                   pl.BlockSpec((1, MLA_HEADS, MLA_V_DIM, tm), lambda i, j: (i, 0, 0, j))],
        out_shape=[jax.ShapeDtypeStruct((b, s // tm, MLA_HEADS, MLA_QK_PAD, tm), BF16),
                   jax.ShapeDtypeStruct((b, MLA_HEADS, s, MLA_QK_PAD), BF16),
                   jax.ShapeDtypeStruct((b, MLA_HEADS, MLA_V_DIM, s), BF16)],
        compiler_params=_cparams(("parallel", "parallel")),
        name="mla_prep",
    )(cq, ckv, kr, cos, sin, cos.T, sin.T, *weights, *q_gains, *k_gains)


def _mla_attn_kernel(q_t_ref, k_ref, v_t_ref, o_ref, *, tq, hb, n_q):
    qi = pl.program_id(2)
    d_v = v_t_ref.shape[2]
    tri = lax.broadcasted_iota(jnp.int32, (tq, tq), 0) <= lax.broadcasted_iota(jnp.int32, (tq, tq), 1)

    def attend(n_before):
        nk = n_before + tq
        ss = [_dot(k_ref[0, h, 0:nk, :], q_t_ref[0, 0, h]) for h in range(hb)]
        for h in range(hb):
            s_diag = jnp.where(tri, ss[h][n_before:nk], NEG)
            m = jnp.max(s_diag, axis=0, keepdims=True)
            if n_before:
                m = jnp.maximum(m, jnp.max(ss[h][0:n_before], axis=0, keepdims=True))
            p_diag = jnp.exp(s_diag - m)
            l = jnp.sum(p_diag, axis=0, keepdims=True)
            p = p_diag.astype(BF16)
            if n_before:
                p_before = jnp.exp(ss[h][0:n_before] - m)
                l = l + jnp.sum(p_before, axis=0, keepdims=True)
                p = jnp.concatenate([p_before.astype(BF16), p], axis=0)
            o_ref[0, :, h * d_v:(h + 1) * d_v] = (_dot(v_t_ref[0, h, :, 0:nk], p) / l).T

    for v in range(n_q):
        pl.when(qi == v)(functools.partial(attend, v * tq))


def _mla_attn(q_t, k, v_t, *, tq=MLA_Q_TILE, hb=MLA_HEADS):
    b, h, s, dq = k.shape
    dv = v_t.shape[2]
    assert q_t.shape == (b, s // tq, h, dq, tq)
    return pl.pallas_call(
        functools.partial(_mla_attn_kernel, tq=tq, hb=hb, n_q=s // tq),
        grid=(b, h // hb, s // tq),
        in_specs=[pl.BlockSpec((1, 1, hb, dq, tq), lambda i, j, l: (i, l, j, 0, 0)),
                  pl.BlockSpec((1, hb, s, dq), lambda i, j, l: (i, j, 0, 0)),
                  pl.BlockSpec((1, hb, dv, s), lambda i, j, l: (i, j, 0, 0))],
        out_specs=pl.BlockSpec((1, tq, hb * dv), lambda i, j, l: (i, l, j)),
        out_shape=jax.ShapeDtypeStruct((b, s, h * dv), F32),
        compiler_params=_cparams(("parallel", "parallel", "arbitrary")),
        name="mla_attn",
    )(q_t, k, v_t)


def _out_kernel(xa_ref, xb_ref, oa_ref, ob_ref, ga_ref, gb_ref, w_ref, o_ref, *, nb_first):
    oa = _rms(oa_ref[...], ga_ref[...]).astype(BF16)
    ob = _rms(ob_ref[...], gb_ref[...]).astype(BF16)
    o_ref[...] = (_two_part_rows(xa_ref, xb_ref, nb_first)
                  + _dot(jnp.concatenate([oa, ob], axis=1), w_ref[...]))


def _out_proj(x_parts, oa, ob, gain_a, gain_b, w, *, tm=512):
    t, d = oa.shape[0], x_parts[0].shape[1]
    nb_first, x_specs = _two_part_specs(x_parts, tm)

    def rows(width):
        return pl.BlockSpec((tm, width), lambda i: (i, 0))

    def full(a):
        return pl.BlockSpec(a.shape, lambda i: (0,) * a.ndim)

    return pl.pallas_call(
        functools.partial(_out_kernel, nb_first=nb_first),
        grid=(t // tm,),
        in_specs=x_specs + [rows(oa.shape[1]), rows(ob.shape[1]), full(gain_a), full(gain_b), full(w)],
        out_specs=rows(d),
        out_shape=jax.ShapeDtypeStruct((t, d), F32),
        compiler_params=_cparams(("parallel",)),
        name="out_proj",
    )(*x_parts, oa, ob, gain_a, gain_b, w)


def _rot_cols(w):
    half = w.shape[1] // 2
    return jnp.concatenate([-w[:, half:], w[:, :half]], axis=1)


def _layout_w_in(w_in):
    d = w_in.shape[0]
    off = np.cumsum([0, NSA_WIDTH] + [NSA_KV_WIDTH] * 6 + [NSA_HEADS * 3, MLA_Q_RANK, MLA_KV_RANK, MLA_ROPE_DIM])
    q, kc, vc, ks, vs, kw, vw, gt, cq, ckv, kr = [w_in[:, off[i]:off[i + 1]] for i in range(11)]
    zr = jnp.zeros((d, LANES - MLA_ROPE_DIM), w_in.dtype)
    zg = jnp.zeros((d, LANES - NSA_HEADS * 3), w_in.dtype)
    w_n = jnp.concatenate([kc, vc, ks, kw, cq, ckv, kr, zr, _rot_cols(kr), zr], axis=1).astype(BF16)
    w_t = jnp.concatenate([q, vs, vw, gt, zg], axis=1).astype(BF16).T
    return w_n, w_t


def _layout_mla_weights(w_uq, w_ukv):
    r = w_uq.shape[0]
    zr = jnp.zeros((r, LANES - MLA_ROPE_DIM), w_uq.dtype)
    main, rot, k_cols, v_cols = [], [], [], []
    for hd in range(MLA_HEADS):
        base = hd * MLA_QK_DIM
        rope = w_uq[:, base + MLA_NOPE_DIM:base + MLA_QK_DIM]
        main += [w_uq[:, base:base + MLA_NOPE_DIM], rope, zr]
        rot += [_rot_cols(rope), zr]
        kv = hd * (MLA_NOPE_DIM + MLA_V_DIM)
        k_cols.append(w_ukv[:, kv:kv + MLA_NOPE_DIM])
        v_cols.append(w_ukv[:, kv + MLA_NOPE_DIM:kv + MLA_NOPE_DIM + MLA_V_DIM])
    cat = lambda parts: jnp.concatenate(parts, axis=1).astype(BF16)
    return cat(main).T, cat(rot).T, cat(k_cols), cat(v_cols).T


def _rope_gains(gain):
    half = MLA_ROPE_DIM // 2
    zr = jnp.zeros((LANES - MLA_ROPE_DIM,), gain.dtype)
    g_rope = gain[MLA_NOPE_DIM:]
    return (gain[None, :MLA_NOPE_DIM], jnp.concatenate([g_rope, zr])[None, :],
            jnp.concatenate([g_rope[half:], g_rope[:half], zr])[None, :])


def _rope_tables(s):
    half = MLA_ROPE_DIM // 2
    inv = ROPE_THETA ** (-jnp.arange(half, dtype=F32) / half)
    ang = jnp.arange(s).astype(F32)[:, None] * inv[None, :]
    zr = jnp.zeros((s, LANES - MLA_ROPE_DIM), F32)
    cos, sin = jnp.cos(ang), jnp.sin(ang)
    return jnp.concatenate([cos, cos, zr], axis=1), jnp.concatenate([sin, sin, zr], axis=1)


def kernel(x, ffn1_norm, ffn1_w_gate, ffn1_w_up, ffn1_w_down, mix_norm, w_in, nsa_q_norm, nsa_k_norm, nsa_cmp_pos_k, nsa_cmp_w1_k, nsa_cmp_w2_k, nsa_cmp_pos_v, nsa_cmp_w1_v, nsa_cmp_w2_v, mla_q_a_norm, mla_w_uq, mla_kv_a_norm, mla_w_ukv, mla_q_norm, mla_k_norm, out_norm_nsa, out_norm_mla, w_out, ffn2_norm, ffn2_w_gate, ffn2_w_up, ffn2_w_down):
    b, s, d = x.shape
    depth = ffn1_norm.shape[0]
    tables = _rope_tables(s)
    xt = x.reshape(b * s, d)
    for l in range(depth):
        x_parts = _ffn(xt, ffn1_norm[l][None, :], ffn1_w_gate[l], ffn1_w_up[l], ffn1_w_down[l], in_place=False)

        w_n, w_t = _layout_w_in(w_in[l])
        q_t, kv_cmp, ks, vs_t, kw, vw_t, gate_t, cq, ckv, kr = _proj(
            x_parts, mix_norm[l][None, :], w_n, w_t, nsa_q_norm[l], nsa_k_norm[l],
            mla_q_a_norm[l][None, :], mla_kv_a_norm[l][None, :])

        kc, vc_t = _compress(
            kv_cmp, s,
            nsa_cmp_w1_k[l], nsa_cmp_w2_k[l], nsa_cmp_pos_k[l].reshape(1, -1),
            nsa_cmp_w1_v[l], nsa_cmp_w2_v[l].T, nsa_cmp_pos_v[l].reshape(1, -1),
            nsa_k_norm[l])

        def bs(a):
            return a.reshape(b, s, a.shape[-1])

        o_a = _nsa(q_t, kc, vc_t, bs(ks), vs_t, bs(kw), vw_t, gate_t)

        q_m, k_m, v_m = _mla_prep(
            bs(cq), bs(ckv), bs(kr), tables, _layout_mla_weights(mla_w_uq[l], mla_w_ukv[l]),
            _rope_gains(mla_q_norm[l]), _rope_gains(mla_k_norm[l]))
        o_b = _mla_attn(q_m, k_m, v_m)

        xt = _out_proj(x_parts, o_a.reshape(b * s, NSA_WIDTH), o_b.reshape(b * s, MLA_WIDTH),
                       out_norm_nsa[l][None, :], out_norm_mla[l][None, :], w_out[l].astype(BF16))

        xt = _ffn(xt, ffn2_norm[l][None, :], ffn2_w_gate[l], ffn2_w_up[l], ffn2_w_down[l], in_place=True)
    return xt.reshape(b, s, d)
```

```python
import functools

import numpy as np
import jax
import jax.numpy as jnp
from jax import lax
from jax.experimental import pallas as pl
from jax.experimental.pallas import tpu as pltpu

F32 = jnp.float32
BF16 = jnp.bfloat16

HEAD_DIM = 128
NSA_HEADS = 8
NSA_KV_HEADS = 2
NSA_GROUP = NSA_HEADS // NSA_KV_HEADS
CMP_BLOCK = 32
CMP_STRIDE = 16
CMP_HIDDEN = 256
SEL_BLOCK = 64
SEL_SHIFT = 6
SEL_TOP = 8
WINDOW = 512
MLA_HEADS = 8
MLA_Q_RANK = 384
MLA_KV_RANK = 256
MLA_NOPE_DIM = 128
MLA_ROPE_DIM = 64
MLA_V_DIM = 128
MLA_QK_DIM = MLA_NOPE_DIM + MLA_ROPE_DIM
ROPE_THETA = 10000.0
EPS = 1e-6
NEG = -1e30
FORCE_SCORE = 1e4
NSA_WIDTH = NSA_HEADS * HEAD_DIM
MLA_WIDTH = MLA_HEADS * MLA_V_DIM
NSA_KV_WIDTH = NSA_KV_HEADS * HEAD_DIM

LANES = 128
MLA_QK_PAD = 2 * LANES
MLA_HEAD_GROUP = 2
VMEM_LIMIT = 58 * 1024 * 1024
POS_SPLIT = 16
FEAT_ROWS = 16
NSA_Q_TILE = 256
MLA_Q_TILE = 256


def _cparams(sem):
    return pltpu.CompilerParams(dimension_semantics=sem, vmem_limit_bytes=VMEM_LIMIT)


def _rms(x, gain):
    return x * lax.rsqrt(jnp.mean(x * x, axis=-1, keepdims=True) + EPS) * gain


def _dot(a, b):
    return jnp.dot(a, b, preferred_element_type=F32)


def _dot_nt(a, b):
    return lax.dot_general(a, b, (((1,), (1,)), ((), ())), preferred_element_type=F32)


def _sigmoid(x):
    return 1.0 / (1.0 + jnp.exp(-x))


def _ffn_accumulate(x_ref, g_ref, weights, o_ref, h_ref):
    @pl.when(pl.program_id(1) == 0)
    def _():
        x = x_ref[...]
        h_ref[...] = _rms(x, g_ref[...]).astype(BF16)
        o_ref[...] = x

    wg, wu, wd = weights()
    h = h_ref[...]
    gate = _dot(h, wg)
    up = _dot(h, wu)
    act = gate * _sigmoid(gate) * up * 0.5
    o_ref[...] += _dot(act.astype(BF16), wd)


def _ffn_first_kernel(x_ref, g_ref, wg_ref, wu_ref, wd_ref, o_ref, wg_out, wu_out, wd_out, h_ref):
    def weights():
        wg, wu, wd = wg_ref[...].astype(BF16), wu_ref[...].astype(BF16), wd_ref[...].astype(BF16)
        wg_out[...], wu_out[...], wd_out[...] = wg, wu, wd
        return wg, wu, wd

    _ffn_accumulate(x_ref, g_ref, weights, o_ref, h_ref)


def _ffn_rest_kernel(x_ref, g_ref, wg_ref, wu_ref, wd_ref, o_ref, h_ref):
    _ffn_accumulate(x_ref, g_ref, lambda: (wg_ref[...], wu_ref[...], wd_ref[...]), o_ref, h_ref)


def _ffn(x, gain, wg, wu, wd, *, in_place, tm=1024, tf_first=256, tf=512):
    t, d = x.shape
    f = wg.shape[1]
    alias = {0: 0} if in_place else {}

    def w_specs(tfx):
        return [pl.BlockSpec((d, tfx), lambda i, j: (0, j)),
                pl.BlockSpec((d, tfx), lambda i, j: (0, j)),
                pl.BlockSpec((tfx, d), lambda i, j: (j, 0))]

    y, wg_b, wu_b, wd_b = pl.pallas_call(
        _ffn_first_kernel,
        grid=(1, f // tf_first),
        in_specs=[pl.BlockSpec((tm, d), lambda i, j: (0, 0), pipeline_mode=pl.Buffered(1)),
                  pl.BlockSpec((1, d), lambda i, j: (0, 0))] + w_specs(tf_first),
        out_specs=[pl.BlockSpec((tm, d), lambda i, j: (0, 0))] + w_specs(tf_first),
        out_shape=[jax.ShapeDtypeStruct((t if in_place else tm, d), F32), jax.ShapeDtypeStruct((d, f), BF16),
                   jax.ShapeDtypeStruct((d, f), BF16), jax.ShapeDtypeStruct((f, d), BF16)],
        scratch_shapes=[pltpu.VMEM((tm, d), BF16)],
        input_output_aliases=alias,
        compiler_params=_cparams(("arbitrary", "arbitrary")),
        name="ffn_first",
    )(x, gain, wg, wu, wd)

    out_row0 = 1 if in_place else 0
    rest = pl.pallas_call(
        _ffn_rest_kernel,
        grid=(t // tm - 1, f // tf),
        in_specs=[pl.BlockSpec((tm, d), lambda i, j: (i + 1, 0)),
                  pl.BlockSpec((1, d), lambda i, j: (0, 0))] + w_specs(tf),
        out_specs=pl.BlockSpec((tm, d), lambda i, j: (i + out_row0, 0)),
        out_shape=jax.ShapeDtypeStruct((t if in_place else t - tm, d), F32),
        scratch_shapes=[pltpu.VMEM((tm, d), BF16)],
        input_output_aliases=alias,
        compiler_params=_cparams(("parallel", "arbitrary")),
        name="ffn_rest",
    )(y if in_place else x, gain, wg_b, wu_b, wd_b)
    return rest if in_place else (y, rest)


_N_CMP = 0
_N_KS = _N_CMP + 2 * NSA_KV_WIDTH
_N_KW = _N_KS + NSA_KV_WIDTH
_N_CQ = _N_KW + NSA_KV_WIDTH
_N_CKV = _N_CQ + MLA_Q_RANK
_N_KR = _N_CKV + MLA_KV_RANK
_N_END = _N_KR + 2 * LANES
_T_Q = 0
_T_VS = _T_Q + NSA_WIDTH
_T_VW = _T_VS + NSA_KV_WIDTH
_T_GATE = _T_VW + NSA_KV_WIDTH
_T_END = _T_GATE + LANES


def _two_part_specs(parts, tm):
    first, rest = parts
    nb_first = first.shape[0] // tm
    d = first.shape[1]
    return nb_first, [pl.BlockSpec((tm, d), lambda i: (jnp.minimum(i, nb_first - 1), 0)),
                      pl.BlockSpec((tm, d), lambda i: (jnp.maximum(i - nb_first, 0), 0))]


def _two_part_rows(first_ref, rest_ref, nb_first):
    return jnp.where(pl.program_id(0) < nb_first, first_ref[...], rest_ref[...])


def _mla_project(cq, ckv, k_rope, k_rot, tables, weights, gains, q_t_ref, k_ref, v_t_ref):
    cos, sin, cos_t, sin_t = (r[...] for r in tables)
    wq_t_ref, wqr_t_ref, wk_ref, wv_t_ref = weights
    qgn, qgr, qgt, kgn, kgr, kgt = (r[...] for r in gains)
    tq = q_t_ref.shape[-1]
    k_rope_ssq = jnp.sum(k_rope * k_rope, axis=-1, keepdims=True)
    k_rope_emb = k_rope * kgr * cos + k_rot * kgt * sin
    q_cos = qgr * cos_t
    q_sin = qgt * sin_t
    qscale = MLA_QK_DIM ** -0.5
    hg = MLA_HEAD_GROUP
    for hd in range(MLA_HEADS):
        if hd % hg == 0:
            qx_all = _dot_nt(wq_t_ref[hd * MLA_QK_PAD:(hd + hg) * MLA_QK_PAD, :], cq)
            q_rot_all = _dot_nt(wqr_t_ref[hd * LANES:(hd + hg) * LANES, :], cq)
            k_nope_all = _dot(ckv, wk_ref[:, hd * LANES:(hd + hg) * LANES])
            v_all = _dot_nt(wv_t_ref[hd * LANES:(hd + hg) * LANES, :], ckv)
        lo = hd % hg
        q_nope = qx_all[lo * MLA_QK_PAD:lo * MLA_QK_PAD + LANES]
        q_rope = qx_all[lo * MLA_QK_PAD + LANES:(lo + 1) * MLA_QK_PAD]
        q_rot = q_rot_all[lo * LANES:(lo + 1) * LANES]
        ssq = jnp.sum(q_nope * q_nope, axis=0, keepdims=True) + jnp.sum(q_rope * q_rope, axis=0, keepdims=True)
        inv = lax.rsqrt(ssq / MLA_QK_DIM + EPS) * qscale
        qn_nope = (q_nope * inv * qgn).astype(BF16)
        qn_rope = ((q_rope * q_cos + q_rot * q_sin) * inv).astype(BF16)
        for c in range(q_t_ref.shape[1]):
            q_t_ref[0, c, hd, 0:LANES, :] = qn_nope[:, c * tq:(c + 1) * tq]
            q_t_ref[0, c, hd, LANES:2 * LANES, :] = qn_rope[:, c * tq:(c + 1) * tq]

        k_nope = k_nope_all[:, lo * LANES:(lo + 1) * LANES]
        ssq = jnp.sum(k_nope * k_nope, axis=-1, keepdims=True) + k_rope_ssq
        inv = lax.rsqrt(ssq / MLA_QK_DIM + EPS)
        k_ref[0, hd, :, 0:LANES] = (k_nope * inv * kgn).astype(BF16)
        k_ref[0, hd, :, LANES:2 * LANES] = (k_rope_emb * inv).astype(BF16)
        v_t_ref[0, hd] = v_all[lo * LANES:(lo + 1) * LANES].astype(BF16)


def _proj_kernel(xa_ref, xb_ref, g_ref, wn_ref, wt_ref, qg_ref, kg_ref, qag_ref, kvag_ref, *rest, nb_first):
    mla_tables, mla_weights, mla_gains = rest[0:4], rest[4:8], rest[8:14]
    q_t_ref, cmp_ref, ks_ref, vs_t_ref, kw_ref, vw_t_ref, gate_t_ref, qm_t_ref, km_ref, vm_t_ref = rest[14:]
    tq = q_t_ref.shape[2]
    h = _rms(_two_part_rows(xa_ref, xb_ref, nb_first), g_ref[...]).astype(BF16)
    n_cuts = (_N_CMP, _N_CQ, _N_END)
    t_cuts = (_T_Q, _T_VS, _T_END)
    p_n = [_dot(h, wn_ref[:, a:b]) for a, b in zip(n_cuts[:-1], n_cuts[1:])]
    p_t = [_dot_nt(wt_ref[a:b, :], h) for a, b in zip(t_cuts[:-1], t_cuts[1:])]

    def proj(lo, hi):
        c = max(i for i, a in enumerate(n_cuts[:-1]) if a <= lo)
        return p_n[c][:, lo - n_cuts[c]:hi - n_cuts[c]]

    def proj_t(lo, hi):
        c = max(i for i, a in enumerate(t_cuts[:-1]) if a <= lo)
        return p_t[c][lo - t_cuts[c]:hi - t_cuts[c], :]

    qscale = HEAD_DIM ** -0.5
    for hd in range(NSA_HEADS):
        q = proj_t(_T_Q + hd * HEAD_DIM, _T_Q + (hd + 1) * HEAD_DIM)
        inv = lax.rsqrt(jnp.mean(q * q, axis=0, keepdims=True) + EPS) * qscale
        qn = (q * inv * qg_ref[...]).astype(BF16)
        for c in range(q_t_ref.shape[0]):
            q_t_ref[c, hd * HEAD_DIM:(hd + 1) * HEAD_DIM, :] = qn[:, c * tq:(c + 1) * tq]
    for c in range(cmp_ref.shape[0]):
        cmp_ref[c] = proj(_N_CMP + c * HEAD_DIM, _N_CMP + (c + 1) * HEAD_DIM)
    for g in range(NSA_KV_HEADS):
        sl = slice(g * HEAD_DIM, (g + 1) * HEAD_DIM)
        ks_ref[:, sl] = _rms(proj(_N_KS + g * HEAD_DIM, _N_KS + (g + 1) * HEAD_DIM), kg_ref[1:2, :]).astype(BF16)
        kw_ref[:, sl] = _rms(proj(_N_KW + g * HEAD_DIM, _N_KW + (g + 1) * HEAD_DIM), kg_ref[2:3, :]).astype(BF16)
    vs_t_ref[...] = proj_t(_T_VS, _T_VW).astype(BF16)
    vw_t_ref[...] = proj_t(_T_VW, _T_GATE).astype(BF16)
    gates = _sigmoid(proj_t(_T_GATE, _T_END))
    for c in range(gate_t_ref.shape[0]):
        gate_t_ref[c] = gates[:, c * tq:(c + 1) * tq]
    cq = _rms(proj(_N_CQ, _N_CKV), qag_ref[...]).astype(BF16)
    ckv = _rms(proj(_N_CKV, _N_KR), kvag_ref[...]).astype(BF16)
    _mla_project(cq, ckv, proj(_N_KR, _N_KR + LANES), proj(_N_KR + LANES, _N_END),
                 mla_tables, mla_weights, mla_gains, qm_t_ref, km_ref, vm_t_ref)


def _proj(x_parts, seq, gain, w_n, w_t, q_gain, k_gains, qa_gain, kva_gain, mla_tables, mla_weights, mla_q_gains,
          mla_k_gains, *, tm=512, tq=NSA_Q_TILE, tq_mla=MLA_Q_TILE):
    t = x_parts[0].shape[0] + x_parts[1].shape[0]
    nb_first, x_specs = _two_part_specs(x_parts, tm)
    q_gain = jnp.broadcast_to(q_gain.reshape(HEAD_DIM, 1), (HEAD_DIM, tm))
    mla_q_gains = tuple(jnp.broadcast_to(g.reshape(LANES, 1), (LANES, tm)) for g in mla_q_gains)
    cos, sin = mla_tables
    nj = seq // tm
    pos_rows = pl.BlockSpec((tm, LANES), lambda i: (i % nj, 0))
    pos_cols = pl.BlockSpec((LANES, tm), lambda i: (0, i % nj))

    def full(a):
        return pl.BlockSpec(a.shape, lambda i: (0,) * a.ndim)

    def rows(width):
        return (t, width), pl.BlockSpec((tm, width), lambda i: (i, 0))

    def cols(height):
        return (height, t), pl.BlockSpec((height, tm), lambda i: (0, i))

    def q_tiles(height):
        return (t // tq, height, tq), pl.BlockSpec((tm // tq, height, tq), lambda i: (i, 0, 0))

    n_cmp_cols = 2 * NSA_KV_WIDTH // HEAD_DIM
    cmp_out = (n_cmp_cols, t, HEAD_DIM), pl.BlockSpec((n_cmp_cols, tm, HEAD_DIM), lambda i: (0, i, 0))
    b = t // seq
    mla_q = ((b, seq // tq_mla, MLA_HEADS, MLA_QK_PAD, tq_mla),
             pl.BlockSpec((1, tm // tq_mla, MLA_HEADS, MLA_QK_PAD, tq_mla), lambda i: (i // nj, i % nj, 0, 0, 0)))
    mla_k = ((b, MLA_HEADS, seq, MLA_QK_PAD),
             pl.BlockSpec((1, MLA_HEADS, tm, MLA_QK_PAD), lambda i: (i // nj, 0, i % nj, 0)))
    mla_v = ((b, MLA_HEADS, MLA_V_DIM, seq),
             pl.BlockSpec((1, MLA_HEADS, MLA_V_DIM, tm), lambda i: (i // nj, 0, 0, i % nj)))
    outs = [(q_tiles(NSA_WIDTH), BF16), (cmp_out, F32), (rows(NSA_KV_WIDTH), BF16),
            (cols(NSA_KV_WIDTH), BF16), (rows(NSA_KV_WIDTH), BF16), (cols(NSA_KV_WIDTH), BF16),
            (q_tiles(LANES), F32), (mla_q, BF16), (mla_k, BF16), (mla_v, BF16)]
    mla_in = (cos, sin, cos.T, sin.T) + tuple(mla_weights) + mla_q_gains + tuple(mla_k_gains)
    mla_specs = [pos_rows, pos_rows, pos_cols, pos_cols] + [full(a) for a in mla_in[4:]]
    return pl.pallas_call(
        functools.partial(_proj_kernel, nb_first=nb_first),
        grid=(t // tm,),
        in_specs=x_specs + [full(gain), full(w_n), full(w_t), full(q_gain), full(k_gains), full(qa_gain),
                            full(kva_gain)] + mla_specs,
        out_specs=[spec for (_, spec), _ in outs],
        out_shape=[jax.ShapeDtypeStruct(shape, dt) for (shape, _), dt in outs],
        compiler_params=_cparams(("parallel",)),
        name="proj",
    )(*x_parts, gain, w_n, w_t, q_gain, k_gains, qa_gain, kva_gain, *mla_in)


def _gelu_tanh(x):
    return 0.5 * x * (1.0 + jnp.tanh(np.sqrt(2.0 / np.pi) * (x + 0.044715 * (x * x * x))))


def _compress_kernel(x_ref, w1k_ref, w2k_ref, pk_ref, w1v_ref, w2v_ref, pv_ref, kg_ref, kc_ref, vc_t_ref):
    half = CMP_STRIDE * HEAD_DIM
    nrow = x_ref.shape[1] // CMP_STRIDE
    for which, (w1_ref, w2_ref, p_ref) in enumerate(((w1k_ref, w2k_ref, pk_ref), (w1v_ref, w2v_ref, pv_ref))):
        w1 = w1_ref[...].astype(BF16)
        w2 = w2_ref[...].astype(BF16)
        pos_bias = _dot(jnp.broadcast_to(p_ref[...], (8, 2 * half)).astype(BF16), w1)[0:1, :]
        for g in range(NSA_KV_HEADS):
            col = which * NSA_KV_HEADS + g
            xg = jnp.concatenate(
                [x_ref[col, pl.ds(l, nrow, stride=CMP_STRIDE), :].astype(BF16) for l in range(CMP_STRIDE)], axis=1)
            first = _dot(xg, w1[0:half, :])
            second = _dot(xg, w1[half:2 * half, :])
            pre = first + pltpu.roll(second, nrow - 1, 0) + pos_bias
            hidden = _gelu_tanh(pre).astype(BF16)
            if which == 0:
                kc_ref[0, g] = _rms(_dot(hidden, w2), kg_ref[0:1, :]).astype(BF16)
            else:
                vc_t_ref[0, g] = _dot_nt(w2, hidden).astype(BF16)


def _compress(xc, seq, w1k, w2k, pk, w1v, w2v, pv, k_gains):
    ncol, t, width = xc.shape
    b, nrow = t // seq, seq // CMP_STRIDE

    def full(a):
        return pl.BlockSpec(a.shape, lambda i: (0,) * a.ndim)

    out_sds = jax.ShapeDtypeStruct((b, NSA_KV_HEADS, nrow, HEAD_DIM), BF16)
    out_spec = pl.BlockSpec((1, NSA_KV_HEADS, nrow, HEAD_DIM), lambda i: (i, 0, 0, 0))
    return pl.pallas_call(
        _compress_kernel,
        grid=(b,),
        in_specs=[pl.BlockSpec((ncol, seq, width), lambda i: (0, i, 0)),
                  full(w1k), full(w2k), full(pk), full(w1v), full(w2v), full(pv), full(k_gains)],
        out_specs=[out_spec, out_spec],
        out_shape=[out_sds, out_sds],
        compiler_params=_cparams(("parallel",)),
        name="compress",
    )(xc, w1k, w2k, pk, w1v, w2v, pv, k_gains)


def _softmax_cols(s):
    m = jnp.max(s, axis=0, keepdims=True)
    p = jnp.exp(s - m)
    return m, jnp.sum(p, axis=0, keepdims=True), p


def _nsa_kernel(q_t_ref, kc_ref, vc_t_ref, ks_ref, vs_t_ref, kw_ref, vw_t_ref, gate_t_ref, kfeat_ref, cfeat_ref,
                slope_ref, o_ref, *, tq, ck):
    qi = pl.program_id(1)
    q0 = qi * tq
    ng = NSA_KV_HEADS
    wide = NSA_GROUP * tq
    n_sel = ks_ref.shape[1] // SEL_BLOCK
    ncmp = kc_ref.shape[2]
    n_win = WINDOW + tq

    def tile4(a):
        return jnp.concatenate([a] * NSA_GROUP, axis=1)

    def masked(s, mask, fill=NEG):
        return jnp.concatenate([jnp.where(mask, s[:, r * tq:(r + 1) * tq], fill) for r in range(NSA_GROUP)], axis=1)

    def q_aug(g, extra):
        q = jnp.concatenate([q_t_ref[0, (g * NSA_GROUP + r) * HEAD_DIM:(g * NSA_GROUP + r + 1) * HEAD_DIM, :]
                             for r in range(NSA_GROUP)], axis=1)
        rows = [q, slope_ref[g]]
        used = slope_ref.shape[1]
        if extra is not None:
            rows.append(extra)
            used += extra.shape[0]
        rows.append(jnp.zeros((HEAD_DIM - used, wide), BF16))
        return jnp.concatenate(rows, axis=0)

    def gsl(g):
        return slice(g * HEAD_DIM, (g + 1) * HEAD_DIM)

    lane_q = q0 + lax.broadcasted_iota(jnp.int32, (1, tq), 1)

    qa_plain = [q_aug(g, None) for g in range(ng)]
    w0 = pl.multiple_of(jnp.maximum(q0 - WINDOW, 0), tq)
    s_cmp = [_dot(jnp.concatenate([kc_ref[0, g], cfeat_ref[...]], axis=1), qa_plain[g]) for g in range(ng)]
    s_win = [_dot(jnp.concatenate([kw_ref[0, pl.ds(w0, n_win), gsl(g)], kfeat_ref[pl.ds(w0, n_win), :]], axis=1),
                  qa_plain[g]) for g in range(ng)]

    n_row = lax.broadcasted_iota(jnp.int32, (ncmp, tq), 0)
    mask_c = n_row * CMP_STRIDE + (CMP_BLOCK - 1) <= lane_q
    s_row = lax.broadcasted_iota(jnp.int32, (n_sel, ncmp), 0)
    n_col = lax.broadcasted_iota(jnp.int32, (n_sel, ncmp), 1)
    overlap_t = jnp.where((n_col * CMP_STRIDE < (s_row + 1) * SEL_BLOCK)
                          & (n_col * CMP_STRIDE + (CMP_BLOCK - 1) >= s_row * SEL_BLOCK), 1.0, 0.0).astype(BF16)
    p_cmp, imp = [], []
    for g in range(ng):
        s = masked(s_cmp[g], mask_c)
        m = jnp.max(s, axis=0, keepdims=True)
        p = masked(jnp.exp(s - m), mask_c, 0.0)
        l = jnp.sum(p, axis=0, keepdims=True)
        p = p / jnp.where(l > 0.0, l, 1.0)
        p_cmp.append(p.astype(BF16))
        p_sum = p[:, 0:tq]
        for r in range(1, NSA_GROUP):
            p_sum = p_sum + p[:, r * tq:(r + 1) * tq]
        p_hi = p_sum.astype(BF16)
        p_lo = (p_sum - p_hi.astype(F32)).astype(BF16)
        imp.append(_dot(overlap_t, p_hi) + _dot(overlap_t, p_lo))
    o_cmp = [_dot(vc_t_ref[0, g], p_cmp[g]) for g in range(ng)]

    blk = lax.broadcasted_iota(jnp.int32, (n_sel, tq), 0)
    blk_f = blk.astype(F32)
    forced = (blk == 0) | (blk == jnp.right_shift(lane_q, SEL_SHIFT))
    future = blk * SEL_BLOCK > lane_q
    penalty = []
    for g in range(ng):
        v = jnp.where(future, NEG, jnp.where(forced, FORCE_SCORE, imp[g]))
        unselected = jnp.full((n_sel, tq), NEG, F32)
        for _ in range(min(SEL_TOP, n_sel)):
            mx = jnp.max(v, axis=0, keepdims=True)
            first = jnp.min(jnp.where(v == mx, blk_f, float(n_sel)), axis=0, keepdims=True)
            pick = blk_f == first
            unselected = jnp.where(pick, 0.0, unselected)
            v = jnp.where(pick, -jnp.inf, v)
        penalty.append(tile4(unselected.astype(BF16)))

    rel = lax.broadcasted_iota(jnp.int32, (n_win, tq), 0) - lax.broadcasted_iota(jnp.int32, (n_win, tq), 1)
    off = q0 - w0
    mask_w = (rel <= off) & (rel > off - WINDOW)
    o_win = []
    p_win = []
    for g in range(ng):
        _, l, p = _softmax_cols(masked(s_win[g], mask_w))
        p_win.append((l, p.astype(BF16)))
    for g in range(ng):
        l, p = p_win[g]
        o_win.append(_dot(vw_t_ref[gsl(g), pl.ds(w0, n_win)], p) / l)

    partial = []
    for g in range(ng):
        for r in range(NSA_GROUP):
            hd = g * NSA_GROUP + r
            cs = slice(r * tq, (r + 1) * tq)
            partial.append(gate_t_ref[0, 3 * hd:3 * hd + 1, :] * o_cmp[g][:, cs]
                           + gate_t_ref[0, 3 * hd + 2:3 * hd + 3, :] * o_win[g][:, cs])

    qa_sel = [q_aug(g, penalty[g]) for g in range(ng)]
    rel_d = lax.broadcasted_iota(jnp.int32, (ck, tq), 0) - lax.broadcasted_iota(jnp.int32, (ck, tq), 1)

    def select_and_store(n_before):
        nk = n_before + ck
        mask_d = rel_d <= q0 - n_before
        ss = [_dot(jnp.concatenate([ks_ref[0, 0:nk, gsl(g)], kfeat_ref[0:nk, :]], axis=1), qa_sel[g])
              for g in range(ng)]
        for g in range(ng):
            s_diag = masked(ss[g][n_before:nk], mask_d)
            m = jnp.max(s_diag, axis=0, keepdims=True)
            if n_before:
                m = jnp.maximum(m, jnp.max(ss[g][0:n_before], axis=0, keepdims=True))
            p_diag = jnp.exp(s_diag - m)
            l = jnp.sum(p_diag, axis=0, keepdims=True)
            p = p_diag.astype(BF16)
            if n_before:
                p_before = jnp.exp(ss[g][0:n_before] - m)
                l = l + jnp.sum(p_before, axis=0, keepdims=True)
                p = jnp.concatenate([p_before.astype(BF16), p], axis=0)
            o_sel = _dot(vs_t_ref[gsl(g), 0:nk], p) / l
            for r in range(NSA_GROUP):
                hd = g * NSA_GROUP + r
                mixed = partial[hd] + gate_t_ref[0, 3 * hd + 1:3 * hd + 2, :] * o_sel[:, r * tq:(r + 1) * tq]
                o_ref[0, :, hd * HEAD_DIM:(hd + 1) * HEAD_DIM] = mixed.T

    per = ck // tq
    for v in range(ks_ref.shape[1] // ck):
        pl.when(qi // per == v)(functools.partial(select_and_store, v * ck))


def _nsa(q_t, kc, vc_t, ks, vs_t, kw, vw_t, gate_t, *, tq=NSA_Q_TILE, ck=256):
    b, s, _ = ks.shape
    nq = s // tq
    pos = np.arange(s)
    kfeat = np.zeros((s, LANES), np.float32)
    kfeat[:, 0] = POS_SPLIT * (pos // POS_SPLIT)
    kfeat[:, 1] = pos % POS_SPLIT
    kfeat[pos, FEAT_ROWS + pos // SEL_BLOCK] = 1.0
    ncmp = kc.shape[2]
    cfeat = np.zeros((ncmp, LANES), np.float32)
    cfeat[:, 0] = CMP_STRIDE * np.arange(ncmp)
    cfeat[:, 1] = 0.5 * (CMP_BLOCK - 1)
    slope = np.zeros((NSA_KV_HEADS, FEAT_ROWS, NSA_GROUP * tq), np.float32)
    for hd in range(NSA_HEADS):
        g, r = divmod(hd, NSA_GROUP)
        slope[g, 0:2, r * tq:(r + 1) * tq] = 2.0 ** (-8.0 * (hd + 1) / NSA_HEADS)
    assert FEAT_ROWS + s // SEL_BLOCK <= LANES and s // SEL_BLOCK % FEAT_ROWS == 0 and s % ck == 0 and ck % tq == 0

    def full(a):
        return pl.BlockSpec(a.shape, lambda i, j: (0,) * a.ndim)

    def per_b(a):
        return pl.BlockSpec((1,) + a.shape[1:], lambda i, j: (i,) + (0,) * (a.ndim - 1))

    def per_b_cols(a):
        return pl.BlockSpec((a.shape[0], s), lambda i, j: (0, i))

    def q_cols(a):
        assert a.shape[0] == b * nq and a.shape[2] == tq
        return pl.BlockSpec((1, a.shape[1], tq), lambda i, j: (i * nq + j, 0, 0))

    consts = [jnp.asarray(kfeat, BF16), jnp.asarray(cfeat, BF16), jnp.asarray(slope, BF16)]
    return pl.pallas_call(
        functools.partial(_nsa_kernel, tq=tq, ck=ck),
        grid=(b, nq),
        in_specs=[q_cols(q_t), per_b(kc), per_b(vc_t), per_b(ks), per_b_cols(vs_t), per_b(kw), per_b_cols(vw_t),
                  q_cols(gate_t)] + [full(c) for c in consts],
        out_specs=pl.BlockSpec((1, tq, NSA_WIDTH), lambda i, j: (i, j, 0)),
        out_shape=jax.ShapeDtypeStruct((b, s, NSA_WIDTH), F32),
        compiler_params=_cparams(("parallel", "arbitrary")),
        name="nsa",
    )(q_t, kc, vc_t, ks, vs_t, kw, vw_t, gate_t, *consts)


def _mla_attn_kernel(q_t_ref, k_ref, v_t_ref, o_ref, *, tq, hb, n_q):
    qi = pl.program_id(2)
    d_v = v_t_ref.shape[2]
    tri = lax.broadcasted_iota(jnp.int32, (tq, tq), 0) <= lax.broadcasted_iota(jnp.int32, (tq, tq), 1)

    def attend(n_before):
        nk = n_before + tq
        ss = [_dot(k_ref[0, h, 0:nk, :], q_t_ref[0, 0, h]) for h in range(hb)]
        for h in range(hb):
            s_diag = jnp.where(tri, ss[h][n_before:nk], NEG)
            m = jnp.max(s_diag, axis=0, keepdims=True)
            if n_before:
                m = jnp.maximum(m, jnp.max(ss[h][0:n_before], axis=0, keepdims=True))
            p_diag = jnp.exp(s_diag - m)
            l = jnp.sum(p_diag, axis=0, keepdims=True)
            p = p_diag.astype(BF16)
            if n_before:
                p_before = jnp.exp(ss[h][0:n_before] - m)
                l = l + jnp.sum(p_before, axis=0, keepdims=True)
                p = jnp.concatenate([p_before.astype(BF16), p], axis=0)
            o_ref[0, :, h * d_v:(h + 1) * d_v] = (_dot(v_t_ref[0, h, :, 0:nk], p) / l).T

    for v in range(n_q):
        pl.when(qi == v)(functools.partial(attend, v * tq))


def _mla_attn(q_t, k, v_t, *, tq=MLA_Q_TILE, hb=MLA_HEADS):
    b, h, s, dq = k.shape
    dv = v_t.shape[2]
    assert q_t.shape == (b, s // tq, h, dq, tq)
    return pl.pallas_call(
        functools.partial(_mla_attn_kernel, tq=tq, hb=hb, n_q=s // tq),
        grid=(b, h // hb, s // tq),
        in_specs=[pl.BlockSpec((1, 1, hb, dq, tq), lambda i, j, l: (i, l, j, 0, 0)),
                  pl.BlockSpec((1, hb, s, dq), lambda i, j, l: (i, j, 0, 0)),
                  pl.BlockSpec((1, hb, dv, s), lambda i, j, l: (i, j, 0, 0))],
        out_specs=pl.BlockSpec((1, tq, hb * dv), lambda i, j, l: (i, l, j)),
        out_shape=jax.ShapeDtypeStruct((b, s, h * dv), F32),
        compiler_params=_cparams(("parallel", "parallel", "arbitrary")),
        name="mla_attn",
    )(q_t, k, v_t)


def _out_kernel(xa_ref, xb_ref, oa_ref, ob_ref, ga_ref, gb_ref, w_ref, o_ref, *, nb_first):
    oa = _rms(oa_ref[...], ga_ref[...]).astype(BF16)
    ob = _rms(ob_ref[...], gb_ref[...]).astype(BF16)
    o_ref[...] = (_two_part_rows(xa_ref, xb_ref, nb_first)
                  + _dot(jnp.concatenate([oa, ob], axis=1), w_ref[...]))


def _out_proj(x_parts, oa, ob, gain_a, gain_b, w, *, tm=512):
    t, d = oa.shape[0], x_parts[0].shape[1]
    nb_first, x_specs = _two_part_specs(x_parts, tm)

    def rows(width):
        return pl.BlockSpec((tm, width), lambda i: (i, 0))

    def full(a):
        return pl.BlockSpec(a.shape, lambda i: (0,) * a.ndim)

    return pl.pallas_call(
        functools.partial(_out_kernel, nb_first=nb_first),
        grid=(t // tm,),
        in_specs=x_specs + [rows(oa.shape[1]), rows(ob.shape[1]), full(gain_a), full(gain_b), full(w)],
        out_specs=rows(d),
        out_shape=jax.ShapeDtypeStruct((t, d), F32),
        compiler_params=_cparams(("parallel",)),
        name="out_proj",
    )(*x_parts, oa, ob, gain_a, gain_b, w)


def _rot_cols(w):
    half = w.shape[1] // 2
    return jnp.concatenate([-w[:, half:], w[:, :half]], axis=1)


def _layout_w_in(w_in):
    d = w_in.shape[0]
    off = np.cumsum([0, NSA_WIDTH] + [NSA_KV_WIDTH] * 6 + [NSA_HEADS * 3, MLA_Q_RANK, MLA_KV_RANK, MLA_ROPE_DIM])
    q, kc, vc, ks, vs, kw, vw, gt, cq, ckv, kr = [w_in[:, off[i]:off[i + 1]] for i in range(11)]
    zr = jnp.zeros((d, LANES - MLA_ROPE_DIM), w_in.dtype)
    zg = jnp.zeros((d, LANES - NSA_HEADS * 3), w_in.dtype)
    w_n = jnp.concatenate([kc, vc, ks, kw, cq, ckv, kr, zr, _rot_cols(kr), zr], axis=1).astype(BF16)
    w_t = jnp.concatenate([q, vs, vw, gt, zg], axis=1).astype(BF16).T
    return w_n, w_t


def _layout_mla_weights(w_uq, w_ukv):
    r = w_uq.shape[0]
    zr = jnp.zeros((r, LANES - MLA_ROPE_DIM), w_uq.dtype)
    main, rot, k_cols, v_cols = [], [], [], []
    for hd in range(MLA_HEADS):
        base = hd * MLA_QK_DIM
        rope = w_uq[:, base + MLA_NOPE_DIM:base + MLA_QK_DIM]
        main += [w_uq[:, base:base + MLA_NOPE_DIM], rope, zr]
        rot += [_rot_cols(rope), zr]
        kv = hd * (MLA_NOPE_DIM + MLA_V_DIM)
        k_cols.append(w_ukv[:, kv:kv + MLA_NOPE_DIM])
        v_cols.append(w_ukv[:, kv + MLA_NOPE_DIM:kv + MLA_NOPE_DIM + MLA_V_DIM])
    cat = lambda parts: jnp.concatenate(parts, axis=1).astype(BF16)
    return cat(main).T, cat(rot).T, cat(k_cols), cat(v_cols).T


def _rope_gains(gain):
    half = MLA_ROPE_DIM // 2
    zr = jnp.zeros((LANES - MLA_ROPE_DIM,), gain.dtype)
    g_rope = gain[MLA_NOPE_DIM:]
    return (gain[None, :MLA_NOPE_DIM], jnp.concatenate([g_rope, zr])[None, :],
            jnp.concatenate([g_rope[half:], g_rope[:half], zr])[None, :])


def _rope_tables(s):
    half = MLA_ROPE_DIM // 2
    inv = ROPE_THETA ** (-jnp.arange(half, dtype=F32) / half)
    ang = jnp.arange(s).astype(F32)[:, None] * inv[None, :]
    zr = jnp.zeros((s, LANES - MLA_ROPE_DIM), F32)
    cos, sin = jnp.cos(ang), jnp.sin(ang)
    return jnp.concatenate([cos, cos, zr], axis=1), jnp.concatenate([sin, sin, zr], axis=1)


def kernel(x, ffn1_norm, ffn1_w_gate, ffn1_w_up, ffn1_w_down, mix_norm, w_in, nsa_q_norm, nsa_k_norm, nsa_cmp_pos_k, nsa_cmp_w1_k, nsa_cmp_w2_k, nsa_cmp_pos_v, nsa_cmp_w1_v, nsa_cmp_w2_v, mla_q_a_norm, mla_w_uq, mla_kv_a_norm, mla_w_ukv, mla_q_norm, mla_k_norm, out_norm_nsa, out_norm_mla, w_out, ffn2_norm, ffn2_w_gate, ffn2_w_up, ffn2_w_down):
    b, s, d = x.shape
    depth = ffn1_norm.shape[0]
    tables = _rope_tables(s)
    xt = x.reshape(b * s, d)
    for l in range(depth):
        x_parts = _ffn(xt, ffn1_norm[l][None, :], ffn1_w_gate[l], ffn1_w_up[l], ffn1_w_down[l], in_place=False)

        w_n, w_t = _layout_w_in(w_in[l])
        q_t, kv_cmp, ks, vs_t, kw, vw_t, gate_t, q_m, k_m, v_m = _proj(
            x_parts, s, mix_norm[l][None, :], w_n, w_t, nsa_q_norm[l], nsa_k_norm[l],
            mla_q_a_norm[l][None, :], mla_kv_a_norm[l][None, :],
            tables, _layout_mla_weights(mla_w_uq[l], mla_w_ukv[l]),
            _rope_gains(mla_q_norm[l]), _rope_gains(mla_k_norm[l]))

        kc, vc_t = _compress(
            kv_cmp, s,
            nsa_cmp_w1_k[l], nsa_cmp_w2_k[l], nsa_cmp_pos_k[l].reshape(1, -1),
            nsa_cmp_w1_v[l], nsa_cmp_w2_v[l].T, nsa_cmp_pos_v[l].reshape(1, -1),
            nsa_k_norm[l])

        def bs(a):
            return a.reshape(b, s, a.shape[-1])

        o_a = _nsa(q_t, kc, vc_t, bs(ks), vs_t, bs(kw), vw_t, gate_t)

        o_b = _mla_attn(q_m, k_m, v_m)

        xt = _out_proj(x_parts, o_a.reshape(b * s, NSA_WIDTH), o_b.reshape(b * s, MLA_WIDTH),
                       out_norm_nsa[l][None, :], out_norm_mla[l][None, :], w_out[l].astype(BF16))

        xt = _ffn(xt, ffn2_norm[l][None, :], ffn2_w_gate[l], ffn2_w_up[l], ffn2_w_down[l], in_place=True)
    return xt.reshape(b, s, d)
```

```python
import functools

import numpy as np
import jax
import jax.numpy as jnp
from jax import lax
from jax.experimental import pallas as pl
from jax.experimental.pallas import tpu as pltpu

F32 = jnp.float32
BF16 = jnp.bfloat16

HEAD_DIM = 128
NSA_HEADS = 8
NSA_KV_HEADS = 2
NSA_GROUP = NSA_HEADS // NSA_KV_HEADS
CMP_BLOCK = 32
CMP_STRIDE = 16
CMP_HIDDEN = 256
SEL_BLOCK = 64
SEL_SHIFT = 6
SEL_TOP = 8
WINDOW = 512
MLA_HEADS = 8
MLA_Q_RANK = 384
MLA_KV_RANK = 256
MLA_NOPE_DIM = 128
MLA_ROPE_DIM = 64
MLA_V_DIM = 128
MLA_QK_DIM = MLA_NOPE_DIM + MLA_ROPE_DIM
ROPE_THETA = 10000.0
EPS = 1e-6
NEG = -1e30
FORCE_SCORE = 1e4
NSA_WIDTH = NSA_HEADS * HEAD_DIM
MLA_WIDTH = MLA_HEADS * MLA_V_DIM
NSA_KV_WIDTH = NSA_KV_HEADS * HEAD_DIM

LANES = 128
MLA_QK_PAD = 2 * LANES
MLA_HEAD_GROUP = 2
VMEM_LIMIT = 58 * 1024 * 1024
POS_SPLIT = 16
FEAT_ROWS = 16
NSA_Q_TILE = 256
MLA_Q_TILE = 256


def _cparams(sem):
    return pltpu.CompilerParams(dimension_semantics=sem, vmem_limit_bytes=VMEM_LIMIT)


def _rms(x, gain):
    return x * lax.rsqrt(jnp.mean(x * x, axis=-1, keepdims=True) + EPS) * gain


def _dot(a, b):
    return jnp.dot(a, b, preferred_element_type=F32)


def _dot_nt(a, b):
    return lax.dot_general(a, b, (((1,), (1,)), ((), ())), preferred_element_type=F32)


def _sigmoid(x):
    return 1.0 / (1.0 + jnp.exp(-x))


def _ffn_accumulate(x_ref, g_ref, weights, o_ref, h_ref):
    @pl.when(pl.program_id(1) == 0)
    def _():
        x = x_ref[...]
        h_ref[...] = _rms(x, g_ref[...]).astype(BF16)
        o_ref[...] = x

    wg, wu, wd = weights()
    h = h_ref[...]
    gate = _dot(h, wg)
    up = _dot(h, wu)
    act = gate * _sigmoid(gate) * up * 0.5
    o_ref[...] += _dot(act.astype(BF16), wd)


def _ffn_first_kernel(x_ref, g_ref, wg_ref, wu_ref, wd_ref, o_ref, wg_out, wu_out, wd_out, h_ref):
    def weights():
        wg, wu, wd = wg_ref[...].astype(BF16), wu_ref[...].astype(BF16), wd_ref[...].astype(BF16)
        wg_out[...], wu_out[...], wd_out[...] = wg, wu, wd
        return wg, wu, wd

    _ffn_accumulate(x_ref, g_ref, weights, o_ref, h_ref)


def _ffn_rest_kernel(x_ref, g_ref, wg_ref, wu_ref, wd_ref, o_ref, h_ref):
    _ffn_accumulate(x_ref, g_ref, lambda: (wg_ref[...], wu_ref[...], wd_ref[...]), o_ref, h_ref)


def _ffn(x, gain, wg, wu, wd, *, in_place, tm=1024, tf_first=256, tf=512):
    t, d = x.shape
    f = wg.shape[1]
    alias = {0: 0} if in_place else {}

    def w_specs(tfx):
        return [pl.BlockSpec((d, tfx), lambda i, j: (0, j)),
                pl.BlockSpec((d, tfx), lambda i, j: (0, j)),
                pl.BlockSpec((tfx, d), lambda i, j: (j, 0))]

    y, wg_b, wu_b, wd_b = pl.pallas_call(
        _ffn_first_kernel,
        grid=(1, f // tf_first),
        in_specs=[pl.BlockSpec((tm, d), lambda i, j: (0, 0), pipeline_mode=pl.Buffered(1)),
                  pl.BlockSpec((1, d), lambda i, j: (0, 0))] + w_specs(tf_first),
        out_specs=[pl.BlockSpec((tm, d), lambda i, j: (0, 0))] + w_specs(tf_first),
        out_shape=[jax.ShapeDtypeStruct((t if in_place else tm, d), F32), jax.ShapeDtypeStruct((d, f), BF16),
                   jax.ShapeDtypeStruct((d, f), BF16), jax.ShapeDtypeStruct((f, d), BF16)],
        scratch_shapes=[pltpu.VMEM((tm, d), BF16)],
        input_output_aliases=alias,
        compiler_params=_cparams(("arbitrary", "arbitrary")),
        name="ffn_first",
    )(x, gain, wg, wu, wd)

    out_row0 = 1 if in_place else 0
    rest = pl.pallas_call(
        _ffn_rest_kernel,
        grid=(t // tm - 1, f // tf),
        in_specs=[pl.BlockSpec((tm, d), lambda i, j: (i + 1, 0)),
                  pl.BlockSpec((1, d), lambda i, j: (0, 0))] + w_specs(tf),
        out_specs=pl.BlockSpec((tm, d), lambda i, j: (i + out_row0, 0)),
        out_shape=jax.ShapeDtypeStruct((t if in_place else t - tm, d), F32),
        scratch_shapes=[pltpu.VMEM((tm, d), BF16)],
        input_output_aliases=alias,
        compiler_params=_cparams(("parallel", "arbitrary")),
        name="ffn_rest",
    )(y if in_place else x, gain, wg_b, wu_b, wd_b)
    return rest if in_place else (y, rest)


_N_CMP = 0
_N_KS = _N_CMP + 2 * NSA_KV_WIDTH
_N_KW = _N_KS + NSA_KV_WIDTH
_N_CQ = _N_KW + NSA_KV_WIDTH
_N_CKV = _N_CQ + MLA_Q_RANK
_N_KR = _N_CKV + MLA_KV_RANK
_N_END = _N_KR + 2 * LANES
_T_Q = 0
_T_VS = _T_Q + NSA_WIDTH
_T_VW = _T_VS + NSA_KV_WIDTH
_T_GATE = _T_VW + NSA_KV_WIDTH
_T_END = _T_GATE + LANES


def _two_part_specs(parts, tm):
    first, rest = parts
    nb_first = first.shape[0] // tm
    d = first.shape[1]
    return nb_first, [pl.BlockSpec((tm, d), lambda i: (jnp.minimum(i, nb_first - 1), 0)),
                      pl.BlockSpec((tm, d), lambda i: (jnp.maximum(i - nb_first, 0), 0))]


def _two_part_rows(first_ref, rest_ref, nb_first):
    return jnp.where(pl.program_id(0) < nb_first, first_ref[...], rest_ref[...])


def _mla_project(cq, ckv, k_rope, k_rot, tables, weights, gains, q_t_ref, k_ref, v_t_ref):
    cos, sin, cos_t, sin_t = (r[...] for r in tables)
    wq_t_ref, wqr_t_ref, wk_ref, wv_t_ref = weights
    qgn, qgr, qgt, kgn, kgr, kgt = (r[...] for r in gains)
    tq = q_t_ref.shape[-1]
    k_rope_ssq = jnp.sum(k_rope * k_rope, axis=-1, keepdims=True)
    k_rope_emb = k_rope * kgr * cos + k_rot * kgt * sin
    q_cos = qgr * cos_t
    q_sin = qgt * sin_t
    qscale = MLA_QK_DIM ** -0.5
    hg = MLA_HEAD_GROUP
    for hd in range(MLA_HEADS):
        if hd % hg == 0:
            qx_all = _dot_nt(wq_t_ref[hd * MLA_QK_PAD:(hd + hg) * MLA_QK_PAD, :], cq)
            q_rot_all = _dot_nt(wqr_t_ref[hd * LANES:(hd + hg) * LANES, :], cq)
            k_nope_all = _dot(ckv, wk_ref[:, hd * LANES:(hd + hg) * LANES])
            v_all = _dot_nt(wv_t_ref[hd * LANES:(hd + hg) * LANES, :], ckv)
        lo = hd % hg
        q_nope = qx_all[lo * MLA_QK_PAD:lo * MLA_QK_PAD + LANES]
        q_rope = qx_all[lo * MLA_QK_PAD + LANES:(lo + 1) * MLA_QK_PAD]
        q_rot = q_rot_all[lo * LANES:(lo + 1) * LANES]
        ssq = jnp.sum(q_nope * q_nope, axis=0, keepdims=True) + jnp.sum(q_rope * q_rope, axis=0, keepdims=True)
        inv = lax.rsqrt(ssq / MLA_QK_DIM + EPS) * qscale
        qn_nope = (q_nope * inv * qgn).astype(BF16)
        qn_rope = ((q_rope * q_cos + q_rot * q_sin) * inv).astype(BF16)
        for c in range(q_t_ref.shape[1]):
            q_t_ref[0, c, hd, 0:LANES, :] = qn_nope[:, c * tq:(c + 1) * tq]
            q_t_ref[0, c, hd, LANES:2 * LANES, :] = qn_rope[:, c * tq:(c + 1) * tq]

        k_nope = k_nope_all[:, lo * LANES:(lo + 1) * LANES]
        ssq = jnp.sum(k_nope * k_nope, axis=-1, keepdims=True) + k_rope_ssq
        inv = lax.rsqrt(ssq / MLA_QK_DIM + EPS)
        k_ref[0, hd, :, 0:LANES] = (k_nope * inv * kgn).astype(BF16)
        k_ref[0, hd, :, LANES:2 * LANES] = (k_rope_emb * inv).astype(BF16)
        v_t_ref[0, hd] = v_all[lo * LANES:(lo + 1) * LANES].astype(BF16)


def _proj_kernel(xa_ref, xb_ref, g_ref, wn_ref, wt_ref, qg_ref, kg_ref, qag_ref, kvag_ref, *rest, nb_first):
    mla_tables, mla_weights, mla_gains = rest[0:4], rest[4:8], rest[8:14]
    q_t_ref, cmp_ref, ks_ref, vs_t_ref, kw_ref, vw_t_ref, gate_t_ref, qm_t_ref, km_ref, vm_t_ref = rest[14:]
    tq = q_t_ref.shape[2]
    h = _rms(_two_part_rows(xa_ref, xb_ref, nb_first), g_ref[...]).astype(BF16)
    n_cuts = (_N_CMP, _N_CQ, _N_END)
    t_cuts = (_T_Q, _T_VS, _T_END)
    p_n = [_dot(h, wn_ref[:, a:b]) for a, b in zip(n_cuts[:-1], n_cuts[1:])]
    p_t = [_dot_nt(wt_ref[a:b, :], h) for a, b in zip(t_cuts[:-1], t_cuts[1:])]

    def proj(lo, hi):
        c = max(i for i, a in enumerate(n_cuts[:-1]) if a <= lo)
        return p_n[c][:, lo - n_cuts[c]:hi - n_cuts[c]]

    def proj_t(lo, hi):
        c = max(i for i, a in enumerate(t_cuts[:-1]) if a <= lo)
        return p_t[c][lo - t_cuts[c]:hi - t_cuts[c], :]

    qscale = HEAD_DIM ** -0.5
    for hd in range(NSA_HEADS):
        q = proj_t(_T_Q + hd * HEAD_DIM, _T_Q + (hd + 1) * HEAD_DIM)
        inv = lax.rsqrt(jnp.mean(q * q, axis=0, keepdims=True) + EPS) * qscale
        qn = (q * inv * qg_ref[...]).astype(BF16)
        for c in range(q_t_ref.shape[0]):
            q_t_ref[c, hd * HEAD_DIM:(hd + 1) * HEAD_DIM, :] = qn[:, c * tq:(c + 1) * tq]
    for c in range(cmp_ref.shape[0]):
        cmp_ref[c] = proj(_N_CMP + c * HEAD_DIM, _N_CMP + (c + 1) * HEAD_DIM)
    for g in range(NSA_KV_HEADS):
        sl = slice(g * HEAD_DIM, (g + 1) * HEAD_DIM)
        ks_ref[:, sl] = _rms(proj(_N_KS + g * HEAD_DIM, _N_KS + (g + 1) * HEAD_DIM), kg_ref[1:2, :]).astype(BF16)
        kw_ref[:, sl] = _rms(proj(_N_KW + g * HEAD_DIM, _N_KW + (g + 1) * HEAD_DIM), kg_ref[2:3, :]).astype(BF16)
    vs_t_ref[...] = proj_t(_T_VS, _T_VW).astype(BF16)
    vw_t_ref[...] = proj_t(_T_VW, _T_GATE).astype(BF16)
    gates = _sigmoid(proj_t(_T_GATE, _T_END))
    for c in range(gate_t_ref.shape[0]):
        gate_t_ref[c] = gates[:, c * tq:(c + 1) * tq]
    cq = _rms(proj(_N_CQ, _N_CKV), qag_ref[...]).astype(BF16)
    ckv = _rms(proj(_N_CKV, _N_KR), kvag_ref[...]).astype(BF16)
    _mla_project(cq, ckv, proj(_N_KR, _N_KR + LANES), proj(_N_KR + LANES, _N_END),
                 mla_tables, mla_weights, mla_gains, qm_t_ref, km_ref, vm_t_ref)


def _proj(x_parts, seq, gain, w_n, w_t, q_gain, k_gains, qa_gain, kva_gain, mla_tables, mla_weights, mla_q_gains,
          mla_k_gains, *, tm=512, tq=NSA_Q_TILE, tq_mla=MLA_Q_TILE):
    t = x_parts[0].shape[0] + x_parts[1].shape[0]
    nb_first, x_specs = _two_part_specs(x_parts, tm)
    q_gain = jnp.broadcast_to(q_gain.reshape(HEAD_DIM, 1), (HEAD_DIM, tm))
    mla_q_gains = tuple(jnp.broadcast_to(g.reshape(LANES, 1), (LANES, tm)) for g in mla_q_gains)
    cos, sin = mla_tables
    nj = seq // tm
    pos_rows = pl.BlockSpec((tm, LANES), lambda i: (i % nj, 0))
    pos_cols = pl.BlockSpec((LANES, tm), lambda i: (0, i % nj))

    def full(a):
        return pl.BlockSpec(a.shape, lambda i: (0,) * a.ndim)

    def rows(width):
        return (t, width), pl.BlockSpec((tm, width), lambda i: (i, 0))

    def cols(height):
        return (height, t), pl.BlockSpec((height, tm), lambda i: (0, i))

    def q_tiles(height):
        return (t // tq, height, tq), pl.BlockSpec((tm // tq, height, tq), lambda i: (i, 0, 0))

    n_cmp_cols = 2 * NSA_KV_WIDTH // HEAD_DIM
    cmp_out = (n_cmp_cols, t, HEAD_DIM), pl.BlockSpec((n_cmp_cols, tm, HEAD_DIM), lambda i: (0, i, 0))
    b = t // seq
    mla_q = ((b, seq // tq_mla, MLA_HEADS, MLA_QK_PAD, tq_mla),
             pl.BlockSpec((1, tm // tq_mla, MLA_HEADS, MLA_QK_PAD, tq_mla), lambda i: (i // nj, i % nj, 0, 0, 0)))
    mla_k = ((b, MLA_HEADS, seq, MLA_QK_PAD),
             pl.BlockSpec((1, MLA_HEADS, tm, MLA_QK_PAD), lambda i: (i // nj, 0, i % nj, 0)))
    mla_v = ((b, MLA_HEADS, MLA_V_DIM, seq),
             pl.BlockSpec((1, MLA_HEADS, MLA_V_DIM, tm), lambda i: (i // nj, 0, 0, i % nj)))
    outs = [(q_tiles(NSA_WIDTH), BF16), (cmp_out, F32), (rows(NSA_KV_WIDTH), BF16),
            (cols(NSA_KV_WIDTH), BF16), (rows(NSA_KV_WIDTH), BF16), (cols(NSA_KV_WIDTH), BF16),
            (q_tiles(LANES), F32), (mla_q, BF16), (mla_k, BF16), (mla_v, BF16)]
    mla_in = (cos, sin, cos.T, sin.T) + tuple(mla_weights) + mla_q_gains + tuple(mla_k_gains)
    mla_specs = [pos_rows, pos_rows, pos_cols, pos_cols] + [full(a) for a in mla_in[4:]]
    return pl.pallas_call(
        functools.partial(_proj_kernel, nb_first=nb_first),
        grid=(t // tm,),
        in_specs=x_specs + [full(gain), full(w_n), full(w_t), full(q_gain), full(k_gains), full(qa_gain),
                            full(kva_gain)] + mla_specs,
        out_specs=[spec for (_, spec), _ in outs],
        out_shape=[jax.ShapeDtypeStruct(shape, dt) for (shape, _), dt in outs],
        compiler_params=_cparams(("parallel",)),
        name="proj",
    )(*x_parts, gain, w_n, w_t, q_gain, k_gains, qa_gain, kva_gain, *mla_in)


def _gelu_tanh(x):
    return 0.5 * x * (1.0 + jnp.tanh(np.sqrt(2.0 / np.pi) * (x + 0.044715 * (x * x * x))))


def _compress_kernel(x_ref, w1k_ref, w2k_ref, pk_ref, w1v_ref, w2v_ref, pv_ref, kg_ref, kc_ref, vc_t_ref):
    half = CMP_STRIDE * HEAD_DIM
    nrow = x_ref.shape[1] // CMP_STRIDE
    for which, (w1_ref, w2_ref, p_ref) in enumerate(((w1k_ref, w2k_ref, pk_ref), (w1v_ref, w2v_ref, pv_ref))):
        w1 = w1_ref[...].astype(BF16)
        w2 = w2_ref[...].astype(BF16)
        pos_bias = _dot(jnp.broadcast_to(p_ref[...], (8, 2 * half)).astype(BF16), w1)[0:1, :]
        for g in range(NSA_KV_HEADS):
            col = which * NSA_KV_HEADS + g
            xg = jnp.concatenate(
                [x_ref[col, pl.ds(l, nrow, stride=CMP_STRIDE), :].astype(BF16) for l in range(CMP_STRIDE)], axis=1)
            first = _dot(xg, w1[0:half, :])
            second = _dot(xg, w1[half:2 * half, :])
            pre = first + pltpu.roll(second, nrow - 1, 0) + pos_bias
            hidden = _gelu_tanh(pre).astype(BF16)
            if which == 0:
                kc_ref[0, g] = _rms(_dot(hidden, w2), kg_ref[0:1, :]).astype(BF16)
            else:
                vc_t_ref[0, g] = _dot_nt(w2, hidden).astype(BF16)


def _compress(xc, seq, w1k, w2k, pk, w1v, w2v, pv, k_gains):
    ncol, t, width = xc.shape
    b, nrow = t // seq, seq // CMP_STRIDE

    def full(a):
        return pl.BlockSpec(a.shape, lambda i: (0,) * a.ndim)

    out_sds = jax.ShapeDtypeStruct((b, NSA_KV_HEADS, nrow, HEAD_DIM), BF16)
    out_spec = pl.BlockSpec((1, NSA_KV_HEADS, nrow, HEAD_DIM), lambda i: (i, 0, 0, 0))
    return pl.pallas_call(
        _compress_kernel,
        grid=(b,),
        in_specs=[pl.BlockSpec((ncol, seq, width), lambda i: (0, i, 0)),
                  full(w1k), full(w2k), full(pk), full(w1v), full(w2v), full(pv), full(k_gains)],
        out_specs=[out_spec, out_spec],
        out_shape=[out_sds, out_sds],
        compiler_params=_cparams(("parallel",)),
        name="compress",
    )(xc, w1k, w2k, pk, w1v, w2v, pv, k_gains)


def _softmax_cols(s):
    m = jnp.max(s, axis=0, keepdims=True)
    p = jnp.exp(s - m)
    return m, jnp.sum(p, axis=0, keepdims=True), p


def _nsa_kernel(q_t_ref, kc_ref, vc_t_ref, ks_ref, vs_t_ref, kw_ref, vw_t_ref, gate_t_ref, kfeat_ref, cfeat_ref,
                slope_ref, o_ref, *, tq, ck):
    qi = pl.program_id(1)
    q0 = qi * tq
    ng = NSA_KV_HEADS
    wide = NSA_GROUP * tq
    n_sel = ks_ref.shape[1] // SEL_BLOCK
    ncmp = kc_ref.shape[2]
    n_win = WINDOW + tq

    def tile4(a):
        return jnp.concatenate([a] * NSA_GROUP, axis=1)

    def masked(s, mask, fill=NEG):
        return jnp.concatenate([jnp.where(mask, s[:, r * tq:(r + 1) * tq], fill) for r in range(NSA_GROUP)], axis=1)

    def q_aug(g, extra):
        q = jnp.concatenate([q_t_ref[0, (g * NSA_GROUP + r) * HEAD_DIM:(g * NSA_GROUP + r + 1) * HEAD_DIM, :]
                             for r in range(NSA_GROUP)], axis=1)
        rows = [q, slope_ref[g]]
        used = slope_ref.shape[1]
        if extra is not None:
            rows.append(extra)
            used += extra.shape[0]
        rows.append(jnp.zeros((HEAD_DIM - used, wide), BF16))
        return jnp.concatenate(rows, axis=0)

    def gsl(g):
        return slice(g * HEAD_DIM, (g + 1) * HEAD_DIM)

    lane_q = q0 + lax.broadcasted_iota(jnp.int32, (1, tq), 1)

    qa_plain = [q_aug(g, None) for g in range(ng)]
    w0 = pl.multiple_of(jnp.maximum(q0 - WINDOW, 0), tq)
    s_cmp = [_dot(jnp.concatenate([kc_ref[0, g], cfeat_ref[...]], axis=1), qa_plain[g]) for g in range(ng)]
    s_win = [_dot(jnp.concatenate([kw_ref[0, pl.ds(w0, n_win), gsl(g)], kfeat_ref[pl.ds(w0, n_win), :]], axis=1),
                  qa_plain[g]) for g in range(ng)]

    n_row = lax.broadcasted_iota(jnp.int32, (ncmp, tq), 0)
    mask_c = n_row * CMP_STRIDE + (CMP_BLOCK - 1) <= lane_q
    s_row = lax.broadcasted_iota(jnp.int32, (n_sel, ncmp), 0)
    n_col = lax.broadcasted_iota(jnp.int32, (n_sel, ncmp), 1)
    overlap_t = jnp.where((n_col * CMP_STRIDE < (s_row + 1) * SEL_BLOCK)
                          & (n_col * CMP_STRIDE + (CMP_BLOCK - 1) >= s_row * SEL_BLOCK), 1.0, 0.0).astype(BF16)
    p_cmp, imp = [], []
    for g in range(ng):
        s = masked(s_cmp[g], mask_c)
        m = jnp.max(s, axis=0, keepdims=True)
        p = masked(jnp.exp(s - m), mask_c, 0.0)
        l = jnp.sum(p, axis=0, keepdims=True)
        p = p / jnp.where(l > 0.0, l, 1.0)
        p_cmp.append(p.astype(BF16))
        p_sum = p[:, 0:tq]
        for r in range(1, NSA_GROUP):
            p_sum = p_sum + p[:, r * tq:(r + 1) * tq]
        p_hi = p_sum.astype(BF16)
        p_lo = (p_sum - p_hi.astype(F32)).astype(BF16)
        imp.append(_dot(overlap_t, p_hi) + _dot(overlap_t, p_lo))
    o_cmp = [_dot(vc_t_ref[0, g], p_cmp[g]) for g in range(ng)]

    blk = lax.broadcasted_iota(jnp.int32, (n_sel, tq), 0)
    blk_f = blk.astype(F32)
    forced = (blk == 0) | (blk == jnp.right_shift(lane_q, SEL_SHIFT))
    future = blk * SEL_BLOCK > lane_q
    penalty = []
    for g in range(ng):
        v = jnp.where(future, NEG, jnp.where(forced, FORCE_SCORE, imp[g]))
        unselected = jnp.full((n_sel, tq), NEG, F32)
        for _ in range(min(SEL_TOP, n_sel)):
            mx = jnp.max(v, axis=0, keepdims=True)
            first = jnp.min(jnp.where(v == mx, blk_f, float(n_sel)), axis=0, keepdims=True)
            pick = blk_f == first
            unselected = jnp.where(pick, 0.0, unselected)
            v = jnp.where(pick, -jnp.inf, v)
        penalty.append(tile4(unselected.astype(BF16)))

    rel = lax.broadcasted_iota(jnp.int32, (n_win, tq), 0) - lax.broadcasted_iota(jnp.int32, (n_win, tq), 1)
    off = q0 - w0
    mask_w = (rel <= off) & (rel > off - WINDOW)
    o_win = []
    p_win = []
    for g in range(ng):
        _, l, p = _softmax_cols(masked(s_win[g], mask_w))
        p_win.append((l, p.astype(BF16)))
    for g in range(ng):
        l, p = p_win[g]
        o_win.append(_dot(vw_t_ref[gsl(g), pl.ds(w0, n_win)], p) / l)

    partial = []
    for g in range(ng):
        for r in range(NSA_GROUP):
            hd = g * NSA_GROUP + r
            cs = slice(r * tq, (r + 1) * tq)
            partial.append(gate_t_ref[0, 3 * hd:3 * hd + 1, :] * o_cmp[g][:, cs]
                           + gate_t_ref[0, 3 * hd + 2:3 * hd + 3, :] * o_win[g][:, cs])

    qa_sel = [q_aug(g, penalty[g]) for g in range(ng)]
    rel_d = lax.broadcasted_iota(jnp.int32, (ck, tq), 0) - lax.broadcasted_iota(jnp.int32, (ck, tq), 1)

    def select_and_store(n_before):
        nk = n_before + ck
        mask_d = rel_d <= q0 - n_before
        ss = [_dot(jnp.concatenate([ks_ref[0, 0:nk, gsl(g)], kfeat_ref[0:nk, :]], axis=1), qa_sel[g])
              for g in range(ng)]
        for g in range(ng):
            s_diag = masked(ss[g][n_before:nk], mask_d)
            m = jnp.max(s_diag, axis=0, keepdims=True)
            if n_before:
                m = jnp.maximum(m, jnp.max(ss[g][0:n_before], axis=0, keepdims=True))
            p_diag = jnp.exp(s_diag - m)
            l = jnp.sum(p_diag, axis=0, keepdims=True)
            p = p_diag.astype(BF16)
            if n_before:
                p_before = jnp.exp(ss[g][0:n_before] - m)
                l = l + jnp.sum(p_before, axis=0, keepdims=True)
                p = jnp.concatenate([p_before.astype(BF16), p], axis=0)
            o_sel = _dot(vs_t_ref[gsl(g), 0:nk], p) / l
            for r in range(NSA_GROUP):
                hd = g * NSA_GROUP + r
                mixed = partial[hd] + gate_t_ref[0, 3 * hd + 1:3 * hd + 2, :] * o_sel[:, r * tq:(r + 1) * tq]
                o_ref[0, :, hd * HEAD_DIM:(hd + 1) * HEAD_DIM] = mixed.T

    per = ck // tq
    for v in range(ks_ref.shape[1] // ck):
        pl.when(qi // per == v)(functools.partial(select_and_store, v * ck))


def _nsa(q_t, kc, vc_t, ks, vs_t, kw, vw_t, gate_t, *, tq=NSA_Q_TILE, ck=256):
    b, s, _ = ks.shape
    nq = s // tq
    pos = np.arange(s)
    kfeat = np.zeros((s, LANES), np.float32)
    kfeat[:, 0] = POS_SPLIT * (pos // POS_SPLIT)
    kfeat[:, 1] = pos % POS_SPLIT
    kfeat[pos, FEAT_ROWS + pos // SEL_BLOCK] = 1.0
    ncmp = kc.shape[2]
    cfeat = np.zeros((ncmp, LANES), np.float32)
    cfeat[:, 0] = CMP_STRIDE * np.arange(ncmp)
    cfeat[:, 1] = 0.5 * (CMP_BLOCK - 1)
    slope = np.zeros((NSA_KV_HEADS, FEAT_ROWS, NSA_GROUP * tq), np.float32)
    for hd in range(NSA_HEADS):
        g, r = divmod(hd, NSA_GROUP)
        slope[g, 0:2, r * tq:(r + 1) * tq] = 2.0 ** (-8.0 * (hd + 1) / NSA_HEADS)
    assert FEAT_ROWS + s // SEL_BLOCK <= LANES and s // SEL_BLOCK % FEAT_ROWS == 0 and s % ck == 0 and ck % tq == 0

    def full(a):
        return pl.BlockSpec(a.shape, lambda i, j: (0,) * a.ndim)

    def per_b(a):
        return pl.BlockSpec((1,) + a.shape[1:], lambda i, j: (i,) + (0,) * (a.ndim - 1))

    def per_b_cols(a):
        return pl.BlockSpec((a.shape[0], s), lambda i, j: (0, i))

    def q_cols(a):
        assert a.shape[0] == b * nq and a.shape[2] == tq
        return pl.BlockSpec((1, a.shape[1], tq), lambda i, j: (i * nq + j, 0, 0))

    consts = [jnp.asarray(kfeat, BF16), jnp.asarray(cfeat, BF16), jnp.asarray(slope, BF16)]
    return pl.pallas_call(
        functools.partial(_nsa_kernel, tq=tq, ck=ck),
        grid=(b, nq),
        in_specs=[q_cols(q_t), per_b(kc), per_b(vc_t), per_b(ks), per_b_cols(vs_t), per_b(kw), per_b_cols(vw_t),
                  q_cols(gate_t)] + [full(c) for c in consts],
        out_specs=pl.BlockSpec((1, tq, NSA_WIDTH), lambda i, j: (i, j, 0)),
        out_shape=jax.ShapeDtypeStruct((b, s, NSA_WIDTH), F32),
        compiler_params=_cparams(("parallel", "arbitrary")),
        name="nsa",
    )(q_t, kc, vc_t, ks, vs_t, kw, vw_t, gate_t, *consts)


def _mla_attn_kernel(q_t_ref, k_ref, v_t_ref, o_ref, *, tq, hb, n_q):
    qi = pl.program_id(2)
    d_v = v_t_ref.shape[2]
    tri = lax.broadcasted_iota(jnp.int32, (tq, tq), 0) <= lax.broadcasted_iota(jnp.int32, (tq, tq), 1)

    def attend(n_before):
        nk = n_before + tq
        ss = [_dot(k_ref[0, h, 0:nk, :], q_t_ref[0, 0, h]) for h in range(hb)]
        for h in range(hb):
            s_diag = jnp.where(tri, ss[h][n_before:nk], NEG)
            m = jnp.max(s_diag, axis=0, keepdims=True)
            if n_before:
                m = jnp.maximum(m, jnp.max(ss[h][0:n_before], axis=0, keepdims=True))
            p_diag = jnp.exp(s_diag - m)
            l = jnp.sum(p_diag, axis=0, keepdims=True)
            p = p_diag.astype(BF16)
            if n_before:
                p_before = jnp.exp(ss[h][0:n_before] - m)
                l = l + jnp.sum(p_before, axis=0, keepdims=True)
                p = jnp.concatenate([p_before.astype(BF16), p], axis=0)
            o_ref[0, :, h * d_v:(h + 1) * d_v] = (_dot(v_t_ref[0, h, :, 0:nk], p) / l).T

    for v in range(n_q):
        pl.when(qi == v)(functools.partial(attend, v * tq))


def _mla_attn(q_t, k, v_t, *, tq=MLA_Q_TILE, hb=MLA_HEADS):
    b, h, s, dq = k.shape
    dv = v_t.shape[2]
    assert q_t.shape == (b, s // tq, h, dq, tq)
    return pl.pallas_call(
        functools.partial(_mla_attn_kernel, tq=tq, hb=hb, n_q=s // tq),
        grid=(b, h // hb, s // tq),
        in_specs=[pl.BlockSpec((1, 1, hb, dq, tq), lambda i, j, l: (i, l, j, 0, 0)),
                  pl.BlockSpec((1, hb, s, dq), lambda i, j, l: (i, j, 0, 0)),
                  pl.BlockSpec((1, hb, dv, s), lambda i, j, l: (i, j, 0, 0))],
        out_specs=pl.BlockSpec((1, tq, hb * dv), lambda i, j, l: (i, l, j)),
        out_shape=jax.ShapeDtypeStruct((b, s, h * dv), F32),
        compiler_params=_cparams(("parallel", "parallel", "arbitrary")),
        name="mla_attn",
    )(q_t, k, v_t)


def _out_kernel(xa_ref, xb_ref, oa_ref, ob_ref, ga_ref, gb_ref, w_ref, o_ref, *, nb_first):
    oa = _rms(oa_ref[...], ga_ref[...]).astype(BF16)
    ob = _rms(ob_ref[...], gb_ref[...]).astype(BF16)
    o_ref[...] = (_two_part_rows(xa_ref, xb_ref, nb_first)
                  + _dot(jnp.concatenate([oa, ob], axis=1), w_ref[...]))


def _out_proj(x_parts, oa, ob, gain_a, gain_b, w, *, tm=512):
    t, d = oa.shape[0], x_parts[0].shape[1]
    nb_first, x_specs = _two_part_specs(x_parts, tm)

    def rows(width):
        return pl.BlockSpec((tm, width), lambda i: (i, 0))

    def full(a):
        return pl.BlockSpec(a.shape, lambda i: (0,) * a.ndim)

    return pl.pallas_call(
        functools.partial(_out_kernel, nb_first=nb_first),
        grid=(t // tm,),
        in_specs=x_specs + [rows(oa.shape[1]), rows(ob.shape[1]), full(gain_a), full(gain_b), full(w)],
        out_specs=rows(d),
        out_shape=jax.ShapeDtypeStruct((t, d), F32),
        compiler_params=_cparams(("parallel",)),
        name="out_proj",
    )(*x_parts, oa, ob, gain_a, gain_b, w)


def _rot_cols(w):
    half = w.shape[1] // 2
    return jnp.concatenate([-w[:, half:], w[:, :half]], axis=1)


def _layout_w_in(w_in):
    d = w_in.shape[0]
    off = np.cumsum([0, NSA_WIDTH] + [NSA_KV_WIDTH] * 6 + [NSA_HEADS * 3, MLA_Q_RANK, MLA_KV_RANK, MLA_ROPE_DIM])
    q, kc, vc, ks, vs, kw, vw, gt, cq, ckv, kr = [w_in[:, off[i]:off[i + 1]] for i in range(11)]
    zr = jnp.zeros((d, LANES - MLA_ROPE_DIM), w_in.dtype)
    zg = jnp.zeros((d, LANES - NSA_HEADS * 3), w_in.dtype)
    w_n = jnp.concatenate([kc, vc, ks, kw, cq, ckv, kr, zr, _rot_cols(kr), zr], axis=1).astype(BF16)
    w_t = jnp.concatenate([q, vs, vw, gt, zg], axis=1).astype(BF16).T
    return w_n, w_t


def _layout_mla_weights(w_uq, w_ukv):
    r = w_uq.shape[0]
    zr = jnp.zeros((r, LANES - MLA_ROPE_DIM), w_uq.dtype)
    main, rot, k_cols, v_cols = [], [], [], []
    for hd in range(MLA_HEADS):
        base = hd * MLA_QK_DIM
        rope = w_uq[:, base + MLA_NOPE_DIM:base + MLA_QK_DIM]
        main += [w_uq[:, base:base + MLA_NOPE_DIM], rope, zr]
        rot += [_rot_cols(rope), zr]
        kv = hd * (MLA_NOPE_DIM + MLA_V_DIM)
        k_cols.append(w_ukv[:, kv:kv + MLA_NOPE_DIM])
        v_cols.append(w_ukv[:, kv + MLA_NOPE_DIM:kv + MLA_NOPE_DIM + MLA_V_DIM])
    cat = lambda parts: jnp.concatenate(parts, axis=1).astype(BF16)
    return cat(main).T, cat(rot).T, cat(k_cols), cat(v_cols).T


def _rope_gains(gain):
    half = MLA_ROPE_DIM // 2
    zr = jnp.zeros((LANES - MLA_ROPE_DIM,), gain.dtype)
    g_rope = gain[MLA_NOPE_DIM:]
    return (gain[None, :MLA_NOPE_DIM], jnp.concatenate([g_rope, zr])[None, :],
            jnp.concatenate([g_rope[half:], g_rope[:half], zr])[None, :])


def _rope_tables(s):
    half = MLA_ROPE_DIM // 2
    inv = ROPE_THETA ** (-jnp.arange(half, dtype=F32) / half)
    ang = jnp.arange(s).astype(F32)[:, None] * inv[None, :]
    zr = jnp.zeros((s, LANES - MLA_ROPE_DIM), F32)
    cos, sin = jnp.cos(ang), jnp.sin(ang)
    return jnp.concatenate([cos, cos, zr], axis=1), jnp.concatenate([sin, sin, zr], axis=1)


def kernel(x, ffn1_norm, ffn1_w_gate, ffn1_w_up, ffn1_w_down, mix_norm, w_in, nsa_q_norm, nsa_k_norm, nsa_cmp_pos_k, nsa_cmp_w1_k, nsa_cmp_w2_k, nsa_cmp_pos_v, nsa_cmp_w1_v, nsa_cmp_w2_v, mla_q_a_norm, mla_w_uq, mla_kv_a_norm, mla_w_ukv, mla_q_norm, mla_k_norm, out_norm_nsa, out_norm_mla, w_out, ffn2_norm, ffn2_w_gate, ffn2_w_up, ffn2_w_down):
    b, s, d = x.shape
    depth = ffn1_norm.shape[0]
    tables = _rope_tables(s)
    xt = x.reshape(b * s, d)
    for l in range(depth):
        x_parts = _ffn(xt, ffn1_norm[l][None, :], ffn1_w_gate[l], ffn1_w_up[l], ffn1_w_down[l], in_place=False)

        w_n, w_t = _layout_w_in(w_in[l])
        q_t, kv_cmp, ks, vs_t, kw, vw_t, gate_t, q_m, k_m, v_m = _proj(
            x_parts, s, mix_norm[l][None, :], w_n, w_t, nsa_q_norm[l], nsa_k_norm[l],
            mla_q_a_norm[l][None, :], mla_kv_a_norm[l][None, :],
            tables, _layout_mla_weights(mla_w_uq[l], mla_w_ukv[l]),
            _rope_gains(mla_q_norm[l]), _rope_gains(mla_k_norm[l]))

        o_b = _mla_attn(q_m, k_m, v_m)

        kc, vc_t = _compress(
            kv_cmp, s,
            nsa_cmp_w1_k[l], nsa_cmp_w2_k[l], nsa_cmp_pos_k[l].reshape(1, -1),
            nsa_cmp_w1_v[l], nsa_cmp_w2_v[l].T, nsa_cmp_pos_v[l].reshape(1, -1),
            nsa_k_norm[l])

        def bs(a):
            return a.reshape(b, s, a.shape[-1])

        o_a = _nsa(q_t, kc, vc_t, bs(ks), vs_t, bs(kw), vw_t, gate_t)

        xt = _out_proj(x_parts, o_a.reshape(b * s, NSA_WIDTH), o_b.reshape(b * s, MLA_WIDTH),
                       out_norm_nsa[l][None, :], out_norm_mla[l][None, :], w_out[l].astype(BF16))

        xt = _ffn(xt, ffn2_norm[l][None, :], ffn2_w_gate[l], ffn2_w_up[l], ffn2_w_down[l], in_place=True)
    return xt.reshape(b, s, d)
```

```python
import functools

import numpy as np
import jax
import jax.numpy as jnp
from jax import lax
from jax.experimental import pallas as pl
from jax.experimental.pallas import tpu as pltpu

F32 = jnp.float32
BF16 = jnp.bfloat16

HEAD_DIM = 128
NSA_HEADS = 8
NSA_KV_HEADS = 2
NSA_GROUP = NSA_HEADS // NSA_KV_HEADS
CMP_BLOCK = 32
CMP_STRIDE = 16
SEL_BLOCK = 64
SEL_SHIFT = 6
SEL_TOP = 8
WINDOW = 512
MLA_HEADS = 8
MLA_Q_RANK = 384
MLA_KV_RANK = 256
MLA_NOPE_DIM = 128
MLA_ROPE_DIM = 64
MLA_V_DIM = 128
MLA_QK_DIM = MLA_NOPE_DIM + MLA_ROPE_DIM
ROPE_THETA = 10000.0
EPS = 1e-6
NEG = -1e30
FORCE_SCORE = 1e4
NSA_WIDTH = NSA_HEADS * HEAD_DIM
MLA_WIDTH = MLA_HEADS * MLA_V_DIM
NSA_KV_WIDTH = NSA_KV_HEADS * HEAD_DIM

LANES = 128
MLA_QK_PAD = 2 * LANES
MLA_HEAD_GROUP = 2
VMEM_PHYSICAL_MIB = 64
VMEM_MIB = {"ffn_first": 44, "ffn_rest": 56, "proj": 58, "compress": 24, "nsa": 40, "mla_attn": 46, "out_proj": 46}
assert max(VMEM_MIB.values()) < VMEM_PHYSICAL_MIB
POS_SPLIT = 16
FEAT_ROWS = 16
NSA_Q_TILE = 256
MLA_Q_TILE = 256


def _cparams(name, sem):
    return pltpu.CompilerParams(dimension_semantics=sem, vmem_limit_bytes=VMEM_MIB[name] * 1024 * 1024)


def _rms(x, gain):
    return x * lax.rsqrt(jnp.mean(x * x, axis=-1, keepdims=True) + EPS) * gain


def _dot(a, b):
    return jnp.dot(a, b, preferred_element_type=F32)


def _dot_nt(a, b):
    return lax.dot_general(a, b, (((1,), (1,)), ((), ())), preferred_element_type=F32)


def _sigmoid(x):
    return 1.0 / (1.0 + jnp.exp(-x))


def _ffn_accumulate(x_ref, g_ref, weights, o_ref, h_ref):
    @pl.when(pl.program_id(1) == 0)
    def _():
        x = x_ref[...]
        h_ref[...] = _rms(x, g_ref[...]).astype(BF16)
        o_ref[...] = x

    wg, wu, wd = weights()
    h = h_ref[...]
    gate = _dot(h, wg)
    up = _dot(h, wu)
    act = gate * _sigmoid(gate) * up * 0.5
    o_ref[...] += _dot(act.astype(BF16), wd)


def _ffn_first_kernel(x_ref, g_ref, wg_ref, wu_ref, wd_ref, o_ref, wg_out, wu_out, wd_out, h_ref):
    def weights():
        wg, wu, wd = wg_ref[...].astype(BF16), wu_ref[...].astype(BF16), wd_ref[...].astype(BF16)
        wg_out[...], wu_out[...], wd_out[...] = wg, wu, wd
        return wg, wu, wd

    _ffn_accumulate(x_ref, g_ref, weights, o_ref, h_ref)


def _ffn_rest_kernel(x_ref, g_ref, wg_ref, wu_ref, wd_ref, o_ref, h_ref):
    _ffn_accumulate(x_ref, g_ref, lambda: (wg_ref[...], wu_ref[...], wd_ref[...]), o_ref, h_ref)


def _ffn(x, gain, wg, wu, wd, *, in_place, tm=1024, tf_first=256, tf=512):
    t, d = x.shape
    f = wg.shape[1]
    alias = {0: 0} if in_place else {}

    def w_specs(tfx):
        return [pl.BlockSpec((d, tfx), lambda i, j: (0, j)),
                pl.BlockSpec((d, tfx), lambda i, j: (0, j)),
                pl.BlockSpec((tfx, d), lambda i, j: (j, 0))]

    y, wg_b, wu_b, wd_b = pl.pallas_call(
        _ffn_first_kernel,
        grid=(1, f // tf_first),
        in_specs=[pl.BlockSpec((tm, d), lambda i, j: (0, 0), pipeline_mode=pl.Buffered(1)),
                  pl.BlockSpec((1, d), lambda i, j: (0, 0))] + w_specs(tf_first),
        out_specs=[pl.BlockSpec((tm, d), lambda i, j: (0, 0))] + w_specs(tf_first),
        out_shape=[jax.ShapeDtypeStruct((t if in_place else tm, d), F32), jax.ShapeDtypeStruct((d, f), BF16),
                   jax.ShapeDtypeStruct((d, f), BF16), jax.ShapeDtypeStruct((f, d), BF16)],
        scratch_shapes=[pltpu.VMEM((tm, d), BF16)],
        input_output_aliases=alias,
        compiler_params=_cparams("ffn_first", ("arbitrary", "arbitrary")),
        name="ffn_first",
    )(x, gain, wg, wu, wd)

    out_row0 = 1 if in_place else 0
    rest = pl.pallas_call(
        _ffn_rest_kernel,
        grid=(t // tm - 1, f // tf),
        in_specs=[pl.BlockSpec((tm, d), lambda i, j: (i + 1, 0)),
                  pl.BlockSpec((1, d), lambda i, j: (0, 0))] + w_specs(tf),
        out_specs=pl.BlockSpec((tm, d), lambda i, j: (i + out_row0, 0)),
        out_shape=jax.ShapeDtypeStruct((t if in_place else t - tm, d), F32),
        scratch_shapes=[pltpu.VMEM((tm, d), BF16)],
        input_output_aliases=alias,
        compiler_params=_cparams("ffn_rest", ("parallel", "arbitrary")),
        name="ffn_rest",
    )(y if in_place else x, gain, wg_b, wu_b, wd_b)
    return rest if in_place else (y, rest)


_N_CMP = 0
_N_KS = _N_CMP + 2 * NSA_KV_WIDTH
_N_KW = _N_KS + NSA_KV_WIDTH
_N_CQ = _N_KW + NSA_KV_WIDTH
_N_CKV = _N_CQ + MLA_Q_RANK
_N_KR = _N_CKV + MLA_KV_RANK
_N_END = _N_KR + 2 * LANES
_T_Q = 0
_T_VS = _T_Q + NSA_WIDTH
_T_VW = _T_VS + NSA_KV_WIDTH
_T_GATE = _T_VW + NSA_KV_WIDTH
_T_END = _T_GATE + LANES


def _two_part_specs(parts, tm):
    first, rest = parts
    nb_first = first.shape[0] // tm
    d = first.shape[1]
    return nb_first, [pl.BlockSpec((tm, d), lambda i: (jnp.minimum(i, nb_first - 1), 0)),
                      pl.BlockSpec((tm, d), lambda i: (jnp.maximum(i - nb_first, 0), 0))]


def _two_part_rows(first_ref, rest_ref, nb_first):
    return jnp.where(pl.program_id(0) < nb_first, first_ref[...], rest_ref[...])


def _mla_project(cq, ckv, k_rope, k_rot, tables, weights, gains, q_t_ref, k_ref, v_t_ref):
    cos, sin, cos_t, sin_t = (r[...] for r in tables)
    wq_t_ref, wqr_t_ref, wk_ref, wv_t_ref = weights
    qgn, qgr, qgt, kgn, kgr, kgt = (r[...] for r in gains)
    tq = q_t_ref.shape[-1]
    k_rope_ssq = jnp.sum(k_rope * k_rope, axis=-1, keepdims=True)
    k_rope_emb = k_rope * kgr * cos + k_rot * kgt * sin
    q_cos = qgr * cos_t
    q_sin = qgt * sin_t
    qscale = MLA_QK_DIM ** -0.5
    hg = MLA_HEAD_GROUP
    for hd in range(MLA_HEADS):
        if hd % hg == 0:
            qx_all = _dot_nt(wq_t_ref[hd * MLA_QK_PAD:(hd + hg) * MLA_QK_PAD, :], cq)
            q_rot_all = _dot_nt(wqr_t_ref[hd * LANES:(hd + hg) * LANES, :], cq)
            k_nope_all = _dot(ckv, wk_ref[:, hd * LANES:(hd + hg) * LANES])
            v_all = _dot_nt(wv_t_ref[hd * LANES:(hd + hg) * LANES, :], ckv)
        lo = hd % hg
        q_nope = qx_all[lo * MLA_QK_PAD:lo * MLA_QK_PAD + LANES]
        q_rope = qx_all[lo * MLA_QK_PAD + LANES:(lo + 1) * MLA_QK_PAD]
        q_rot = q_rot_all[lo * LANES:(lo + 1) * LANES]
        ssq = jnp.sum(q_nope * q_nope, axis=0, keepdims=True) + jnp.sum(q_rope * q_rope, axis=0, keepdims=True)
        inv = lax.rsqrt(ssq / MLA_QK_DIM + EPS) * qscale
        qn_nope = (q_nope * inv * qgn).astype(BF16)
        qn_rope = ((q_rope * q_cos + q_rot * q_sin) * inv).astype(BF16)
        for c in range(q_t_ref.shape[1]):
            q_t_ref[0, c, hd, 0:LANES, :] = qn_nope[:, c * tq:(c + 1) * tq]
            q_t_ref[0, c, hd, LANES:2 * LANES, :] = qn_rope[:, c * tq:(c + 1) * tq]

        k_nope = k_nope_all[:, lo * LANES:(lo + 1) * LANES]
        ssq = jnp.sum(k_nope * k_nope, axis=-1, keepdims=True) + k_rope_ssq
        inv = lax.rsqrt(ssq / MLA_QK_DIM + EPS)
        k_ref[0, hd, :, 0:LANES] = (k_nope * inv * kgn).astype(BF16)
        k_ref[0, hd, :, LANES:2 * LANES] = (k_rope_emb * inv).astype(BF16)
        v_t_ref[0, hd] = v_all[lo * LANES:(lo + 1) * LANES].astype(BF16)


def _proj_kernel(xa_ref, xb_ref, g_ref, wn_ref, wt_ref, qg_ref, kg_ref, qag_ref, kvag_ref, *rest, nb_first):
    mla_tables, mla_weights, mla_gains = rest[0:4], rest[4:8], rest[8:14]
    q_t_ref, cmp_ref, ks_ref, vs_t_ref, kw_ref, vw_t_ref, gate_t_ref, qm_t_ref, km_ref, vm_t_ref = rest[14:]
    tq = q_t_ref.shape[2]
    h = _rms(_two_part_rows(xa_ref, xb_ref, nb_first), g_ref[...]).astype(BF16)
    n_cuts = (_N_CMP, _N_CQ, _N_END)
    t_cuts = (_T_Q, _T_VS, _T_END)
    p_n = [_dot(h, wn_ref[:, a:b]) for a, b in zip(n_cuts[:-1], n_cuts[1:])]
    p_t = [_dot_nt(wt_ref[a:b, :], h) for a, b in zip(t_cuts[:-1], t_cuts[1:])]

    def proj(lo, hi):
        c = max(i for i, a in enumerate(n_cuts[:-1]) if a <= lo)
        return p_n[c][:, lo - n_cuts[c]:hi - n_cuts[c]]

    def proj_t(lo, hi):
        c = max(i for i, a in enumerate(t_cuts[:-1]) if a <= lo)
        return p_t[c][lo - t_cuts[c]:hi - t_cuts[c], :]

    qscale = HEAD_DIM ** -0.5
    for hd in range(NSA_HEADS):
        q = proj_t(_T_Q + hd * HEAD_DIM, _T_Q + (hd + 1) * HEAD_DIM)
        inv = lax.rsqrt(jnp.mean(q * q, axis=0, keepdims=True) + EPS) * qscale
        qn = (q * inv * qg_ref[...]).astype(BF16)
        for c in range(q_t_ref.shape[0]):
            q_t_ref[c, hd * HEAD_DIM:(hd + 1) * HEAD_DIM, :] = qn[:, c * tq:(c + 1) * tq]
    for c in range(cmp_ref.shape[0]):
        cmp_ref[c] = proj(_N_CMP + c * HEAD_DIM, _N_CMP + (c + 1) * HEAD_DIM)
    for g in range(NSA_KV_HEADS):
        sl = slice(g * HEAD_DIM, (g + 1) * HEAD_DIM)
        ks_ref[:, sl] = _rms(proj(_N_KS + g * HEAD_DIM, _N_KS + (g + 1) * HEAD_DIM), kg_ref[1:2, :]).astype(BF16)
        kw_ref[:, sl] = _rms(proj(_N_KW + g * HEAD_DIM, _N_KW + (g + 1) * HEAD_DIM), kg_ref[2:3, :]).astype(BF16)
    vs_t_ref[...] = proj_t(_T_VS, _T_VW).astype(BF16)
    vw_t_ref[...] = proj_t(_T_VW, _T_GATE).astype(BF16)
    gates = _sigmoid(proj_t(_T_GATE, _T_END))
    for c in range(gate_t_ref.shape[0]):
        gate_t_ref[c] = gates[:, c * tq:(c + 1) * tq]
    cq = _rms(proj(_N_CQ, _N_CKV), qag_ref[...]).astype(BF16)
    ckv = _rms(proj(_N_CKV, _N_KR), kvag_ref[...]).astype(BF16)
    _mla_project(cq, ckv, proj(_N_KR, _N_KR + LANES), proj(_N_KR + LANES, _N_END),
                 mla_tables, mla_weights, mla_gains, qm_t_ref, km_ref, vm_t_ref)


def _proj(x_parts, seq, gain, w_n, w_t, q_gain, k_gains, qa_gain, kva_gain, mla_tables, mla_weights, mla_q_gains,
          mla_k_gains, *, tm=512, tq=NSA_Q_TILE, tq_mla=MLA_Q_TILE):
    t = x_parts[0].shape[0] + x_parts[1].shape[0]
    nb_first, x_specs = _two_part_specs(x_parts, tm)
    q_gain = jnp.broadcast_to(q_gain.reshape(HEAD_DIM, 1), (HEAD_DIM, tm))
    mla_q_gains = tuple(jnp.broadcast_to(g.reshape(LANES, 1), (LANES, tm)) for g in mla_q_gains)
    cos, sin = mla_tables
    nj = seq // tm
    pos_rows = pl.BlockSpec((tm, LANES), lambda i: (i % nj, 0))
    pos_cols = pl.BlockSpec((LANES, tm), lambda i: (0, i % nj))

    def full(a):
        return pl.BlockSpec(a.shape, lambda i: (0,) * a.ndim)

    def rows(width):
        return (t, width), pl.BlockSpec((tm, width), lambda i: (i, 0))

    def cols(height):
        return (height, t), pl.BlockSpec((height, tm), lambda i: (0, i))

    def q_tiles(height):
        return (t // tq, height, tq), pl.BlockSpec((tm // tq, height, tq), lambda i: (i, 0, 0))

    n_cmp_cols = 2 * NSA_KV_WIDTH // HEAD_DIM
    cmp_out = (n_cmp_cols, t, HEAD_DIM), pl.BlockSpec((n_cmp_cols, tm, HEAD_DIM), lambda i: (0, i, 0))
    b = t // seq
    mla_q = ((b, seq // tq_mla, MLA_HEADS, MLA_QK_PAD, tq_mla),
             pl.BlockSpec((1, tm // tq_mla, MLA_HEADS, MLA_QK_PAD, tq_mla), lambda i: (i // nj, i % nj, 0, 0, 0)))
    mla_k = ((b, MLA_HEADS, seq, MLA_QK_PAD),
             pl.BlockSpec((1, MLA_HEADS, tm, MLA_QK_PAD), lambda i: (i // nj, 0, i % nj, 0)))
    mla_v = ((b, MLA_HEADS, MLA_V_DIM, seq),
             pl.BlockSpec((1, MLA_HEADS, MLA_V_DIM, tm), lambda i: (i // nj, 0, 0, i % nj)))
    outs = [(q_tiles(NSA_WIDTH), BF16), (cmp_out, F32), (rows(NSA_KV_WIDTH), BF16),
            (cols(NSA_KV_WIDTH), BF16), (rows(NSA_KV_WIDTH), BF16), (cols(NSA_KV_WIDTH), BF16),
            (q_tiles(LANES), F32), (mla_q, BF16), (mla_k, BF16), (mla_v, BF16)]
    mla_in = (cos, sin, cos.T, sin.T) + tuple(mla_weights) + mla_q_gains + tuple(mla_k_gains)
    mla_specs = [pos_rows, pos_rows, pos_cols, pos_cols] + [full(a) for a in mla_in[4:]]
    return pl.pallas_call(
        functools.partial(_proj_kernel, nb_first=nb_first),
        grid=(t // tm,),
        in_specs=x_specs + [full(gain), full(w_n), full(w_t), full(q_gain), full(k_gains), full(qa_gain),
                            full(kva_gain)] + mla_specs,
        out_specs=[spec for (_, spec), _ in outs],
        out_shape=[jax.ShapeDtypeStruct(shape, dt) for (shape, _), dt in outs],
        compiler_params=_cparams("proj", ("parallel",)),
        name="proj",
    )(*x_parts, gain, w_n, w_t, q_gain, k_gains, qa_gain, kva_gain, *mla_in)


def _gelu_tanh(x):
    return 0.5 * x * (1.0 + jnp.tanh(np.sqrt(2.0 / np.pi) * (x + 0.044715 * (x * x * x))))


def _compress_kernel(x_ref, w1k_ref, w2k_ref, pk_ref, w1v_ref, w2v_ref, pv_ref, kg_ref, kc_ref, vc_t_ref):
    half = CMP_STRIDE * HEAD_DIM
    nrow = x_ref.shape[1] // CMP_STRIDE
    for which, (w1_ref, w2_ref, p_ref) in enumerate(((w1k_ref, w2k_ref, pk_ref), (w1v_ref, w2v_ref, pv_ref))):
        w1 = w1_ref[...].astype(BF16)
        w2 = w2_ref[...].astype(BF16)
        pos_bias = _dot(jnp.broadcast_to(p_ref[...], (8, 2 * half)).astype(BF16), w1)[0:1, :]
        for g in range(NSA_KV_HEADS):
            col = which * NSA_KV_HEADS + g
            xg = jnp.concatenate(
                [x_ref[col, pl.ds(l, nrow, stride=CMP_STRIDE), :].astype(BF16) for l in range(CMP_STRIDE)], axis=1)
            first = _dot(xg, w1[0:half, :])
            second = _dot(xg, w1[half:2 * half, :])
            pre = first + pltpu.roll(second, nrow - 1, 0) + pos_bias
            hidden = _gelu_tanh(pre).astype(BF16)
            if which == 0:
                kc_ref[0, g] = _rms(_dot(hidden, w2), kg_ref[0:1, :]).astype(BF16)
            else:
                vc_t_ref[0, g] = _dot_nt(w2, hidden).astype(BF16)


def _compress(xc, seq, w1k, w2k, pk, w1v, w2v, pv, k_gains):
    ncol, t, width = xc.shape
    b, nrow = t // seq, seq // CMP_STRIDE

    def full(a):
        return pl.BlockSpec(a.shape, lambda i: (0,) * a.ndim)

    out_sds = jax.ShapeDtypeStruct((b, NSA_KV_HEADS, nrow, HEAD_DIM), BF16)
    out_spec = pl.BlockSpec((1, NSA_KV_HEADS, nrow, HEAD_DIM), lambda i: (i, 0, 0, 0))
    return pl.pallas_call(
        _compress_kernel,
        grid=(b,),
        in_specs=[pl.BlockSpec((ncol, seq, width), lambda i: (0, i, 0)),
                  full(w1k), full(w2k), full(pk), full(w1v), full(w2v), full(pv), full(k_gains)],
        out_specs=[out_spec, out_spec],
        out_shape=[out_sds, out_sds],
        compiler_params=_cparams("compress", ("parallel",)),
        name="compress",
    )(xc, w1k, w2k, pk, w1v, w2v, pv, k_gains)


def _softmax_cols(s):
    p = jnp.exp(s - jnp.max(s, axis=0, keepdims=True))
    return jnp.sum(p, axis=0, keepdims=True), p


def _nsa_kernel(q_t_ref, kc_ref, vc_t_ref, ks_ref, vs_t_ref, kw_ref, vw_t_ref, gate_t_ref, kfeat_ref, cfeat_ref,
                slope_ref, o_ref, *, tq, ck):
    qi = pl.program_id(1)
    q0 = qi * tq
    ng = NSA_KV_HEADS
    wide = NSA_GROUP * tq
    n_sel = ks_ref.shape[1] // SEL_BLOCK
    ncmp = kc_ref.shape[2]
    n_win = WINDOW + tq

    def tile4(a):
        return jnp.concatenate([a] * NSA_GROUP, axis=1)

    def masked(s, mask, fill=NEG):
        return jnp.concatenate([jnp.where(mask, s[:, r * tq:(r + 1) * tq], fill) for r in range(NSA_GROUP)], axis=1)

    def q_aug(g, extra):
        q = jnp.concatenate([q_t_ref[0, (g * NSA_GROUP + r) * HEAD_DIM:(g * NSA_GROUP + r + 1) * HEAD_DIM, :]
                             for r in range(NSA_GROUP)], axis=1)
        rows = [q, slope_ref[g]]
        used = slope_ref.shape[1]
        if extra is not None:
            rows.append(extra)
            used += extra.shape[0]
        rows.append(jnp.zeros((HEAD_DIM - used, wide), BF16))
        return jnp.concatenate(rows, axis=0)

    def gsl(g):
        return slice(g * HEAD_DIM, (g + 1) * HEAD_DIM)

    lane_q = q0 + lax.broadcasted_iota(jnp.int32, (1, tq), 1)

    qa_plain = [q_aug(g, None) for g in range(ng)]
    w0 = pl.multiple_of(jnp.maximum(q0 - WINDOW, 0), tq)
    s_cmp = [_dot(jnp.concatenate([kc_ref[0, g], cfeat_ref[...]], axis=1), qa_plain[g]) for g in range(ng)]
    s_win = [_dot(jnp.concatenate([kw_ref[0, pl.ds(w0, n_win), gsl(g)], kfeat_ref[pl.ds(w0, n_win), :]], axis=1),
                  qa_plain[g]) for g in range(ng)]

    n_row = lax.broadcasted_iota(jnp.int32, (ncmp, tq), 0)
    mask_c = n_row * CMP_STRIDE + (CMP_BLOCK - 1) <= lane_q
    s_row = lax.broadcasted_iota(jnp.int32, (n_sel, ncmp), 0)
    n_col = lax.broadcasted_iota(jnp.int32, (n_sel, ncmp), 1)
    overlap_t = jnp.where((n_col * CMP_STRIDE < (s_row + 1) * SEL_BLOCK)
                          & (n_col * CMP_STRIDE + (CMP_BLOCK - 1) >= s_row * SEL_BLOCK), 1.0, 0.0).astype(BF16)
    p_cmp, imp = [], []
    for g in range(ng):
        s = masked(s_cmp[g], mask_c)
        m = jnp.max(s, axis=0, keepdims=True)
        p = masked(jnp.exp(s - m), mask_c, 0.0)
        l = jnp.sum(p, axis=0, keepdims=True)
        p = p / jnp.where(l > 0.0, l, 1.0)
        p_cmp.append(p.astype(BF16))
        p_sum = p[:, 0:tq]
        for r in range(1, NSA_GROUP):
            p_sum = p_sum + p[:, r * tq:(r + 1) * tq]
        p_hi = p_sum.astype(BF16)
        p_lo = (p_sum - p_hi.astype(F32)).astype(BF16)
        imp.append(_dot(overlap_t, p_hi) + _dot(overlap_t, p_lo))
    o_cmp = [_dot(vc_t_ref[0, g], p_cmp[g]) for g in range(ng)]

    blk = lax.broadcasted_iota(jnp.int32, (n_sel, tq), 0)
    blk_f = blk.astype(F32)
    forced = (blk == 0) | (blk == jnp.right_shift(lane_q, SEL_SHIFT))
    future = blk * SEL_BLOCK > lane_q
    penalty = []
    for g in range(ng):
        v = jnp.where(future, NEG, jnp.where(forced, FORCE_SCORE, imp[g]))
        unselected = jnp.full((n_sel, tq), NEG, F32)
        for _ in range(min(SEL_TOP, n_sel)):
            mx = jnp.max(v, axis=0, keepdims=True)
            first = jnp.min(jnp.where(v == mx, blk_f, float(n_sel)), axis=0, keepdims=True)
            pick = blk_f == first
            unselected = jnp.where(pick, 0.0, unselected)
            v = jnp.where(pick, -jnp.inf, v)
        penalty.append(tile4(unselected.astype(BF16)))

    rel = lax.broadcasted_iota(jnp.int32, (n_win, tq), 0) - lax.broadcasted_iota(jnp.int32, (n_win, tq), 1)
    off = q0 - w0
    mask_w = (rel <= off) & (rel > off - WINDOW)
    o_win = []
    p_win = []
    for g in range(ng):
        l, p = _softmax_cols(masked(s_win[g], mask_w))
        p_win.append((l, p.astype(BF16)))
    for g in range(ng):
        l, p = p_win[g]
        o_win.append(_dot(vw_t_ref[gsl(g), pl.ds(w0, n_win)], p) / l)

    partial = []
    for g in range(ng):
        for r in range(NSA_GROUP):
            hd = g * NSA_GROUP + r
            cs = slice(r * tq, (r + 1) * tq)
            partial.append(gate_t_ref[0, 3 * hd:3 * hd + 1, :] * o_cmp[g][:, cs]
                           + gate_t_ref[0, 3 * hd + 2:3 * hd + 3, :] * o_win[g][:, cs])

    qa_sel = [q_aug(g, penalty[g]) for g in range(ng)]
    rel_d = lax.broadcasted_iota(jnp.int32, (ck, tq), 0) - lax.broadcasted_iota(jnp.int32, (ck, tq), 1)

    def select_and_store(n_before):
        nk = n_before + ck
        mask_d = rel_d <= q0 - n_before
        ss = [_dot(jnp.concatenate([ks_ref[0, 0:nk, gsl(g)], kfeat_ref[0:nk, :]], axis=1), qa_sel[g])
              for g in range(ng)]
        for g in range(ng):
            s_diag = masked(ss[g][n_before:nk], mask_d)
            m = jnp.max(s_diag, axis=0, keepdims=True)
            if n_before:
                m = jnp.maximum(m, jnp.max(ss[g][0:n_before], axis=0, keepdims=True))
            p_diag = jnp.exp(s_diag - m)
            l = jnp.sum(p_diag, axis=0, keepdims=True)
            p = p_diag.astype(BF16)
            if n_before:
                p_before = jnp.exp(ss[g][0:n_before] - m)
                l = l + jnp.sum(p_before, axis=0, keepdims=True)
                p = jnp.concatenate([p_before.astype(BF16), p], axis=0)
            o_sel = _dot(vs_t_ref[gsl(g), 0:nk], p) / l
            for r in range(NSA_GROUP):
                hd = g * NSA_GROUP + r
                mixed = partial[hd] + gate_t_ref[0, 3 * hd + 1:3 * hd + 2, :] * o_sel[:, r * tq:(r + 1) * tq]
                o_ref[0, :, hd * HEAD_DIM:(hd + 1) * HEAD_DIM] = mixed.T

    per = ck // tq
    for v in range(ks_ref.shape[1] // ck):
        pl.when(qi // per == v)(functools.partial(select_and_store, v * ck))


def _nsa(q_t, kc, vc_t, ks, vs_t, kw, vw_t, gate_t, *, tq=NSA_Q_TILE, ck=256):
    b, s, _ = ks.shape
    nq = s // tq
    pos = np.arange(s)
    kfeat = np.zeros((s, LANES), np.float32)
    kfeat[:, 0] = POS_SPLIT * (pos // POS_SPLIT)
    kfeat[:, 1] = pos % POS_SPLIT
    kfeat[pos, FEAT_ROWS + pos // SEL_BLOCK] = 1.0
    ncmp = kc.shape[2]
    cfeat = np.zeros((ncmp, LANES), np.float32)
    cfeat[:, 0] = CMP_STRIDE * np.arange(ncmp)
    cfeat[:, 1] = 0.5 * (CMP_BLOCK - 1)
    slope = np.zeros((NSA_KV_HEADS, FEAT_ROWS, NSA_GROUP * tq), np.float32)
    for hd in range(NSA_HEADS):
        g, r = divmod(hd, NSA_GROUP)
        slope[g, 0:2, r * tq:(r + 1) * tq] = 2.0 ** (-8.0 * (hd + 1) / NSA_HEADS)
    assert FEAT_ROWS + s // SEL_BLOCK <= LANES and s // SEL_BLOCK % FEAT_ROWS == 0 and s % ck == 0 and ck % tq == 0

    def full(a):
        return pl.BlockSpec(a.shape, lambda i, j: (0,) * a.ndim)

    def per_b(a):
        return pl.BlockSpec((1,) + a.shape[1:], lambda i, j: (i,) + (0,) * (a.ndim - 1))

    def per_b_cols(a):
        return pl.BlockSpec((a.shape[0], s), lambda i, j: (0, i))

    def q_cols(a):
        assert a.shape[0] == b * nq and a.shape[2] == tq
        return pl.BlockSpec((1, a.shape[1], tq), lambda i, j: (i * nq + j, 0, 0))

    consts = [jnp.asarray(kfeat, BF16), jnp.asarray(cfeat, BF16), jnp.asarray(slope, BF16)]
    return pl.pallas_call(
        functools.partial(_nsa_kernel, tq=tq, ck=ck),
        grid=(b, nq),
        in_specs=[q_cols(q_t), per_b(kc), per_b(vc_t), per_b(ks), per_b_cols(vs_t), per_b(kw), per_b_cols(vw_t),
                  q_cols(gate_t)] + [full(c) for c in consts],
        out_specs=pl.BlockSpec((1, tq, NSA_WIDTH), lambda i, j: (i, j, 0)),
        out_shape=jax.ShapeDtypeStruct((b, s, NSA_WIDTH), F32),
        compiler_params=_cparams("nsa", ("parallel", "arbitrary")),
        name="nsa",
    )(q_t, kc, vc_t, ks, vs_t, kw, vw_t, gate_t, *consts)


def _mla_attn_kernel(q_t_ref, k_ref, v_t_ref, o_ref, *, tq, hb, n_q):
    qi = pl.program_id(2)
    d_v = v_t_ref.shape[2]
    tri = lax.broadcasted_iota(jnp.int32, (tq, tq), 0) <= lax.broadcasted_iota(jnp.int32, (tq, tq), 1)

    def attend(n_before):
        nk = n_before + tq
        ss = [_dot(k_ref[0, h, 0:nk, :], q_t_ref[0, 0, h]) for h in range(hb)]
        for h in range(hb):
            s_diag = jnp.where(tri, ss[h][n_before:nk], NEG)
            m = jnp.max(s_diag, axis=0, keepdims=True)
            if n_before:
                m = jnp.maximum(m, jnp.max(ss[h][0:n_before], axis=0, keepdims=True))
            p_diag = jnp.exp(s_diag - m)
            l = jnp.sum(p_diag, axis=0, keepdims=True)
            p = p_diag.astype(BF16)
            if n_before:
                p_before = jnp.exp(ss[h][0:n_before] - m)
                l = l + jnp.sum(p_before, axis=0, keepdims=True)
                p = jnp.concatenate([p_before.astype(BF16), p], axis=0)
            o_ref[0, :, h * d_v:(h + 1) * d_v] = (_dot(v_t_ref[0, h, :, 0:nk], p) / l).T

    for v in range(n_q):
        pl.when(qi == v)(functools.partial(attend, v * tq))


def _mla_attn(q_t, k, v_t, *, tq=MLA_Q_TILE, hb=MLA_HEADS):
    b, h, s, dq = k.shape
    dv = v_t.shape[2]
    assert q_t.shape == (b, s // tq, h, dq, tq)
    return pl.pallas_call(
        functools.partial(_mla_attn_kernel, tq=tq, hb=hb, n_q=s // tq),
        grid=(b, h // hb, s // tq),
        in_specs=[pl.BlockSpec((1, 1, hb, dq, tq), lambda i, j, l: (i, l, j, 0, 0)),
                  pl.BlockSpec((1, hb, s, dq), lambda i, j, l: (i, j, 0, 0)),
                  pl.BlockSpec((1, hb, dv, s), lambda i, j, l: (i, j, 0, 0))],
        out_specs=pl.BlockSpec((1, tq, hb * dv), lambda i, j, l: (i, l, j)),
        out_shape=jax.ShapeDtypeStruct((b, s, h * dv), F32),
        compiler_params=_cparams("mla_attn", ("parallel", "parallel", "arbitrary")),
        name="mla_attn",
    )(q_t, k, v_t)


def _out_kernel(xa_ref, xb_ref, oa_ref, ob_ref, ga_ref, gb_ref, w_ref, o_ref, *, nb_first):
    oa = _rms(oa_ref[...], ga_ref[...]).astype(BF16)
    ob = _rms(ob_ref[...], gb_ref[...]).astype(BF16)
    o_ref[...] = (_two_part_rows(xa_ref, xb_ref, nb_first)
                  + _dot(jnp.concatenate([oa, ob], axis=1), w_ref[...]))


def _out_proj(x_parts, oa, ob, gain_a, gain_b, w, *, tm=512):
    t, d = oa.shape[0], x_parts[0].shape[1]
    nb_first, x_specs = _two_part_specs(x_parts, tm)

    def rows(width):
        return pl.BlockSpec((tm, width), lambda i: (i, 0))

    def full(a):
        return pl.BlockSpec(a.shape, lambda i: (0,) * a.ndim)

    return pl.pallas_call(
        functools.partial(_out_kernel, nb_first=nb_first),
        grid=(t // tm,),
        in_specs=x_specs + [rows(oa.shape[1]), rows(ob.shape[1]), full(gain_a), full(gain_b), full(w)],
        out_specs=rows(d),
        out_shape=jax.ShapeDtypeStruct((t, d), F32),
        compiler_params=_cparams("out_proj", ("parallel",)),
        name="out_proj",
    )(*x_parts, oa, ob, gain_a, gain_b, w)


def _rot_cols(w):
    half = w.shape[1] // 2
    return jnp.concatenate([-w[:, half:], w[:, :half]], axis=1)


def _layout_w_in(w_in):
    d = w_in.shape[0]
    off = np.cumsum([0, NSA_WIDTH] + [NSA_KV_WIDTH] * 6 + [NSA_HEADS * 3, MLA_Q_RANK, MLA_KV_RANK, MLA_ROPE_DIM])
    q, kc, vc, ks, vs, kw, vw, gt, cq, ckv, kr = [w_in[:, off[i]:off[i + 1]] for i in range(11)]
    zr = jnp.zeros((d, LANES - MLA_ROPE_DIM), w_in.dtype)
    zg = jnp.zeros((d, LANES - NSA_HEADS * 3), w_in.dtype)
    w_n = jnp.concatenate([kc, vc, ks, kw, cq, ckv, kr, zr, _rot_cols(kr), zr], axis=1).astype(BF16)
    w_t = jnp.concatenate([q, vs, vw, gt, zg], axis=1).astype(BF16).T
    return w_n, w_t


def _layout_mla_weights(w_uq, w_ukv):
    r = w_uq.shape[0]
    zr = jnp.zeros((r, LANES - MLA_ROPE_DIM), w_uq.dtype)
    main, rot, k_cols, v_cols = [], [], [], []
    for hd in range(MLA_HEADS):
        base = hd * MLA_QK_DIM
        rope = w_uq[:, base + MLA_NOPE_DIM:base + MLA_QK_DIM]
        main += [w_uq[:, base:base + MLA_NOPE_DIM], rope, zr]
        rot += [_rot_cols(rope), zr]
        kv = hd * (MLA_NOPE_DIM + MLA_V_DIM)
        k_cols.append(w_ukv[:, kv:kv + MLA_NOPE_DIM])
        v_cols.append(w_ukv[:, kv + MLA_NOPE_DIM:kv + MLA_NOPE_DIM + MLA_V_DIM])
    cat = lambda parts: jnp.concatenate(parts, axis=1).astype(BF16)
    return cat(main).T, cat(rot).T, cat(k_cols), cat(v_cols).T


def _rope_gains(gain):
    half = MLA_ROPE_DIM // 2
    zr = jnp.zeros((LANES - MLA_ROPE_DIM,), gain.dtype)
    g_rope = gain[MLA_NOPE_DIM:]
    return (gain[None, :MLA_NOPE_DIM], jnp.concatenate([g_rope, zr])[None, :],
            jnp.concatenate([g_rope[half:], g_rope[:half], zr])[None, :])


def _rope_tables(s):
    half = MLA_ROPE_DIM // 2
    inv = ROPE_THETA ** (-jnp.arange(half, dtype=F32) / half)
    ang = jnp.arange(s).astype(F32)[:, None] * inv[None, :]
    zr = jnp.zeros((s, LANES - MLA_ROPE_DIM), F32)
    cos, sin = jnp.cos(ang), jnp.sin(ang)
    return jnp.concatenate([cos, cos, zr], axis=1), jnp.concatenate([sin, sin, zr], axis=1)


def kernel(x, ffn1_norm, ffn1_w_gate, ffn1_w_up, ffn1_w_down, mix_norm, w_in, nsa_q_norm, nsa_k_norm, nsa_cmp_pos_k, nsa_cmp_w1_k, nsa_cmp_w2_k, nsa_cmp_pos_v, nsa_cmp_w1_v, nsa_cmp_w2_v, mla_q_a_norm, mla_w_uq, mla_kv_a_norm, mla_w_ukv, mla_q_norm, mla_k_norm, out_norm_nsa, out_norm_mla, w_out, ffn2_norm, ffn2_w_gate, ffn2_w_up, ffn2_w_down):
    b, s, d = x.shape
    depth = ffn1_norm.shape[0]
    tables = _rope_tables(s)
    xt = x.reshape(b * s, d)
    for l in range(depth):
        x_parts = _ffn(xt, ffn1_norm[l][None, :], ffn1_w_gate[l], ffn1_w_up[l], ffn1_w_down[l], in_place=False)

        w_n, w_t = _layout_w_in(w_in[l])
        q_t, kv_cmp, ks, vs_t, kw, vw_t, gate_t, q_m, k_m, v_m = _proj(
            x_parts, s, mix_norm[l][None, :], w_n, w_t, nsa_q_norm[l], nsa_k_norm[l],
            mla_q_a_norm[l][None, :], mla_kv_a_norm[l][None, :],
            tables, _layout_mla_weights(mla_w_uq[l], mla_w_ukv[l]),
            _rope_gains(mla_q_norm[l]), _rope_gains(mla_k_norm[l]))

        o_b = _mla_attn(q_m, k_m, v_m)

        kc, vc_t = _compress(
            kv_cmp, s,
            nsa_cmp_w1_k[l], nsa_cmp_w2_k[l], nsa_cmp_pos_k[l].reshape(1, -1),
            nsa_cmp_w1_v[l], nsa_cmp_w2_v[l].T, nsa_cmp_pos_v[l].reshape(1, -1),
            nsa_k_norm[l])

        def bs(a):
            return a.reshape(b, s, a.shape[-1])

        o_a = _nsa(q_t, kc, vc_t, bs(ks), vs_t, bs(kw), vw_t, gate_t)

        xt = _out_proj(x_parts, o_a.reshape(b * s, NSA_WIDTH), o_b.reshape(b * s, MLA_WIDTH),
                       out_norm_nsa[l][None, :], out_norm_mla[l][None, :], w_out[l].astype(BF16))

        xt = _ffn(xt, ffn2_norm[l][None, :], ffn2_w_gate[l], ffn2_w_up[l], ffn2_w_down[l], in_place=True)
    return xt.reshape(b, s, d)
```

```python
import functools

import numpy as np
import jax
import jax.numpy as jnp
from jax import lax
from jax.experimental import pallas as pl
from jax.experimental.pallas import tpu as pltpu

F32 = jnp.float32
BF16 = jnp.bfloat16

HEAD_DIM = 128
NSA_HEADS = 8
NSA_KV_HEADS = 2
NSA_GROUP = NSA_HEADS // NSA_KV_HEADS
CMP_BLOCK = 32
CMP_STRIDE = 16
SEL_BLOCK = 64
SEL_SHIFT = 6
SEL_TOP = 8
WINDOW = 512
MLA_HEADS = 8
MLA_Q_RANK = 384
MLA_KV_RANK = 256
MLA_NOPE_DIM = 128
MLA_ROPE_DIM = 64
MLA_V_DIM = 128
MLA_QK_DIM = MLA_NOPE_DIM + MLA_ROPE_DIM
ROPE_THETA = 10000.0
EPS = 1e-6
NEG = -1e30
FORCE_SCORE = 1e4
NSA_WIDTH = NSA_HEADS * HEAD_DIM
MLA_WIDTH = MLA_HEADS * MLA_V_DIM
NSA_KV_WIDTH = NSA_KV_HEADS * HEAD_DIM

LANES = 128
MLA_QK_PAD = 2 * LANES
MLA_HEAD_GROUP = 2
VMEM_PHYSICAL_MIB = 64
VMEM_MIB = {"ffn_first": 44, "ffn_rest": 56, "proj": 58, "compress": 22, "nsa": 37, "mla_attn": 45, "out_proj": 45}
assert max(VMEM_MIB.values()) < VMEM_PHYSICAL_MIB
POS_SPLIT = 16
FEAT_ROWS = 16
NSA_Q_TILE = 256
MLA_Q_TILE = 256


def _cparams(name, sem):
    return pltpu.CompilerParams(dimension_semantics=sem, vmem_limit_bytes=VMEM_MIB[name] * 1024 * 1024)


def _rms(x, gain):
    return x * lax.rsqrt(jnp.mean(x * x, axis=-1, keepdims=True) + EPS) * gain


def _dot(a, b):
    return jnp.dot(a, b, preferred_element_type=F32)


def _dot_nt(a, b):
    return lax.dot_general(a, b, (((1,), (1,)), ((), ())), preferred_element_type=F32)


def _sigmoid(x):
    return 1.0 / (1.0 + jnp.exp(-x))


def _ffn_accumulate(x_ref, g_ref, weights, o_ref, h_ref):
    @pl.when(pl.program_id(1) == 0)
    def _():
        x = x_ref[...]
        h_ref[...] = _rms(x, g_ref[...]).astype(BF16)
        o_ref[...] = x

    wg, wu, wd = weights()
    h = h_ref[...]
    gate = _dot(h, wg)
    up = _dot(h, wu)
    act = gate * _sigmoid(gate) * up * 0.5
    o_ref[...] += _dot(act.astype(BF16), wd)


def _ffn_first_kernel(x_ref, g_ref, wg_ref, wu_ref, wd_ref, o_ref, wg_out, wu_out, wd_out, h_ref):
    def weights():
        wg, wu, wd = wg_ref[...].astype(BF16), wu_ref[...].astype(BF16), wd_ref[...].astype(BF16)
        wg_out[...], wu_out[...], wd_out[...] = wg, wu, wd
        return wg, wu, wd

    _ffn_accumulate(x_ref, g_ref, weights, o_ref, h_ref)


def _ffn_rest_kernel(x_ref, g_ref, wg_ref, wu_ref, wd_ref, o_ref, h_ref):
    _ffn_accumulate(x_ref, g_ref, lambda: (wg_ref[...], wu_ref[...], wd_ref[...]), o_ref, h_ref)


def _ffn(x, gain, wg, wu, wd, *, in_place, tm=1024, tf_first=256, tf=512):
    t, d = x.shape
    f = wg.shape[1]
    alias = {0: 0} if in_place else {}

    def w_specs(tfx):
        return [pl.BlockSpec((d, tfx), lambda i, j: (0, j)),
                pl.BlockSpec((d, tfx), lambda i, j: (0, j)),
                pl.BlockSpec((tfx, d), lambda i, j: (j, 0))]

    y, wg_b, wu_b, wd_b = pl.pallas_call(
        _ffn_first_kernel,
        grid=(1, f // tf_first),
        in_specs=[pl.BlockSpec((tm, d), lambda i, j: (0, 0), pipeline_mode=pl.Buffered(1)),
                  pl.BlockSpec((1, d), lambda i, j: (0, 0))] + w_specs(tf_first),
        out_specs=[pl.BlockSpec((tm, d), lambda i, j: (0, 0))] + w_specs(tf_first),
        out_shape=[jax.ShapeDtypeStruct((t if in_place else tm, d), F32), jax.ShapeDtypeStruct((d, f), BF16),
                   jax.ShapeDtypeStruct((d, f), BF16), jax.ShapeDtypeStruct((f, d), BF16)],
        scratch_shapes=[pltpu.VMEM((tm, d), BF16)],
        input_output_aliases=alias,
        compiler_params=_cparams("ffn_first", ("arbitrary", "arbitrary")),
        name="ffn_first",
    )(x, gain, wg, wu, wd)

    out_row0 = 1 if in_place else 0
    rest = pl.pallas_call(
        _ffn_rest_kernel,
        grid=(t // tm - 1, f // tf),
        in_specs=[pl.BlockSpec((tm, d), lambda i, j: (i + 1, 0)),
                  pl.BlockSpec((1, d), lambda i, j: (0, 0))] + w_specs(tf),
        out_specs=pl.BlockSpec((tm, d), lambda i, j: (i + out_row0, 0)),
        out_shape=jax.ShapeDtypeStruct((t if in_place else t - tm, d), F32),
        scratch_shapes=[pltpu.VMEM((tm, d), BF16)],
        input_output_aliases=alias,
        compiler_params=_cparams("ffn_rest", ("parallel", "arbitrary")),
        name="ffn_rest",
    )(y if in_place else x, gain, wg_b, wu_b, wd_b)
    return rest if in_place else (y, rest)


_N_CMP = 0
_N_KS = _N_CMP + 2 * NSA_KV_WIDTH
_N_KW = _N_KS + NSA_KV_WIDTH
_N_CQ = _N_KW + NSA_KV_WIDTH
_N_CKV = _N_CQ + MLA_Q_RANK
_N_KR = _N_CKV + MLA_KV_RANK
_N_END = _N_KR + 2 * LANES
_T_Q = 0
_T_VS = _T_Q + NSA_WIDTH
_T_VW = _T_VS + NSA_KV_WIDTH
_T_GATE = _T_VW + NSA_KV_WIDTH
_T_END = _T_GATE + LANES


def _two_part_specs(parts, tm):
    first, rest = parts
    nb_first = first.shape[0] // tm
    d = first.shape[1]
    return nb_first, [pl.BlockSpec((tm, d), lambda i: (jnp.minimum(i, nb_first - 1), 0)),
                      pl.BlockSpec((tm, d), lambda i: (jnp.maximum(i - nb_first, 0), 0))]


def _two_part_rows(first_ref, rest_ref, nb_first):
    return jnp.where(pl.program_id(0) < nb_first, first_ref[...], rest_ref[...])


def _mla_project(cq, ckv, k_rope, k_rot, tables, weights, gains, q_t_ref, k_ref, v_t_ref):
    cos, sin, cos_t, sin_t = (r[...] for r in tables)
    wq_t_ref, wqr_t_ref, wk_ref, wv_t_ref = weights
    qgn, qgr, qgt, kgn, kgr, kgt = (r[...] for r in gains)
    tq = q_t_ref.shape[-1]
    k_rope_ssq = jnp.sum(k_rope * k_rope, axis=-1, keepdims=True)
    k_rope_emb = k_rope * kgr * cos + k_rot * kgt * sin
    q_cos = qgr * cos_t
    q_sin = qgt * sin_t
    qscale = MLA_QK_DIM ** -0.5
    hg = MLA_HEAD_GROUP
    for hd in range(MLA_HEADS):
        if hd % hg == 0:
            qx_all = _dot_nt(wq_t_ref[hd * MLA_QK_PAD:(hd + hg) * MLA_QK_PAD, :], cq)
            q_rot_all = _dot_nt(wqr_t_ref[hd * LANES:(hd + hg) * LANES, :], cq)
            k_nope_all = _dot(ckv, wk_ref[:, hd * LANES:(hd + hg) * LANES])
            v_all = _dot_nt(wv_t_ref[hd * LANES:(hd + hg) * LANES, :], ckv)
        lo = hd % hg
        q_nope = qx_all[lo * MLA_QK_PAD:lo * MLA_QK_PAD + LANES]
        q_rope = qx_all[lo * MLA_QK_PAD + LANES:(lo + 1) * MLA_QK_PAD]
        q_rot = q_rot_all[lo * LANES:(lo + 1) * LANES]
        ssq = jnp.sum(q_nope * q_nope, axis=0, keepdims=True) + jnp.sum(q_rope * q_rope, axis=0, keepdims=True)
        inv = lax.rsqrt(ssq / MLA_QK_DIM + EPS) * qscale
        qn_nope = (q_nope * inv * qgn).astype(BF16)
        qn_rope = ((q_rope * q_cos + q_rot * q_sin) * inv).astype(BF16)
        for c in range(q_t_ref.shape[1]):
            q_t_ref[0, c, hd, 0:LANES, :] = qn_nope[:, c * tq:(c + 1) * tq]
            q_t_ref[0, c, hd, LANES:2 * LANES, :] = qn_rope[:, c * tq:(c + 1) * tq]

        k_nope = k_nope_all[:, lo * LANES:(lo + 1) * LANES]
        ssq = jnp.sum(k_nope * k_nope, axis=-1, keepdims=True) + k_rope_ssq
        inv = lax.rsqrt(ssq / MLA_QK_DIM + EPS)
        k_ref[0, hd, :, 0:LANES] = (k_nope * inv * kgn).astype(BF16)
        k_ref[0, hd, :, LANES:2 * LANES] = (k_rope_emb * inv).astype(BF16)
        v_t_ref[0, hd] = v_all[lo * LANES:(lo + 1) * LANES].astype(BF16)


def _proj_kernel(xa_ref, xb_ref, g_ref, wn_ref, wt_ref, qg_ref, kg_ref, qag_ref, kvag_ref, *rest, nb_first):
    mla_tables, mla_weights, mla_gains = rest[0:4], rest[4:8], rest[8:14]
    q_t_ref, cmp_ref, ks_ref, vs_t_ref, kw_ref, vw_t_ref, gate_t_ref, qm_t_ref, km_ref, vm_t_ref = rest[14:]
    tq = q_t_ref.shape[2]
    h = _rms(_two_part_rows(xa_ref, xb_ref, nb_first), g_ref[...]).astype(BF16)
    n_cuts = (_N_CMP, _N_CQ, _N_END)
    t_cuts = (_T_Q, _T_VS, _T_END)
    p_n = [_dot(h, wn_ref[:, a:b]) for a, b in zip(n_cuts[:-1], n_cuts[1:])]
    p_t = [_dot_nt(wt_ref[a:b, :], h) for a, b in zip(t_cuts[:-1], t_cuts[1:])]

    def proj(lo, hi):
        c = max(i for i, a in enumerate(n_cuts[:-1]) if a <= lo)
        return p_n[c][:, lo - n_cuts[c]:hi - n_cuts[c]]

    def proj_t(lo, hi):
        c = max(i for i, a in enumerate(t_cuts[:-1]) if a <= lo)
        return p_t[c][lo - t_cuts[c]:hi - t_cuts[c], :]

    qscale = HEAD_DIM ** -0.5
    for hd in range(NSA_HEADS):
        q = proj_t(_T_Q + hd * HEAD_DIM, _T_Q + (hd + 1) * HEAD_DIM)
        inv = lax.rsqrt(jnp.mean(q * q, axis=0, keepdims=True) + EPS) * qscale
        qn = (q * inv * qg_ref[...]).astype(BF16)
        for c in range(q_t_ref.shape[0]):
            q_t_ref[c, hd * HEAD_DIM:(hd + 1) * HEAD_DIM, :] = qn[:, c * tq:(c + 1) * tq]
    for c in range(cmp_ref.shape[0]):
        cmp_ref[c] = proj(_N_CMP + c * HEAD_DIM, _N_CMP + (c + 1) * HEAD_DIM)
    for g in range(NSA_KV_HEADS):
        sl = slice(g * HEAD_DIM, (g + 1) * HEAD_DIM)
        ks_ref[:, sl] = _rms(proj(_N_KS + g * HEAD_DIM, _N_KS + (g + 1) * HEAD_DIM), kg_ref[1:2, :]).astype(BF16)
        kw_ref[:, sl] = _rms(proj(_N_KW + g * HEAD_DIM, _N_KW + (g + 1) * HEAD_DIM), kg_ref[2:3, :]).astype(BF16)
    vs_t_ref[...] = proj_t(_T_VS, _T_VW).astype(BF16)
    vw_t_ref[...] = proj_t(_T_VW, _T_GATE).astype(BF16)
    gates = _sigmoid(proj_t(_T_GATE, _T_END))
    for c in range(gate_t_ref.shape[0]):
        gate_t_ref[c] = gates[:, c * tq:(c + 1) * tq]
    cq = _rms(proj(_N_CQ, _N_CKV), qag_ref[...]).astype(BF16)
    ckv = _rms(proj(_N_CKV, _N_KR), kvag_ref[...]).astype(BF16)
    _mla_project(cq, ckv, proj(_N_KR, _N_KR + LANES), proj(_N_KR + LANES, _N_END),
                 mla_tables, mla_weights, mla_gains, qm_t_ref, km_ref, vm_t_ref)


def _proj(x_parts, seq, gain, w_n, w_t, q_gain, k_gains, qa_gain, kva_gain, mla_tables, mla_weights, mla_q_gains,
          mla_k_gains, *, tm=512, tq=NSA_Q_TILE, tq_mla=MLA_Q_TILE):
    t = x_parts[0].shape[0] + x_parts[1].shape[0]
    nb_first, x_specs = _two_part_specs(x_parts, tm)
    q_gain = jnp.broadcast_to(q_gain.reshape(HEAD_DIM, 1), (HEAD_DIM, tm))
    mla_q_gains = tuple(jnp.broadcast_to(g.reshape(LANES, 1), (LANES, tm)) for g in mla_q_gains)
    cos, sin = mla_tables
    nj = seq // tm
    pos_rows = pl.BlockSpec((tm, LANES), lambda i: (i % nj, 0))
    pos_cols = pl.BlockSpec((LANES, tm), lambda i: (0, i % nj))

    def full(a):
        return pl.BlockSpec(a.shape, lambda i: (0,) * a.ndim)

    def rows(width):
        return (t, width), pl.BlockSpec((tm, width), lambda i: (i, 0))

    def cols(height):
        return (height, t), pl.BlockSpec((height, tm), lambda i: (0, i))

    def q_tiles(height):
        return (t // tq, height, tq), pl.BlockSpec((tm // tq, height, tq), lambda i: (i, 0, 0))

    n_cmp_cols = 2 * NSA_KV_WIDTH // HEAD_DIM
    cmp_out = (n_cmp_cols, t, HEAD_DIM), pl.BlockSpec((n_cmp_cols, tm, HEAD_DIM), lambda i: (0, i, 0))
    b = t // seq
    mla_q = ((b, seq // tq_mla, MLA_HEADS, MLA_QK_PAD, tq_mla),
             pl.BlockSpec((1, tm // tq_mla, MLA_HEADS, MLA_QK_PAD, tq_mla), lambda i: (i // nj, i % nj, 0, 0, 0)))
    mla_k = ((b, MLA_HEADS, seq, MLA_QK_PAD),
             pl.BlockSpec((1, MLA_HEADS, tm, MLA_QK_PAD), lambda i: (i // nj, 0, i % nj, 0)))
    mla_v = ((b, MLA_HEADS, MLA_V_DIM, seq),
             pl.BlockSpec((1, MLA_HEADS, MLA_V_DIM, tm), lambda i: (i // nj, 0, 0, i % nj)))
    outs = [(q_tiles(NSA_WIDTH), BF16), (cmp_out, F32), (rows(NSA_KV_WIDTH), BF16),
            (cols(NSA_KV_WIDTH), BF16), (rows(NSA_KV_WIDTH), BF16), (cols(NSA_KV_WIDTH), BF16),
            (q_tiles(LANES), F32), (mla_q, BF16), (mla_k, BF16), (mla_v, BF16)]
    mla_in = (cos, sin, cos.T, sin.T) + tuple(mla_weights) + mla_q_gains + tuple(mla_k_gains)
    mla_specs = [pos_rows, pos_rows, pos_cols, pos_cols] + [full(a) for a in mla_in[4:]]
    return pl.pallas_call(
        functools.partial(_proj_kernel, nb_first=nb_first),
        grid=(t // tm,),
        in_specs=x_specs + [full(gain), full(w_n), full(w_t), full(q_gain), full(k_gains), full(qa_gain),
                            full(kva_gain)] + mla_specs,
        out_specs=[spec for (_, spec), _ in outs],
        out_shape=[jax.ShapeDtypeStruct(shape, dt) for (shape, _), dt in outs],
        compiler_params=_cparams("proj", ("parallel",)),
        name="proj",
    )(*x_parts, gain, w_n, w_t, q_gain, k_gains, qa_gain, kva_gain, *mla_in)


def _gelu_tanh(x):
    return 0.5 * x * (1.0 + jnp.tanh(np.sqrt(2.0 / np.pi) * (x + 0.044715 * (x * x * x))))


def _compress_kernel(x_ref, w1k_ref, w2k_ref, pk_ref, w1v_ref, w2v_ref, pv_ref, kg_ref, kc_ref, vc_t_ref):
    half = CMP_STRIDE * HEAD_DIM
    nrow = x_ref.shape[1] // CMP_STRIDE
    for which, (w1_ref, w2_ref, p_ref) in enumerate(((w1k_ref, w2k_ref, pk_ref), (w1v_ref, w2v_ref, pv_ref))):
        w1 = w1_ref[...].astype(BF16)
        w2 = w2_ref[...].astype(BF16)
        pos_bias = _dot(jnp.broadcast_to(p_ref[...], (8, 2 * half)).astype(BF16), w1)[0:1, :]
        for g in range(NSA_KV_HEADS):
            col = which * NSA_KV_HEADS + g
            xg = jnp.concatenate(
                [x_ref[col, pl.ds(l, nrow, stride=CMP_STRIDE), :].astype(BF16) for l in range(CMP_STRIDE)], axis=1)
            first = _dot(xg, w1[0:half, :])
            second = _dot(xg, w1[half:2 * half, :])
            pre = first + pltpu.roll(second, nrow - 1, 0) + pos_bias
            hidden = _gelu_tanh(pre).astype(BF16)
            if which == 0:
                kc_ref[0, g] = _rms(_dot(hidden, w2), kg_ref[0:1, :]).astype(BF16)
            else:
                vc_t_ref[0, g] = _dot_nt(w2, hidden).astype(BF16)


def _compress(xc, seq, w1k, w2k, pk, w1v, w2v, pv, k_gains):
    ncol, t, width = xc.shape
    b, nrow = t // seq, seq // CMP_STRIDE

    def full(a):
        return pl.BlockSpec(a.shape, lambda i: (0,) * a.ndim)

    out_sds = jax.ShapeDtypeStruct((b, NSA_KV_HEADS, nrow, HEAD_DIM), BF16)
    out_spec = pl.BlockSpec((1, NSA_KV_HEADS, nrow, HEAD_DIM), lambda i: (i, 0, 0, 0))
    return pl.pallas_call(
        _compress_kernel,
        grid=(b,),
        in_specs=[pl.BlockSpec((ncol, seq, width), lambda i: (0, i, 0)),
                  full(w1k), full(w2k), full(pk), full(w1v), full(w2v), full(pv), full(k_gains)],
        out_specs=[out_spec, out_spec],
        out_shape=[out_sds, out_sds],
        compiler_params=_cparams("compress", ("parallel",)),
        name="compress",
    )(xc, w1k, w2k, pk, w1v, w2v, pv, k_gains)


def _softmax_cols(s):
    p = jnp.exp(s - jnp.max(s, axis=0, keepdims=True))
    return jnp.sum(p, axis=0, keepdims=True), p


def _nsa_kernel(q_t_ref, kc_ref, vc_t_ref, ks_ref, vs_t_ref, kw_ref, vw_t_ref, gate_t_ref, kfeat_ref, cfeat_ref,
                slope_ref, o_ref, *, tq, ck):
    qi = pl.program_id(1)
    q0 = qi * tq
    ng = NSA_KV_HEADS
    wide = NSA_GROUP * tq
    n_sel = ks_ref.shape[1] // SEL_BLOCK
    ncmp = kc_ref.shape[2]
    n_win = WINDOW + tq

    def tile4(a):
        return jnp.concatenate([a] * NSA_GROUP, axis=1)

    def masked(s, mask, fill=NEG):
        return jnp.concatenate([jnp.where(mask, s[:, r * tq:(r + 1) * tq], fill) for r in range(NSA_GROUP)], axis=1)

    def q_aug(g, extra):
        q = jnp.concatenate([q_t_ref[0, (g * NSA_GROUP + r) * HEAD_DIM:(g * NSA_GROUP + r + 1) * HEAD_DIM, :]
                             for r in range(NSA_GROUP)], axis=1)
        rows = [q, slope_ref[g]]
        used = slope_ref.shape[1]
        if extra is not None:
            rows.append(extra)
            used += extra.shape[0]
        rows.append(jnp.zeros((HEAD_DIM - used, wide), BF16))
        return jnp.concatenate(rows, axis=0)

    def gsl(g):
        return slice(g * HEAD_DIM, (g + 1) * HEAD_DIM)

    lane_q = q0 + lax.broadcasted_iota(jnp.int32, (1, tq), 1)

    qa_plain = [q_aug(g, None) for g in range(ng)]
    w0 = pl.multiple_of(jnp.maximum(q0 - WINDOW, 0), tq)
    s_cmp = [_dot(jnp.concatenate([kc_ref[0, g], cfeat_ref[...]], axis=1), qa_plain[g]) for g in range(ng)]
    s_win = [_dot(jnp.concatenate([kw_ref[0, pl.ds(w0, n_win), gsl(g)], kfeat_ref[pl.ds(w0, n_win), :]], axis=1),
                  qa_plain[g]) for g in range(ng)]

    n_row = lax.broadcasted_iota(jnp.int32, (ncmp, tq), 0)
    mask_c = n_row * CMP_STRIDE + (CMP_BLOCK - 1) <= lane_q
    s_row = lax.broadcasted_iota(jnp.int32, (n_sel, ncmp), 0)
    n_col = lax.broadcasted_iota(jnp.int32, (n_sel, ncmp), 1)
    overlap_t = jnp.where((n_col * CMP_STRIDE < (s_row + 1) * SEL_BLOCK)
                          & (n_col * CMP_STRIDE + (CMP_BLOCK - 1) >= s_row * SEL_BLOCK), 1.0, 0.0).astype(BF16)
    p_cmp, imp = [], []
    for g in range(ng):
        s = masked(s_cmp[g], mask_c)
        m = jnp.max(s, axis=0, keepdims=True)
        p = masked(jnp.exp(s - m), mask_c, 0.0)
        l = jnp.sum(p, axis=0, keepdims=True)
        p = p / jnp.where(l > 0.0, l, 1.0)
        p_cmp.append(p.astype(BF16))
        p_sum = p[:, 0:tq]
        for r in range(1, NSA_GROUP):
            p_sum = p_sum + p[:, r * tq:(r + 1) * tq]
        p_hi = p_sum.astype(BF16)
        p_lo = (p_sum - p_hi.astype(F32)).astype(BF16)
        imp.append(_dot(overlap_t, p_hi) + _dot(overlap_t, p_lo))
    o_cmp = [_dot(vc_t_ref[0, g], p_cmp[g]) for g in range(ng)]

    blk = lax.broadcasted_iota(jnp.int32, (n_sel, tq), 0)
    blk_f = blk.astype(F32)
    forced = (blk == 0) | (blk == jnp.right_shift(lane_q, SEL_SHIFT))
    future = blk * SEL_BLOCK > lane_q
    penalty = []
    for g in range(ng):
        v = jnp.where(future, NEG, jnp.where(forced, FORCE_SCORE, imp[g]))
        unselected = jnp.full((n_sel, tq), NEG, F32)
        for _ in range(min(SEL_TOP, n_sel)):
            mx = jnp.max(v, axis=0, keepdims=True)
            first = jnp.min(jnp.where(v == mx, blk_f, float(n_sel)), axis=0, keepdims=True)
            pick = blk_f == first
            unselected = jnp.where(pick, 0.0, unselected)
            v = jnp.where(pick, -jnp.inf, v)
        penalty.append(tile4(unselected.astype(BF16)))

    rel = lax.broadcasted_iota(jnp.int32, (n_win, tq), 0) - lax.broadcasted_iota(jnp.int32, (n_win, tq), 1)
    off = q0 - w0
    mask_w = (rel <= off) & (rel > off - WINDOW)
    o_win = []
    p_win = []
    for g in range(ng):
        l, p = _softmax_cols(masked(s_win[g], mask_w))
        p_win.append((l, p.astype(BF16)))
    for g in range(ng):
        l, p = p_win[g]
        o_win.append(_dot(vw_t_ref[gsl(g), pl.ds(w0, n_win)], p) / l)

    partial = []
    for g in range(ng):
        for r in range(NSA_GROUP):
            hd = g * NSA_GROUP + r
            cs = slice(r * tq, (r + 1) * tq)
            partial.append(gate_t_ref[0, 3 * hd:3 * hd + 1, :] * o_cmp[g][:, cs]
                           + gate_t_ref[0, 3 * hd + 2:3 * hd + 3, :] * o_win[g][:, cs])

    qa_sel = [q_aug(g, penalty[g]) for g in range(ng)]
    rel_d = lax.broadcasted_iota(jnp.int32, (ck, tq), 0) - lax.broadcasted_iota(jnp.int32, (ck, tq), 1)

    def select_and_store(n_before):
        nk = n_before + ck
        mask_d = rel_d <= q0 - n_before
        ss = [_dot(jnp.concatenate([ks_ref[0, 0:nk, gsl(g)], kfeat_ref[0:nk, :]], axis=1), qa_sel[g])
              for g in range(ng)]
        for g in range(ng):
            s_diag = masked(ss[g][n_before:nk], mask_d)
            m = jnp.max(s_diag, axis=0, keepdims=True)
            if n_before:
                m = jnp.maximum(m, jnp.max(ss[g][0:n_before], axis=0, keepdims=True))
            p_diag = jnp.exp(s_diag - m)
            l = jnp.sum(p_diag, axis=0, keepdims=True)
            p = p_diag.astype(BF16)
            if n_before:
                p_before = jnp.exp(ss[g][0:n_before] - m)
                l = l + jnp.sum(p_before, axis=0, keepdims=True)
                p = jnp.concatenate([p_before.astype(BF16), p], axis=0)
            o_sel = _dot(vs_t_ref[gsl(g), 0:nk], p) / l
            for r in range(NSA_GROUP):
                hd = g * NSA_GROUP + r
                mixed = partial[hd] + gate_t_ref[0, 3 * hd + 1:3 * hd + 2, :] * o_sel[:, r * tq:(r + 1) * tq]
                o_ref[0, :, hd * HEAD_DIM:(hd + 1) * HEAD_DIM] = mixed.T

    per = ck // tq
    for v in range(ks_ref.shape[1] // ck):
        pl.when(qi // per == v)(functools.partial(select_and_store, v * ck))


def _nsa(q_t, kc, vc_t, ks, vs_t, kw, vw_t, gate_t, *, tq=NSA_Q_TILE, ck=256):
    b, s, _ = ks.shape
    nq = s // tq
    pos = np.arange(s)
    kfeat = np.zeros((s, LANES), np.float32)
    kfeat[:, 0] = POS_SPLIT * (pos // POS_SPLIT)
    kfeat[:, 1] = pos % POS_SPLIT
    kfeat[pos, FEAT_ROWS + pos // SEL_BLOCK] = 1.0
    ncmp = kc.shape[2]
    cfeat = np.zeros((ncmp, LANES), np.float32)
    cfeat[:, 0] = CMP_STRIDE * np.arange(ncmp)
    cfeat[:, 1] = 0.5 * (CMP_BLOCK - 1)
    slope = np.zeros((NSA_KV_HEADS, FEAT_ROWS, NSA_GROUP * tq), np.float32)
    for hd in range(NSA_HEADS):
        g, r = divmod(hd, NSA_GROUP)
        slope[g, 0:2, r * tq:(r + 1) * tq] = 2.0 ** (-8.0 * (hd + 1) / NSA_HEADS)
    assert FEAT_ROWS + s // SEL_BLOCK <= LANES and s // SEL_BLOCK % FEAT_ROWS == 0 and s % ck == 0 and ck % tq == 0

    def full(a):
        return pl.BlockSpec(a.shape, lambda i, j: (0,) * a.ndim)

    def per_b(a):
        return pl.BlockSpec((1,) + a.shape[1:], lambda i, j: (i,) + (0,) * (a.ndim - 1))

    def per_b_cols(a):
        return pl.BlockSpec((a.shape[0], s), lambda i, j: (0, i))

    def q_cols(a):
        assert a.shape[0] == b * nq and a.shape[2] == tq
        return pl.BlockSpec((1, a.shape[1], tq), lambda i, j: (i * nq + j, 0, 0))

    consts = [jnp.asarray(kfeat, BF16), jnp.asarray(cfeat, BF16), jnp.asarray(slope, BF16)]
    return pl.pallas_call(
        functools.partial(_nsa_kernel, tq=tq, ck=ck),
        grid=(b, nq),
        in_specs=[q_cols(q_t), per_b(kc), per_b(vc_t), per_b(ks), per_b_cols(vs_t), per_b(kw), per_b_cols(vw_t),
                  q_cols(gate_t)] + [full(c) for c in consts],
        out_specs=pl.BlockSpec((1, tq, NSA_WIDTH), lambda i, j: (i, j, 0)),
        out_shape=jax.ShapeDtypeStruct((b, s, NSA_WIDTH), F32),
        compiler_params=_cparams("nsa", ("parallel", "arbitrary")),
        name="nsa",
    )(q_t, kc, vc_t, ks, vs_t, kw, vw_t, gate_t, *consts)


def _mla_attn_kernel(q_t_ref, k_ref, v_t_ref, o_ref, *, tq, hb, n_q):
    qi = pl.program_id(2)
    d_v = v_t_ref.shape[2]
    tri = lax.broadcasted_iota(jnp.int32, (tq, tq), 0) <= lax.broadcasted_iota(jnp.int32, (tq, tq), 1)

    def attend(n_before):
        nk = n_before + tq
        ss = [_dot(k_ref[0, h, 0:nk, :], q_t_ref[0, 0, h]) for h in range(hb)]
        for h in range(hb):
            s_diag = jnp.where(tri, ss[h][n_before:nk], NEG)
            m = jnp.max(s_diag, axis=0, keepdims=True)
            if n_before:
                m = jnp.maximum(m, jnp.max(ss[h][0:n_before], axis=0, keepdims=True))
            p_diag = jnp.exp(s_diag - m)
            l = jnp.sum(p_diag, axis=0, keepdims=True)
            p = p_diag.astype(BF16)
            if n_before:
                p_before = jnp.exp(ss[h][0:n_before] - m)
                l = l + jnp.sum(p_before, axis=0, keepdims=True)
                p = jnp.concatenate([p_before.astype(BF16), p], axis=0)
            o_ref[0, :, h * d_v:(h + 1) * d_v] = (_dot(v_t_ref[0, h, :, 0:nk], p) / l).T

    for v in range(n_q):
        pl.when(qi == v)(functools.partial(attend, v * tq))


def _mla_attn(q_t, k, v_t, *, tq=MLA_Q_TILE, hb=MLA_HEADS):
    b, h, s, dq = k.shape
    dv = v_t.shape[2]
    assert q_t.shape == (b, s // tq, h, dq, tq)
    return pl.pallas_call(
        functools.partial(_mla_attn_kernel, tq=tq, hb=hb, n_q=s // tq),
        grid=(b, h // hb, s // tq),
        in_specs=[pl.BlockSpec((1, 1, hb, dq, tq), lambda i, j, l: (i, l, j, 0, 0)),
                  pl.BlockSpec((1, hb, s, dq), lambda i, j, l: (i, j, 0, 0)),
                  pl.BlockSpec((1, hb, dv, s), lambda i, j, l: (i, j, 0, 0))],
        out_specs=pl.BlockSpec((1, tq, hb * dv), lambda i, j, l: (i, l, j)),
        out_shape=jax.ShapeDtypeStruct((b, s, h * dv), F32),
        compiler_params=_cparams("mla_attn", ("parallel", "parallel", "arbitrary")),
        name="mla_attn",
    )(q_t, k, v_t)


def _out_kernel(xa_ref, xb_ref, oa_ref, ob_ref, ga_ref, gb_ref, w_ref, o_ref, *, nb_first):
    oa = _rms(oa_ref[...], ga_ref[...]).astype(BF16)
    ob = _rms(ob_ref[...], gb_ref[...]).astype(BF16)
    o_ref[...] = (_two_part_rows(xa_ref, xb_ref, nb_first)
                  + _dot(jnp.concatenate([oa, ob], axis=1), w_ref[...]))


def _out_proj(x_parts, oa, ob, gain_a, gain_b, w, *, tm=512):
    t, d = oa.shape[0], x_parts[0].shape[1]
    nb_first, x_specs = _two_part_specs(x_parts, tm)

    def rows(width):
        return pl.BlockSpec((tm, width), lambda i: (i, 0))

    def full(a):
        return pl.BlockSpec(a.shape, lambda i: (0,) * a.ndim)

    return pl.pallas_call(
        functools.partial(_out_kernel, nb_first=nb_first),
        grid=(t // tm,),
        in_specs=x_specs + [rows(oa.shape[1]), rows(ob.shape[1]), full(gain_a), full(gain_b), full(w)],
        out_specs=rows(d),
        out_shape=jax.ShapeDtypeStruct((t, d), F32),
        compiler_params=_cparams("out_proj", ("parallel",)),
        name="out_proj",
    )(*x_parts, oa, ob, gain_a, gain_b, w)


def _rot_cols(w):
    half = w.shape[1] // 2
    return jnp.concatenate([-w[:, half:], w[:, :half]], axis=1)


def _layout_w_in(w_in):
    d = w_in.shape[0]
    off = np.cumsum([0, NSA_WIDTH] + [NSA_KV_WIDTH] * 6 + [NSA_HEADS * 3, MLA_Q_RANK, MLA_KV_RANK, MLA_ROPE_DIM])
    q, kc, vc, ks, vs, kw, vw, gt, cq, ckv, kr = [w_in[:, off[i]:off[i + 1]] for i in range(11)]
    zr = jnp.zeros((d, LANES - MLA_ROPE_DIM), w_in.dtype)
    zg = jnp.zeros((d, LANES - NSA_HEADS * 3), w_in.dtype)
    w_n = jnp.concatenate([kc, vc, ks, kw, cq, ckv, kr, zr, _rot_cols(kr), zr], axis=1).astype(BF16)
    w_t = jnp.concatenate([q, vs, vw, gt, zg], axis=1).astype(BF16).T
    return w_n, w_t


def _layout_mla_weights(w_uq, w_ukv):
    r = w_uq.shape[0]
    zr = jnp.zeros((r, LANES - MLA_ROPE_DIM), w_uq.dtype)
    main, rot, k_cols, v_cols = [], [], [], []
    for hd in range(MLA_HEADS):
        base = hd * MLA_QK_DIM
        rope = w_uq[:, base + MLA_NOPE_DIM:base + MLA_QK_DIM]
        main += [w_uq[:, base:base + MLA_NOPE_DIM], rope, zr]
        rot += [_rot_cols(rope), zr]
        kv = hd * (MLA_NOPE_DIM + MLA_V_DIM)
        k_cols.append(w_ukv[:, kv:kv + MLA_NOPE_DIM])
        v_cols.append(w_ukv[:, kv + MLA_NOPE_DIM:kv + MLA_NOPE_DIM + MLA_V_DIM])
    cat = lambda parts: jnp.concatenate(parts, axis=1).astype(BF16)
    return cat(main).T, cat(rot).T, cat(k_cols), cat(v_cols).T


def _rope_gains(gain):
    half = MLA_ROPE_DIM // 2
    zr = jnp.zeros((LANES - MLA_ROPE_DIM,), gain.dtype)
    g_rope = gain[MLA_NOPE_DIM:]
    return (gain[None, :MLA_NOPE_DIM], jnp.concatenate([g_rope, zr])[None, :],
            jnp.concatenate([g_rope[half:], g_rope[:half], zr])[None, :])


def _rope_tables(s):
    half = MLA_ROPE_DIM // 2
    inv = ROPE_THETA ** (-jnp.arange(half, dtype=F32) / half)
    ang = jnp.arange(s).astype(F32)[:, None] * inv[None, :]
    zr = jnp.zeros((s, LANES - MLA_ROPE_DIM), F32)
    cos, sin = jnp.cos(ang), jnp.sin(ang)
    return jnp.concatenate([cos, cos, zr], axis=1), jnp.concatenate([sin, sin, zr], axis=1)


def kernel(x, ffn1_norm, ffn1_w_gate, ffn1_w_up, ffn1_w_down, mix_norm, w_in, nsa_q_norm, nsa_k_norm, nsa_cmp_pos_k, nsa_cmp_w1_k, nsa_cmp_w2_k, nsa_cmp_pos_v, nsa_cmp_w1_v, nsa_cmp_w2_v, mla_q_a_norm, mla_w_uq, mla_kv_a_norm, mla_w_ukv, mla_q_norm, mla_k_norm, out_norm_nsa, out_norm_mla, w_out, ffn2_norm, ffn2_w_gate, ffn2_w_up, ffn2_w_down):
    b, s, d = x.shape
    depth = ffn1_norm.shape[0]
    tables = _rope_tables(s)
    xt = x.reshape(b * s, d)
    for l in range(depth):
        x_parts = _ffn(xt, ffn1_norm[l][None, :], ffn1_w_gate[l], ffn1_w_up[l], ffn1_w_down[l], in_place=False)

        w_n, w_t = _layout_w_in(w_in[l])
        q_t, kv_cmp, ks, vs_t, kw, vw_t, gate_t, q_m, k_m, v_m = _proj(
            x_parts, s, mix_norm[l][None, :], w_n, w_t, nsa_q_norm[l], nsa_k_norm[l],
            mla_q_a_norm[l][None, :], mla_kv_a_norm[l][None, :],
            tables, _layout_mla_weights(mla_w_uq[l], mla_w_ukv[l]),
            _rope_gains(mla_q_norm[l]), _rope_gains(mla_k_norm[l]))

        o_b = _mla_attn(q_m, k_m, v_m)

        kc, vc_t = _compress(
            kv_cmp, s,
            nsa_cmp_w1_k[l], nsa_cmp_w2_k[l], nsa_cmp_pos_k[l].reshape(1, -1),
            nsa_cmp_w1_v[l], nsa_cmp_w2_v[l].T, nsa_cmp_pos_v[l].reshape(1, -1),
            nsa_k_norm[l])

        def bs(a):
            return a.reshape(b, s, a.shape[-1])

        o_a = _nsa(q_t, kc, vc_t, bs(ks), vs_t, bs(kw), vw_t, gate_t)

        xt = _out_proj(x_parts, o_a.reshape(b * s, NSA_WIDTH), o_b.reshape(b * s, MLA_WIDTH),
                       out_norm_nsa[l][None, :], out_norm_mla[l][None, :], w_out[l].astype(BF16))

        xt = _ffn(xt, ffn2_norm[l][None, :], ffn2_w_gate[l], ffn2_w_up[l], ffn2_w_down[l], in_place=True)
    return xt.reshape(b, s, d)
```
